```python
import jax
import jax.numpy as jnp
from jax import lax
import numpy as np


D_MODEL = 2048
BATCH = 4
SEQ = 8192
DEPTH = 4

GRID_W = 64
CTX_LEN = 256
D_FF = 4 * D_MODEL
ROPE_BASE = 10000.0
NORM_EPS = 1e-6
Q_BLOCK = 128

MLA_HEADS = 4
MLA_Q_RANK = 512
MLA_KV_RANK = 256
MLA_NOPE = 128
MLA_ROPE = 64
MLA_V = 128

SWA_HEADS = 16
SWA_KV_HEADS = 2
SWA_HEAD_DIM = 64
WINDOW = 128
SWA_BLOCK = 128

MLSTM_HEADS = 4
MLSTM_QK = 64
MLSTM_V = 128
MLSTM_CHUNK = 64

D_MIX = MLA_HEADS * MLA_V + SWA_HEADS * SWA_HEAD_DIM + MLSTM_HEADS * MLSTM_V
IN_WIDTHS = (MLA_Q_RANK, MLA_KV_RANK, MLA_ROPE,
             SWA_HEADS * SWA_HEAD_DIM, SWA_KV_HEADS * SWA_HEAD_DIM, SWA_KV_HEADS * SWA_HEAD_DIM,
             MLSTM_HEADS * MLSTM_QK, MLSTM_HEADS * MLSTM_QK, MLSTM_HEADS * MLSTM_V,
             4 * MLSTM_HEADS, MLSTM_HEADS * MLSTM_V)
D_IN = sum(IN_WIDTHS)

kernel_name = 'hybrid_mla_swa_mlstm_dit_trunk'

F32 = jnp.float32


def rmsnorm(x, g):
    xf = x.astype(F32)
    y = xf * lax.rsqrt(jnp.mean(xf * xf, axis=-1, keepdims=True) + NORM_EPS)
    return y.astype(x.dtype) * g


def modulate_norm(x, g, shift, scale):
    return rmsnorm(x, g) * (1 + scale) + shift


def split_in(z):
    offs = [int(o) for o in np.cumsum(IN_WIDTHS)[:-1]]
    return jnp.split(z, offs, axis=-1)


def rope_tables(row, col, dim):
    half = dim // 2
    inv = ROPE_BASE ** (-jnp.arange(0, half, 2, dtype=F32) / half)
    ar = row[:, None].astype(F32) * inv
    ac = col[:, None].astype(F32) * inv
    ang = jnp.concatenate([ar, ar, ac, ac], axis=-1)
    return jnp.cos(ang), jnp.sin(ang)


def apply_rope(x, cos, sin):
    x1, x2, x3, x4 = jnp.split(x, 4, axis=-1)
    rot = jnp.concatenate([-x2, x1, -x4, x3], axis=-1)
    return x * cos[:, None].astype(x.dtype) + rot * sin[:, None].astype(x.dtype)


def softmax_with_sink(logits, sink):
    sink = jnp.broadcast_to(sink, logits.shape[:-1] + (1,))
    return jax.nn.softmax(jnp.concatenate([sink, logits], axis=-1), axis=-1)[..., 1:]


def mla_qkv(zq, zkv, zr, g_q, w_uq, g_kv, w_ukv, rope):
    B, T, _ = zq.shape
    q = (rmsnorm(zq, g_q) @ w_uq).reshape(B, T, MLA_HEADS, MLA_NOPE + MLA_ROPE)
    q_nope, q_rope = q[..., :MLA_NOPE], q[..., MLA_NOPE:]
    kv = (rmsnorm(zkv, g_kv) @ w_ukv).reshape(B, T, MLA_HEADS, MLA_NOPE + MLA_V)
    k_nope, v = kv[..., :MLA_NOPE], kv[..., MLA_NOPE:]
    k_rope = zr[:, :, None, :]
    if rope is not None:
        cos, sin = rope
        q_rope = apply_rope(q_rope, cos, sin)
        k_rope = apply_rope(k_rope, cos, sin)
    return q_nope, q_rope, k_nope, k_rope, v


def mla_attend(q_nope, q_rope, k_nope, k_rope, v):
    B, T, H, _ = q_nope.shape
    nb = T // Q_BLOCK
    scale = (MLA_NOPE + MLA_ROPE) ** -0.5
    kr = k_rope[:, :, 0]

    def blk(args):
        qn, qr = args
        s = jnp.einsum('bqhd,bkhd->bhqk', qn, k_nope) + jnp.einsum('bqhd,bkd->bhqk', qr, kr)
        p = jax.nn.softmax(s.astype(F32) * scale, axis=-1).astype(v.dtype)
        return jnp.einsum('bhqk,bkhd->bqhd', p, v)

    qn_b = q_nope.reshape(B, nb, Q_BLOCK, H, MLA_NOPE).swapaxes(0, 1)
    qr_b = q_rope.reshape(B, nb, Q_BLOCK, H, MLA_ROPE).swapaxes(0, 1)
    out = lax.map(blk, (qn_b, qr_b))
    return out.swapaxes(0, 1).reshape(B, T, H * MLA_V)


def gqa_sink_dense(q, k, v, sink):
    B, T, Hq, d = q.shape
    G = k.shape[2]
    R = Hq // G
    qg = q.reshape(B, T, G, R, d)
    s = jnp.einsum('bqgrd,bkgd->bgrqk', qg, k).astype(F32) * (d ** -0.5)
    p = softmax_with_sink(s, sink.reshape(G, R)[None, :, :, None, None].astype(F32)).astype(v.dtype)
    return jnp.einsum('bgrqk,bkgd->bqgrd', p, v).reshape(B, T, Hq * d)


def swa_latent(q, k, v, kc, vc, sink):
    B, S, Hq, d = q.shape
    G = SWA_KV_HEADS
    R = Hq // G
    Lc = kc.shape[1]
    nb = S // SWA_BLOCK
    span = SWA_BLOCK + 2 * WINDOW
    pad = ((0, 0), (WINDOW, WINDOW), (0, 0), (0, 0))
    kp = jnp.pad(k, pad)
    vp = jnp.pad(v, pad)
    qb = q.reshape(B, nb, SWA_BLOCK, G, R, d).swapaxes(0, 1)
    qi = jnp.arange(SWA_BLOCK)[:, None]
    ki = jnp.arange(span)[None, :]
    rel = ki - WINDOW - qi
    scale = d ** -0.5
    sink_b = sink.reshape(G, R)[None, :, :, None, None].astype(F32)

    def blk(args):
        j, qj = args
        kj = lax.dynamic_slice_in_dim(kp, j * SWA_BLOCK, span, axis=1)
        vj = lax.dynamic_slice_in_dim(vp, j * SWA_BLOCK, span, axis=1)
        u = j * SWA_BLOCK - WINDOW + ki
        valid = (jnp.abs(rel) <= WINDOW) & (u >= 0) & (u < S)
        s_loc = jnp.einsum('bqgrd,bkgd->bgrqk', qj, kj).astype(F32) * scale
        s_loc = jnp.where(valid, s_loc, -jnp.inf)
        s_ctx = jnp.einsum('bqgrd,bkgd->bgrqk', qj, kc).astype(F32) * scale
        p = softmax_with_sink(jnp.concatenate([s_ctx, s_loc], axis=-1), sink_b).astype(v.dtype)
        return (jnp.einsum('bgrqk,bkgd->bqgrd', p[..., :Lc], vc)
                + jnp.einsum('bgrqk,bkgd->bqgrd', p[..., Lc:], vj))

    out = lax.map(blk, (jnp.arange(nb), qb))
    return out.swapaxes(0, 1).reshape(B, S, Hq * d)


def mlstm_heads(zq, zk, zv, zg, gate_bias):
    B, T, _ = zq.shape
    q = zq.reshape(B, T, MLSTM_HEADS, MLSTM_QK).transpose(0, 2, 1, 3).astype(F32) * (MLSTM_QK ** -0.5)
    k = zk.reshape(B, T, MLSTM_HEADS, MLSTM_QK).transpose(0, 2, 1, 3).astype(F32)
    v = zv.reshape(B, T, MLSTM_HEADS, MLSTM_V).transpose(0, 2, 1, 3).astype(F32)
    g = (zg.reshape(B, T, 4, MLSTM_HEADS).astype(F32) + gate_bias.astype(F32)).transpose(2, 0, 3, 1)
    return q, k, v, g


def mlstm_zero_state(B):
    return (jnp.zeros((B, MLSTM_HEADS, MLSTM_QK, MLSTM_V), F32),
            jnp.zeros((B, MLSTM_HEADS, MLSTM_QK), F32),
            jnp.zeros((B, MLSTM_HEADS), F32))


def mlstm_chunkwise(q, k, v, i_pre, f_pre, state):
    B, H, T, dk = q.shape
    dv = v.shape[-1]
    L = MLSTM_CHUNK
    N = T // L
    qc = q.reshape(B, H, N, L, dk)
    kc = k.reshape(B, H, N, L, dk)
    vc = v.reshape(B, H, N, L, dv)
    ig = i_pre.reshape(B, H, N, L)
    b = jnp.cumsum(jax.nn.log_sigmoid(f_pre).reshape(B, H, N, L), axis=-1)
    g = b[..., -1]
    a = g[..., None] - b + ig
    a_max = jnp.max(a, axis=-1)
    w = jnp.exp(a - a_max[..., None])
    dC = jnp.einsum('bhnl,bhnld,bhnle->bhnde', w, kc, vc)
    dn = jnp.einsum('bhnl,bhnld->bhnd', w, kc)

    def step(carry, inp):
        C, n, m = carry
        g_c, am_c, dC_c, dn_c = inp
        m_new = jnp.maximum(g_c + m, am_c)
        decay = jnp.exp(g_c + m - m_new)
        grow = jnp.exp(am_c - m_new)
        C_new = decay[..., None, None] * C + grow[..., None, None] * dC_c
        n_new = decay[..., None] * n + grow[..., None] * dn_c
        return (C_new, n_new, m_new), (C, n, m)

    to_t = lambda t: jnp.moveaxis(t, 2, 0)
    final, (C_in, n_in, m_in) = lax.scan(step, state, (to_t(g), to_t(a_max), to_t(dC), to_t(dn)))
    C_in = jnp.moveaxis(C_in, 0, 2)
    n_in = jnp.moveaxis(n_in, 0, 2)
    m_in = jnp.moveaxis(m_in, 0, 2)
    causal = jnp.tril(jnp.ones((L, L), dtype=bool))
    d_log = jnp.where(causal, b[..., :, None] - b[..., None, :] + ig[..., None, :], -jnp.inf)
    inter_log = b + m_in[..., None]
    m_t = jnp.maximum(inter_log, jnp.max(d_log, axis=-1))
    s = jnp.einsum('bhnld,bhnsd->bhnls', qc, kc) * jnp.exp(d_log - m_t[..., None])
    inter_w = jnp.exp(inter_log - m_t)
    num = (jnp.einsum('bhnls,bhnse->bhnle', s, vc)
           + inter_w[..., None] * jnp.einsum('bhnld,bhnde->bhnle', qc, C_in))
    den = jnp.sum(s, axis=-1) + inter_w * jnp.einsum('bhnld,bhnd->bhnl', qc, n_in)
    h = num / jnp.maximum(jnp.abs(den), jnp.exp(-m_t))[..., None]
    return h.reshape(B, H, T, dv), final


def mlstm_bidir(q, k, v, g, init_f, init_b):
    h_f, st_f = mlstm_chunkwise(q, k, v, g[0], g[1], init_f)
    fl = lambda t: jnp.flip(t, axis=2)
    h_b, st_b = mlstm_chunkwise(fl(q), fl(k), fl(v), jnp.flip(g[2], axis=-1), jnp.flip(g[3], axis=-1), init_b)
    return h_f + fl(h_b), st_f, st_b


def mlstm_out(hm, zo, g_h, dtype):
    B, H, T, dv = hm.shape
    hn = rmsnorm(hm.transpose(0, 2, 1, 3), g_h.astype(F32))
    return (hn.reshape(B, T, H * dv) * jax.nn.sigmoid(zo.astype(F32))).astype(dtype)


def token_mixers(h, hc, w_in, mla_g_q, mla_w_uq, mla_g_kv, mla_w_ukv, swa_sink,
                 mlstm_gate_bias, mlstm_g_h, rope_mla, rope_swa, need_ctx):
    B = h.shape[0]
    z = split_in(h @ w_in)
    zc = split_in(hc @ w_in)

    qn, qr, kn, kr, vv = mla_qkv(z[0], z[1], z[2], mla_g_q, mla_w_uq, mla_g_kv, mla_w_ukv, rope_mla)
    qnc, qrc, knc, krc, vvc = mla_qkv(zc[0], zc[1], zc[2], mla_g_q, mla_w_uq, mla_g_kv, mla_w_ukv, None)
    y_mla = mla_attend(qn, qr, jnp.concatenate([knc, kn], axis=1),
                       jnp.concatenate([krc, kr], axis=1), jnp.concatenate([vvc, vv], axis=1))

    heads = lambda t, n: t.reshape(t.shape[0], t.shape[1], n, SWA_HEAD_DIM)
    cos_s, sin_s = rope_swa
    qs = apply_rope(heads(z[3], SWA_HEADS), cos_s, sin_s)
    ks = apply_rope(heads(z[4], SWA_KV_HEADS), cos_s, sin_s)
    vs = heads(z[5], SWA_KV_HEADS)
    qsc, ksc, vsc = heads(zc[3], SWA_HEADS), heads(zc[4], SWA_KV_HEADS), heads(zc[5], SWA_KV_HEADS)
    y_swa = swa_latent(qs, ks, vs, ksc, vsc, swa_sink)

    qm, km, vm, gm = mlstm_heads(z[6], z[7], z[8], z[9], mlstm_gate_bias)
    qmc, kmc, vmc, gmc = mlstm_heads(zc[6], zc[7], zc[8], zc[9], mlstm_gate_bias)
    zero = mlstm_zero_state(B)
    hmc, st_f, st_b = mlstm_bidir(qmc, kmc, vmc, gmc, zero, zero)
    hm, _, _ = mlstm_bidir(qm, km, vm, gm, st_f, st_b)
    y_mlstm = mlstm_out(hm, z[10], mlstm_g_h, h.dtype)

    y = jnp.concatenate([y_mla, y_swa, y_mlstm], axis=-1)
    if need_ctx:
        yc = jnp.concatenate([mla_attend(qnc, qrc, knc, krc, vvc),
                              gqa_sink_dense(qsc, ksc, vsc, swa_sink),
                              mlstm_out(hmc, zc[10], mlstm_g_h, hc.dtype)], axis=-1)
    else:
        yc = None
    return y, yc


def squared_relu_mlp(h, w1, w2):
    return jnp.square(jax.nn.relu(h @ w1)) @ w2


def setup_inputs(seed: int = 0) -> dict:
    key = jax.random.key(seed)
    ks = jax.random.split(key, 24)
    nrm = lambda k, shape, s: jax.random.normal(k, shape, F32) * s
    L, D = DEPTH, D_MODEL
    f_sel = jnp.array([0.0, 1.0, 0.0, 1.0], F32)[None, :, None]
    gate_bias = nrm(ks[14], (L, 4, MLSTM_HEADS), 0.1) + f_sel * jax.random.uniform(
        ks[15], (L, 4, MLSTM_HEADS), F32, minval=3.0, maxval=6.0)
    return {
        'x': nrm(ks[0], (BATCH, SEQ, D), 1.0),
        'c': nrm(ks[1], (BATCH, D), 1.0),
        'ctx': nrm(ks[2], (BATCH, CTX_LEN, D), 1.0),
        'c_ctx': nrm(ks[3], (D,), 1.0),
        'w_mod': nrm(ks[4], (L, D, 6 * D), 0.5 * D ** -0.5),
        'b_mod': nrm(ks[5], (L, 6 * D), 0.01),
        'g_norm1': 1.0 + nrm(ks[6], (L, D), 0.02),
        'g_norm2': 1.0 + nrm(ks[7], (L, D), 0.02),
        'w_in': nrm(ks[8], (L, D, D_IN), D ** -0.5),
        'mla_g_q': 1.0 + nrm(ks[9], (L, MLA_Q_RANK), 0.02),
        'mla_w_uq': nrm(ks[10], (L, MLA_Q_RANK, MLA_HEADS * (MLA_NOPE + MLA_ROPE)), MLA_Q_RANK ** -0.5),
        'mla_g_kv': 1.0 + nrm(ks[11], (L, MLA_KV_RANK), 0.02),
        'mla_w_ukv': nrm(ks[12], (L, MLA_KV_RANK, MLA_HEADS * (MLA_NOPE + MLA_V)), MLA_KV_RANK ** -0.5),
        'swa_sink': nrm(ks[13], (L, SWA_HEADS), 0.5),
        'mlstm_gate_bias': gate_bias,
        'mlstm_g_h': 1.0 + nrm(ks[16], (L, MLSTM_HEADS, MLSTM_V), 0.02),
        'w_out': nrm(ks[17], (L, D_MIX, D), D_MIX ** -0.5),
        'w_ff1': nrm(ks[18], (L, D, D_FF), D ** -0.5),
        'w_ff2': nrm(ks[19], (L, D_FF, D), D_FF ** -0.5),
        'g_final': 1.0 + nrm(ks[20], (D,), 0.02),
    }


def reference(x, c, ctx, c_ctx, w_mod, b_mod, g_norm1, g_norm2, w_in, mla_g_q, mla_w_uq,
              mla_g_kv, mla_w_ukv, swa_sink, mlstm_gate_bias, mlstm_g_h, w_out, w_ff1, w_ff2, g_final):
    B, S, D = x.shape
    rows = S // GRID_W
    row = jnp.repeat(jnp.arange(rows), GRID_W)
    col = jnp.tile(jnp.arange(GRID_W), rows)
    rope_mla = rope_tables(row, col, MLA_ROPE)
    rope_swa = rope_tables(row, col, SWA_HEAD_DIM)
    silu_c = jax.nn.silu(c)
    silu_cc = jax.nn.silu(c_ctx)
    xc = ctx
    for l in range(DEPTH):
        need_ctx = l < DEPTH - 1
        mod = silu_c @ w_mod[l] + b_mod[l]
        modc = silu_cc @ w_mod[l] + b_mod[l]
        sh1, sc1, gt1, sh2, sc2, gt2 = [m[:, None, :] for m in jnp.split(mod, 6, axis=-1)]
        sh1c, sc1c, gt1c, sh2c, sc2c, gt2c = jnp.split(modc, 6, axis=-1)
        h = modulate_norm(x, g_norm1[l], sh1, sc1)
        hc = modulate_norm(xc, g_norm1[l], sh1c, sc1c)
        y, yc = token_mixers(h, hc, w_in[l], mla_g_q[l], mla_w_uq[l], mla_g_kv[l], mla_w_ukv[l],
                             swa_sink[l], mlstm_gate_bias[l], mlstm_g_h[l], rope_mla, rope_swa, need_ctx)
        x = x + gt1 * (y @ w_out[l])
        x = x + gt2 * squared_relu_mlp(modulate_norm(x, g_norm2[l], sh2, sc2), w_ff1[l], w_ff2[l])
        if need_ctx:
            xc = xc + gt1c * (yc @ w_out[l])
            xc = xc + gt2c * squared_relu_mlp(modulate_norm(xc, g_norm2[l], sh2c, sc2c), w_ff1[l], w_ff2[l])
    return rmsnorm(x, g_final)
```

```python
import functools
import math

import jax
import jax.numpy as jnp
from jax import lax
from jax.experimental import pallas as pl
from jax.experimental.pallas import tpu as pltpu

F32 = jnp.float32
BF16 = jnp.bfloat16

GRID_W = 64
ROPE_BASE = 10000.0
NORM_EPS = 1e-6
MLA_HEADS = 4
MLA_Q_RANK = 512
MLA_KV_RANK = 256
MLA_NOPE = 128
MLA_ROPE = 64
MLA_V = 128
SWA_HEADS = 16
SWA_KV_HEADS = 2
SWA_HEAD_DIM = 64
WINDOW = 128
MLSTM_HEADS = 4
MLSTM_QK = 64
MLSTM_V = 128
N_GATES = 4 * MLSTM_HEADS

V7X_LANES = 128
V7X_VMEM_LIMIT_BYTES = 56 * 1024 * 1024

LOG2E = math.log2(math.e)
MLA_QSCALE = (MLA_NOPE + MLA_ROPE) ** -0.5 * LOG2E
SWA_QSCALE = SWA_HEAD_DIM ** -0.5 * LOG2E
MLSTM_QSCALE = MLSTM_QK ** -0.5
NEG_BIG = -1e30

C_ZQ = 0
C_ZKV = C_ZQ + MLA_Q_RANK
C_SQ = C_ZKV + MLA_KV_RANK
C_SK = C_SQ + SWA_HEADS * SWA_HEAD_DIM
C_SV = C_SK + SWA_KV_HEADS * SWA_HEAD_DIM
C_MQ = C_SV + SWA_KV_HEADS * SWA_HEAD_DIM
C_MK = C_MQ + MLSTM_HEADS * MLSTM_QK
C_MV = C_MK + MLSTM_HEADS * MLSTM_QK
C_MO = C_MV + MLSTM_HEADS * MLSTM_V
C_SMALL = C_MO + MLSTM_HEADS * MLSTM_V
C_END = C_SMALL + V7X_LANES
GATE_LANE0 = MLA_ROPE

MLA_QW = MLA_HEADS * 256
MLA_VW = MLA_HEADS * MLA_V
SWA_QW = SWA_HEADS * SWA_HEAD_DIM
SWA_KW = SWA_KV_HEADS * SWA_HEAD_DIM
MQW = MLSTM_HEADS * MLSTM_QK
MVW = MLSTM_HEADS * MLSTM_V


def _cparams(*sem):
    return pltpu.CompilerParams(dimension_semantics=sem, vmem_limit_bytes=V7X_VMEM_LIMIT_BYTES)


def _resident(shape):
    nd = len(shape)
    return pl.BlockSpec(shape, lambda *_: (0,) * nd, pipeline_mode=pl.Buffered(1))


def _dot(a, b):
    return jnp.dot(a, b, preferred_element_type=F32)


def _dot_nt(a, b):
    return lax.dot_general(a, b, (((1,), (1,)), ((), ())), preferred_element_type=F32)


def _rms(x):
    return x * lax.rsqrt(jnp.mean(x * x, axis=-1, keepdims=True) + NORM_EPS)


def _mod_kernel(c_ref, w_ref, b_ref, o_ref):
    c = c_ref[...]
    a = c * jax.nn.sigmoid(c)
    o_ref[...] = jnp.dot(a, w_ref[...], preferred_element_type=F32,
                         precision=lax.Precision.HIGHEST) + b_ref[...]


def _mod_all(c_all, w_mod, b_mod):
    L, D, N = w_mod.shape
    R = c_all.shape[0]
    tn = min(1024, N)
    return pl.pallas_call(
        _mod_kernel,
        grid=(L, N // tn),
        in_specs=[pl.BlockSpec((R, D), lambda l, j: (0, 0)),
                  pl.BlockSpec((None, D, tn), lambda l, j: (l, 0, j)),
                  pl.BlockSpec((None, 1, tn), lambda l, j: (l, 0, j))],
        out_specs=pl.BlockSpec((None, R, tn), lambda l, j: (l, 0, j)),
        out_shape=jax.ShapeDtypeStruct((L, R, N), F32),
        compiler_params=_cparams("parallel", "parallel"),
        name="mod",
    )(c_all, w_mod, b_mod.reshape(L, 1, N))


def _rope(x, cos, sina, sinb):
    return x * cos + pltpu.roll(x, V7X_LANES - 16, 1) * sina + pltpu.roll(x, 16, 1) * sinb


def _inproj_kernel(x_ref, sh_ref, sc_ref, g1_ref, w_ref, gq_ref, wuq_ref, gkv_ref, wukv_ref,
                   cos_ref, sina_ref, sinb_ref,
                   qm_ref, km_ref, vm_ref, qs_ref, ks_ref, vs_ref, mq_ref, mk_ref, mv_ref, mo_ref,
                   gs_ref):
    tm = x_ref.shape[0]
    h = (_rms(x_ref[...]) * g1_ref[...]) * (1.0 + sc_ref[...]) + sh_ref[...]
    hb = h.astype(BF16)
    cos, sina, sinb = cos_ref[...], sina_ref[...], sinb_ref[...]
    rope = lambda t: _rope(t, cos, sina, sinb)
    proj = lambda c0, c1: _dot(hb, w_ref[:, c0:c1])
    low = lax.broadcasted_iota(jnp.int32, (tm, V7X_LANES), 1) < MLA_ROPE

    small = proj(C_SMALL, C_END)
    gs_ref[...] = small
    k_rope = jnp.where(low, rope(small), 0.0).astype(BF16)

    zqn = (_rms(proj(C_ZQ, C_ZKV)) * gq_ref[...]).astype(BF16)
    for hh in range(MLA_HEADS):
        qa = _dot(zqn, wuq_ref[:, 256 * hh:256 * (hh + 1)]) * MLA_QSCALE
        qm_ref[:, 256 * hh:256 * hh + 128] = qa[:, :128].astype(BF16)
        qm_ref[:, 256 * hh + 128:256 * (hh + 1)] = jnp.where(low, rope(qa[:, 128:]), 0.0).astype(BF16)

    zkvn = (_rms(proj(C_ZKV, C_SQ)) * gkv_ref[...]).astype(BF16)
    kv = _dot(zkvn, wukv_ref[...])
    for hh in range(MLA_HEADS):
        km_ref[:, 256 * hh:256 * hh + 128] = kv[:, 128 * hh:128 * (hh + 1)].astype(BF16)
        km_ref[:, 256 * hh + 128:256 * (hh + 1)] = k_rope
    vm_ref[...] = kv[:, MLA_HEADS * MLA_NOPE:].astype(BF16)

    for c in range(SWA_QW // 512):
        sq = proj(C_SQ + 512 * c, C_SQ + 512 * (c + 1))
        for p in range(4):
            blk = rope(sq[:, 128 * p:128 * (p + 1)]) * SWA_QSCALE
            qs_ref[:, 512 * c + 128 * p:512 * c + 128 * (p + 1)] = blk.astype(BF16)
    ks_ref[...] = rope(proj(C_SK, C_SV)).astype(BF16)
    vs_ref[...] = proj(C_SV, C_MQ).astype(BF16)

    mq_ref[...] = (proj(C_MQ, C_MK) * MLSTM_QSCALE).astype(BF16)
    mk_ref[...] = proj(C_MK, C_MV).astype(BF16)
    mv_ref[...] = proj(C_MV, C_MO).astype(BF16)
    mo_ref[...] = proj(C_MO, C_SMALL).astype(BF16)


def _inproj(x, sh, sc, g1, w_ext, gq, wuq, gkv, wukv, cos, sina, sinb, *, tm):
    M, D = x.shape
    S = cos.shape[0]
    R = M // sh.shape[0]
    nt = S // tm
    row = lambda i: (i, 0)
    modv = pl.BlockSpec((None, 1, D), lambda i: (i // (R // tm), 0, 0))
    tab = pl.BlockSpec((tm, V7X_LANES), lambda i: (i % nt, 0))
    widths = (MLA_QW, MLA_QW, MLA_VW, SWA_QW, SWA_KW, SWA_KW, MQW, MQW, MVW, MVW)
    out_shape = [jax.ShapeDtypeStruct((M, w), BF16) for w in widths]
    out_shape.append(jax.ShapeDtypeStruct((M, V7X_LANES), F32))
    out_specs = [pl.BlockSpec((tm, w), row) for w in widths] + [pl.BlockSpec((tm, V7X_LANES), row)]
    return pl.pallas_call(
        _inproj_kernel,
        grid=(M // tm,),
        in_specs=[pl.BlockSpec((tm, D), row), modv, modv, _resident((1, D)),
                  _resident(w_ext.shape), _resident(gq.shape), _resident(wuq.shape),
                  _resident(gkv.shape), _resident(wukv.shape), tab, tab, tab],
        out_specs=out_specs,
        out_shape=out_shape,
        compiler_params=_cparams("parallel"),
        name="inproj",
    )(x, sh, sc, g1, w_ext, gq, wuq, gkv, wukv, cos, sina, sinb)


def _mla_kernel(*refs, tk, nk):
    if nk:
        q_ref, k_ref, v_ref, kc_ref, vc_ref, o_ref, m_ref, l_ref, acc_ref = refs
    else:
        q_ref, kc_ref, vc_ref, o_ref, m_ref, l_ref, acc_ref = refs
    q = q_ref[...]
    s = _dot_nt(q, kc_ref[...])
    m0 = jnp.max(s, axis=1, keepdims=True)
    p = jnp.exp2(s - m0)
    m_ref[...] = m0
    l_ref[...] = jnp.sum(p, axis=1, keepdims=True)
    acc_ref[...] = _dot(p.astype(BF16), vc_ref[...])

    if nk:
        def body(j, carry):
            off = pl.multiple_of(j * tk, tk)
            s = _dot_nt(q, k_ref[pl.ds(off, tk), :])
            m_prev = m_ref[...]
            m_new = jnp.maximum(m_prev, jnp.max(s, axis=1, keepdims=True))
            alpha = jnp.exp2(m_prev - m_new)
            p = jnp.exp2(s - m_new)
            l_ref[...] = alpha * l_ref[...] + jnp.sum(p, axis=1, keepdims=True)
            acc_ref[...] = alpha * acc_ref[...] + _dot(p.astype(BF16), v_ref[pl.ds(off, tk), :])
            m_ref[...] = m_new
            return carry
        lax.fori_loop(0, nk, body, 0)
    o_ref[...] = (acc_ref[...] / l_ref[...]).astype(o_ref.dtype)


def _mla(q, kc, vc, k=None, v=None, *, B, tq, tk):
    M = q.shape[0]
    T = M // B
    Lc = kc.shape[0] // B
    nq = T // tq
    H = MLA_HEADS
    qspec = pl.BlockSpec((tq, 256), lambda b, h, i: (b * nq + i, h))
    cspecs = [pl.BlockSpec((Lc, 256), lambda b, h, i: (b, h)),
              pl.BlockSpec((Lc, MLA_V), lambda b, h, i: (b, h))]
    if k is None:
        nk, in_specs, args = 0, [qspec] + cspecs, (q, kc, vc)
    else:
        nk = T // tk
        in_specs = [qspec, pl.BlockSpec((T, 256), lambda b, h, i: (b, h)),
                    pl.BlockSpec((T, MLA_V), lambda b, h, i: (b, h))] + cspecs
        args = (q, k, v, kc, vc)
    return pl.pallas_call(
        functools.partial(_mla_kernel, tk=tk, nk=nk),
        grid=(B, H, nq),
        in_specs=in_specs,
        out_specs=pl.BlockSpec((tq, MLA_V), lambda b, h, i: (b * nq + i, h)),
        out_shape=jax.ShapeDtypeStruct((M, MLA_VW), BF16),
        scratch_shapes=[pltpu.VMEM((tq, 1), F32), pltpu.VMEM((tq, 1), F32),
                        pltpu.VMEM((tq, MLA_V), F32)],
        compiler_params=_cparams("parallel", "parallel", "arbitrary"),
        name="mla_latent" if nk else "mla_ctx",
    )(*args)


def _swa_kernel(*refs, local, tq, S):
    if local:
        sink_ref, q_ref, k_ref, v_ref, kc_ref, vc_ref, o_ref = refs
    else:
        sink_ref, q_ref, kc_ref, vc_ref, o_ref = refs
    d = SWA_HEAD_DIM
    R = SWA_HEADS // SWA_KV_HEADS
    if local:
        span = tq + 2 * WINDOW
        q0 = pl.program_id(1) * tq
        w0 = pl.multiple_of(jnp.clip(q0 - WINDOW, 0, S - span), V7X_LANES)
        kwin = k_ref[pl.ds(w0, span), :]
        vwin = v_ref[pl.ds(w0, span), :]
        rel = (lax.broadcasted_iota(jnp.int32, (tq, span), 1)
               - lax.broadcasted_iota(jnp.int32, (tq, span), 0)) + (w0 - q0)
        valid = jnp.abs(rel) <= WINDOW
    kc = kc_ref[...]
    vc = vc_ref[...]
    for hq in range(SWA_HEADS):
        g = hq // R
        qh = q_ref[:, d * hq:d * (hq + 1)]
        sink = jnp.full((qh.shape[0], 1), sink_ref[hq] * LOG2E, F32)
        s_ctx = _dot_nt(qh, kc[:, d * g:d * (g + 1)])
        m = jnp.maximum(sink, jnp.max(s_ctx, axis=1, keepdims=True))
        if local:
            s_loc = jnp.where(valid, _dot_nt(qh, kwin[:, d * g:d * (g + 1)]), NEG_BIG)
            m = jnp.maximum(m, jnp.max(s_loc, axis=1, keepdims=True))
        p_ctx = jnp.exp2(s_ctx - m)
        den = jnp.exp2(sink - m) + jnp.sum(p_ctx, axis=1, keepdims=True)
        acc = _dot(p_ctx.astype(BF16), vc[:, d * g:d * (g + 1)])
        if local:
            p_loc = jnp.exp2(s_loc - m)
            den = den + jnp.sum(p_loc, axis=1, keepdims=True)
            acc = acc + _dot(p_loc.astype(BF16), vwin[:, d * g:d * (g + 1)])
        o_ref[:, d * hq:d * (hq + 1)] = (acc / den).astype(o_ref.dtype)


def _swa(sink, q, kc, vc, k=None, v=None, *, B, tq):
    M = q.shape[0]
    T = M // B
    Lc = kc.shape[0] // B
    local = k is not None
    if not local:
        tq = T
    nq = T // tq
    qspec = pl.BlockSpec((tq, SWA_QW), lambda b, i: (b * nq + i, 0))
    cspec = pl.BlockSpec((Lc, SWA_KW), lambda b, i: (b, 0))
    in_specs = [pl.BlockSpec(memory_space=pltpu.SMEM), qspec]
    args = [sink, q]
    if local:
        in_specs += [pl.BlockSpec((T, SWA_KW), lambda b, i: (b, 0))] * 2
        args += [k, v]
    in_specs += [cspec, cspec]
    args += [kc, vc]
    return pl.pallas_call(
        functools.partial(_swa_kernel, local=local, tq=tq, S=T),
        grid=(B, nq),
        in_specs=in_specs,
        out_specs=qspec,
        out_shape=jax.ShapeDtypeStruct((M, SWA_QW), BF16),
        compiler_params=_cparams("parallel", "parallel"),
        name="swa_latent" if local else "swa_ctx",
    )(*args)


def _log_sigmoid(x):
    return jnp.minimum(x, 0.0) - jnp.log1p(jnp.exp(-jnp.abs(x)))


def _mlstm_kernel(q_ref, k_ref, v_ref, g_ref, bias_ref, c0_ref, m0_ref,
                  h_ref, cf_ref, mf_ref, c_scr, m_scr, *, rev, nchunks):
    L = q_ref.shape[0]
    n = pl.program_id(1)

    @pl.when(n == 0)
    def _():
        c_scr[...] = c0_ref[...]
        m_scr[...] = m0_ref[...]

    pre = g_ref[...] + bias_ref[...]
    row = lax.broadcasted_iota(jnp.int32, (L, L), 0)
    col = lax.broadcasted_iota(jnp.int32, (L, L), 1)
    allowed = (col >= row) if rev else (col <= row)
    bcum = jnp.dot(allowed.astype(F32), _log_sigmoid(pre), preferred_element_type=F32,
                   precision=lax.Precision.HIGHEST)
    pre_t = pre.T
    bcum_t = bcum.T
    k_t = k_ref[...].astype(F32).T
    ones_col = (lax.broadcasted_iota(jnp.int32, (L, MLSTM_V), 1) == 0).astype(BF16)
    last = 0 if rev else L - 1
    dk, dv = MLSTM_QK, MLSTM_V
    for hh in range(MLSTM_HEADS):
        li = GATE_LANE0 + (2 * MLSTM_HEADS if rev else 0) + hh
        lf = li + MLSTM_HEADS
        b_col = bcum[:, lf:lf + 1]
        b_row = bcum_t[lf:lf + 1, :]
        i_row = pre_t[li:li + 1, :]
        g_tot = b_col[last:last + 1, :]
        m_in = m_scr[hh, 0:1, 0:1]
        c_in = c_scr[hh]

        d_log = jnp.where(allowed, b_col - b_row + i_row, -jnp.inf)
        inter_log = b_col + m_in
        m_t = jnp.maximum(inter_log, jnp.max(d_log, axis=1, keepdims=True))
        q = q_ref[:, dk * hh:dk * (hh + 1)]
        k = k_ref[:, dk * hh:dk * (hh + 1)]
        v_aug = jnp.concatenate([v_ref[:, dv * hh:dv * (hh + 1)], ones_col], axis=1)
        s = _dot_nt(q, k) * jnp.exp(d_log - m_t)
        tot = _dot(s.astype(BF16), v_aug) + jnp.exp(inter_log - m_t) * _dot(q, c_in.astype(BF16))
        den = tot[:, dv:dv + 1]
        h_ref[:, dv * hh:dv * (hh + 1)] = tot[:, :dv] / jnp.maximum(jnp.abs(den), jnp.exp(-m_t))

        a_row = g_tot - b_row + i_row
        a_max = jnp.max(a_row, axis=1, keepdims=True)
        kw_t = (k_t[dk * hh:dk * (hh + 1), :] * jnp.exp(a_row - a_max)).astype(BF16)
        d_c = _dot(kw_t, v_aug)
        m_new = jnp.maximum(g_tot + m_in, a_max)
        c_scr[hh] = jnp.exp(g_tot + m_in - m_new) * c_in + jnp.exp(a_max - m_new) * d_c
        m_scr[hh] = jnp.broadcast_to(m_new, m_scr.shape[1:])

    @pl.when(n == nchunks - 1)
    def _():
        cf_ref[...] = c_scr[...]
        mf_ref[...] = m_scr[...]


def _mlstm(q, k, v, g, bias, c0, m0, *, B, rev, L):
    M = q.shape[0]
    N = M // B // L
    H = MLSTM_HEADS
    chunk = (lambda b, n: (b * N + N - 1 - n, 0)) if rev else (lambda b, n: (b * N + n, 0))
    cspec = pl.BlockSpec((None, H, MLSTM_QK, 256), lambda b, n: (b, 0, 0, 0))
    mspec = pl.BlockSpec((None, H, 8, V7X_LANES), lambda b, n: (b, 0, 0, 0))
    return pl.pallas_call(
        functools.partial(_mlstm_kernel, rev=rev, nchunks=N),
        grid=(B, N),
        in_specs=[pl.BlockSpec((L, MQW), chunk), pl.BlockSpec((L, MQW), chunk),
                  pl.BlockSpec((L, MVW), chunk), pl.BlockSpec((L, V7X_LANES), chunk),
                  _resident(bias.shape), cspec, mspec],
        out_specs=[pl.BlockSpec((L, MVW), chunk), cspec, mspec],
        out_shape=[jax.ShapeDtypeStruct((M, MVW), F32),
                   jax.ShapeDtypeStruct(c0.shape, F32), jax.ShapeDtypeStruct(m0.shape, F32)],
        scratch_shapes=[pltpu.VMEM((H, MLSTM_QK, 256), F32), pltpu.VMEM((H, 8, V7X_LANES), F32)],
        compiler_params=_cparams("parallel", "arbitrary"),
        name="mlstm_bwd" if rev else "mlstm_fwd",
    )(q, k, v, g, bias, c0, m0)


def _outproj_kernel(ya_ref, ys_ref, hf_ref, hb_ref, mo_ref, gh_ref, w_ref, x_ref, gt_ref, o_ref, *, tn):
    dv = MLSTM_V
    parts = []
    for hh in range(MLSTM_HEADS):
        sl = slice(dv * hh, dv * (hh + 1))
        hn = _rms(hf_ref[:, sl] + hb_ref[:, sl]) * gh_ref[:, sl]
        parts.append((hn * jax.nn.sigmoid(mo_ref[:, sl].astype(F32))).astype(BF16))
    ym = jnp.concatenate(parts, axis=1)
    ya = ya_ref[...]
    ys = ys_ref[...]
    r1 = MLA_VW
    r2 = r1 + SWA_QW
    for c in range(o_ref.shape[1] // tn):
        cs = slice(tn * c, tn * (c + 1))
        acc = _dot(ya, w_ref[0:r1, cs]) + _dot(ys, w_ref[r1:r2, cs]) + _dot(ym, w_ref[r2:, cs])
        o_ref[:, cs] = x_ref[:, cs] + gt_ref[:, cs] * acc


def _outproj(ya, ys, hf, hb, mo, gh, w_out, x, gt, *, tm):
    M, D = x.shape
    R = M // gt.shape[0]
    row = lambda i: (i, 0)
    return pl.pallas_call(
        functools.partial(_outproj_kernel, tn=min(512, D)),
        grid=(M // tm,),
        in_specs=[pl.BlockSpec((tm, MLA_VW), row), pl.BlockSpec((tm, SWA_QW), row),
                  pl.BlockSpec((tm, MVW), row), pl.BlockSpec((tm, MVW), row),
                  pl.BlockSpec((tm, MVW), row), _resident(gh.shape), _resident(w_out.shape),
                  pl.BlockSpec((tm, D), row),
                  pl.BlockSpec((None, 1, D), lambda i: (i // (R // tm), 0, 0))],
        out_specs=pl.BlockSpec((tm, D), row),
        out_shape=jax.ShapeDtypeStruct((M, D), F32),
        compiler_params=_cparams("parallel"),
        name="outproj",
    )(ya, ys, hf, hb, mo, gh, w_out, x, gt)


def _ffn_kernel(x_ref, sh_ref, sc_ref, gt_ref, g2_ref, w1_ref, w2_ref, o_ref, h_scr, *, nf):
    f = pl.program_id(1)

    @pl.when(f == 0)
    def _():
        h = (_rms(x_ref[...]) * g2_ref[...]) * (1.0 + sc_ref[...]) + sh_ref[...]
        h_scr[...] = h.astype(BF16)
        o_ref[...] = jnp.zeros_like(o_ref)

    u = jnp.maximum(_dot(h_scr[...], w1_ref[...]), 0.0)
    o_ref[...] += _dot((u * u).astype(BF16), w2_ref[...])

    @pl.when(f == nf - 1)
    def _():
        o_ref[...] = x_ref[...] + gt_ref[...] * o_ref[...]


def _ffn(x, sh, sc, gt, g2, w1, w2, *, tm, tf):
    M, D = x.shape
    FF = w1.shape[1]
    R = M // sh.shape[0]
    nf = FF // tf
    modv = pl.BlockSpec((None, 1, D), lambda i, f: (i // (R // tm), 0, 0))
    return pl.pallas_call(
        functools.partial(_ffn_kernel, nf=nf),
        grid=(M // tm, nf),
        in_specs=[pl.BlockSpec((tm, D), lambda i, f: (i, 0)), modv, modv, modv,
                  _resident((1, D)),
                  pl.BlockSpec((D, tf), lambda i, f: (0, f)),
                  pl.BlockSpec((tf, D), lambda i, f: (f, 0))],
        out_specs=pl.BlockSpec((tm, D), lambda i, f: (i, 0)),
        out_shape=jax.ShapeDtypeStruct((M, D), F32),
        scratch_shapes=[pltpu.VMEM((tm, D), BF16)],
        compiler_params=_cparams("parallel", "arbitrary"),
        name="ffn",
    )(x, sh, sc, gt, g2, w1, w2)


def _final_norm_kernel(x_ref, g_ref, o_ref):
    o_ref[...] = _rms(x_ref[...]) * g_ref[...]


def _final_norm(x, g, *, tm):
    M, D = x.shape
    return pl.pallas_call(
        _final_norm_kernel,
        grid=(M // tm,),
        in_specs=[pl.BlockSpec((tm, D), lambda i: (i, 0)), _resident((1, D))],
        out_specs=pl.BlockSpec((tm, D), lambda i: (i, 0)),
        out_shape=jax.ShapeDtypeStruct((M, D), F32),
        compiler_params=_cparams("parallel"),
        name="final_norm",
    )(x, g)


def _rope_tables(S):
    half = SWA_HEAD_DIM // 2
    pos = jnp.arange(S)
    inv = ROPE_BASE ** (-jnp.arange(0, half, 2, dtype=F32) / half)
    ar = (pos // GRID_W)[:, None].astype(F32) * inv
    ac = (pos % GRID_W)[:, None].astype(F32) * inv
    ang = jnp.concatenate([ar, ar, ac, ac] * 2, axis=-1)
    first = (jnp.arange(V7X_LANES) % 32) < 16
    sin = jnp.sin(ang)
    return jnp.cos(ang), jnp.where(first, -sin, 0.0), jnp.where(first, 0.0, sin)


def _permute_w_in(w_in):
    o = [0]
    for w in (MLA_Q_RANK, MLA_KV_RANK, MLA_ROPE, SWA_QW, SWA_KW, SWA_KW, MQW, MQW, MVW, N_GATES, MVW):
        o.append(o[-1] + w)
    seg = lambda i: w_in[..., o[i]:o[i + 1]]
    pad = jnp.zeros(w_in.shape[:-1] + (V7X_LANES - MLA_ROPE - N_GATES,), w_in.dtype)
    cols = [seg(0), seg(1), seg(3), seg(4), seg(5), seg(6), seg(7), seg(8), seg(10), seg(2), seg(9), pad]
    return jnp.concatenate(cols, axis=-1).astype(BF16)


def _permute_w_uq(w_uq):
    L, Rk, _ = w_uq.shape
    w = w_uq.reshape(L, Rk, MLA_HEADS, MLA_NOPE + MLA_ROPE)
    w = jnp.pad(w, ((0, 0), (0, 0), (0, 0), (0, 256 - MLA_NOPE - MLA_ROPE)))
    return w.reshape(L, Rk, MLA_QW).astype(BF16)


def _permute_w_ukv(w_ukv):
    L, Rk, _ = w_ukv.shape
    w = w_ukv.reshape(L, Rk, MLA_HEADS, MLA_NOPE + MLA_V)
    k = w[..., :MLA_NOPE].reshape(L, Rk, MLA_HEADS * MLA_NOPE)
    v = w[..., MLA_NOPE:].reshape(L, Rk, MLA_VW)
    return jnp.concatenate([k, v], axis=-1).astype(BF16)


def _row_tile(rows, want):
    return want if rows % want == 0 else rows


def kernel(x, c, ctx, c_ctx, w_mod, b_mod, g_norm1, g_norm2, w_in, mla_g_q, mla_w_uq, mla_g_kv,
           mla_w_ukv, swa_sink, mlstm_gate_bias, mlstm_g_h, w_out, w_ff1, w_ff2, g_final):
    B, S, D = x.shape
    Lc = ctx.shape[1]
    depth = w_in.shape[0]
    H = MLSTM_HEADS

    rows = -(-(B + 1) // 8) * 8
    c_all = jnp.concatenate([c, c_ctx[None, :], jnp.zeros((rows - B - 1, D), F32)], axis=0)
    mod = _mod_all(c_all, w_mod, b_mod)
    mod6 = mod.reshape(depth, rows, 6, D)

    w_in_p = _permute_w_in(w_in)
    w_uq_p = _permute_w_uq(mla_w_uq)
    w_ukv_p = _permute_w_ukv(mla_w_ukv)
    w_out_b = w_out.astype(BF16)
    w1_b = w_ff1.astype(BF16)
    w2_b = w_ff2.astype(BF16)
    cos, sina, sinb = _rope_tables(S)
    ones_t = jnp.ones((Lc, V7X_LANES), F32)
    zeros_t = jnp.zeros((Lc, V7X_LANES), F32)
    bias_lanes = jnp.pad(mlstm_gate_bias.reshape(depth, 1, N_GATES),
                         ((0, 0), (0, 0), (GATE_LANE0, V7X_LANES - GATE_LANE0 - N_GATES)))
    c_zero = jnp.zeros((B, H, MLSTM_QK, 256), F32)
    m_zero = jnp.zeros((B, H, 8, V7X_LANES), F32)

    tm = _row_tile(S, 512)
    tm_ffn = _row_tile(S, 1024)
    tf = min(512, w_ff1.shape[2])
    tq_mla = _row_tile(S, 512)
    tk_mla = _row_tile(S, 512)
    tq_swa = _row_tile(S, 256)
    chunk = _row_tile(S, 256)

    xs = x.reshape(B * S, D)
    xc = ctx.reshape(B * Lc, D)
    for l in range(depth):
        need_ctx = l < depth - 1
        vec = lambda j: mod6[l, :B, j][:, None, :]
        vecc = lambda j: mod6[l, B:B + 1, j][:, None, :]
        g1 = g_norm1[l][None, :]
        g2 = g_norm2[l][None, :]
        gq = mla_g_q[l][None, :]
        gkv = mla_g_kv[l][None, :]
        gh = mlstm_g_h[l].reshape(1, MVW)
        inproj = functools.partial(_inproj, g1=g1, w_ext=w_in_p[l], gq=gq, wuq=w_uq_p[l], gkv=gkv,
                                   wukv=w_ukv_p[l])
        (qmc, kmc, vmc, qsc, ksc, vsc, mqc, mkc, mvc, moc, gsc) = inproj(
            xc, vecc(0), vecc(1), cos=ones_t, sina=zeros_t, sinb=zeros_t, tm=Lc)
        hfc, cf, mf = _mlstm(mqc, mkc, mvc, gsc, bias_lanes[l], c_zero, m_zero, B=B, rev=False, L=Lc)
        hbc, cb, mb = _mlstm(mqc, mkc, mvc, gsc, bias_lanes[l], c_zero, m_zero, B=B, rev=True, L=Lc)

        (qm, km, vm, qs, ks, vs, mq, mk, mv, mo, gs) = inproj(
            xs, vec(0), vec(1), cos=cos, sina=sina, sinb=sinb, tm=tm)
        y_mla = _mla(qm, kmc, vmc, km, vm, B=B, tq=tq_mla, tk=tk_mla)
        y_swa = _swa(swa_sink[l], qs, ksc, vsc, ks, vs, B=B, tq=tq_swa)
        hf, _, _ = _mlstm(mq, mk, mv, gs, bias_lanes[l], cf, mf, B=B, rev=False, L=chunk)
        hb, _, _ = _mlstm(mq, mk, mv, gs, bias_lanes[l], cb, mb, B=B, rev=True, L=chunk)
        xs = _outproj(y_mla, y_swa, hf, hb, mo, gh, w_out_b[l], xs, vec(2), tm=tm)
        xs = _ffn(xs, vec(3), vec(4), vec(5), g2, w1_b[l], w2_b[l], tm=tm_ffn, tf=tf)
        if need_ctx:
            yc_mla = _mla(qmc, kmc, vmc, B=B, tq=Lc, tk=Lc)
            yc_swa = _swa(swa_sink[l], qsc, ksc, vsc, B=B, tq=Lc)
            xc = _outproj(yc_mla, yc_swa, hfc, hbc, moc, gh, w_out_b[l], xc, vecc(2), tm=Lc)
            xc = _ffn(xc, vecc(3), vecc(4), vecc(5), g2, w1_b[l], w2_b[l], tm=B * Lc, tf=tf)
    return _final_norm(xs, g_final[None, :], tm=tm).reshape(B, S, D)
```

```python
import functools
import math

import jax
import jax.numpy as jnp
from jax import lax
from jax.experimental import pallas as pl
from jax.experimental.pallas import tpu as pltpu

F32 = jnp.float32
BF16 = jnp.bfloat16

GRID_W = 64
ROPE_BASE = 10000.0
NORM_EPS = 1e-6
MLA_HEADS = 4
MLA_Q_RANK = 512
MLA_KV_RANK = 256
MLA_NOPE = 128
MLA_ROPE = 64
MLA_V = 128
SWA_HEADS = 16
SWA_KV_HEADS = 2
SWA_HEAD_DIM = 64
WINDOW = 128
MLSTM_HEADS = 4
MLSTM_QK = 64
MLSTM_V = 128
N_GATES = 4 * MLSTM_HEADS

V7X_LANES = 128
V7X_VMEM_LIMIT_BYTES = 60 * 1024 * 1024

LOG2E = math.log2(math.e)
MLA_QSCALE = (MLA_NOPE + MLA_ROPE) ** -0.5 * LOG2E
SWA_QSCALE = SWA_HEAD_DIM ** -0.5 * LOG2E
MLSTM_QSCALE = MLSTM_QK ** -0.5
NEG_BIG = -1e30

C_ZQ = 0
C_ZKV = C_ZQ + MLA_Q_RANK
C_SQ = C_ZKV + MLA_KV_RANK
C_SK = C_SQ + SWA_HEADS * SWA_HEAD_DIM
C_SV = C_SK + SWA_KV_HEADS * SWA_HEAD_DIM
C_MQ = C_SV + SWA_KV_HEADS * SWA_HEAD_DIM
C_MK = C_MQ + MLSTM_HEADS * MLSTM_QK
C_MV = C_MK + MLSTM_HEADS * MLSTM_QK
C_MO = C_MV + MLSTM_HEADS * MLSTM_V
C_SMALL = C_MO + MLSTM_HEADS * MLSTM_V
C_END = C_SMALL + V7X_LANES
GATE_LANE0 = MLA_ROPE

MLA_QW = MLA_HEADS * 256
MLA_VW = MLA_HEADS * MLA_V
SWA_QW = SWA_HEADS * SWA_HEAD_DIM
SWA_KW = SWA_KV_HEADS * SWA_HEAD_DIM
MQW = MLSTM_HEADS * MLSTM_QK
MVW = MLSTM_HEADS * MLSTM_V


def _cparams(*sem):
    return pltpu.CompilerParams(dimension_semantics=sem, vmem_limit_bytes=V7X_VMEM_LIMIT_BYTES)


def _resident(shape):
    nd = len(shape)
    return pl.BlockSpec(shape, lambda *_: (0,) * nd, pipeline_mode=pl.Buffered(1))


def _layer_resident(stacked, l):
    nd = stacked.ndim - 1
    return pl.BlockSpec((None,) + stacked.shape[1:], lambda *_: (l,) + (0,) * nd,
                        pipeline_mode=pl.Buffered(1))


def _dot(a, b):
    return jnp.dot(a, b, preferred_element_type=F32)


def _dot_nt(a, b):
    return lax.dot_general(a, b, (((1,), (1,)), ((), ())), preferred_element_type=F32)


def _rms(x):
    return x * lax.rsqrt(jnp.mean(x * x, axis=-1, keepdims=True) + NORM_EPS)


def _mod_kernel(c_ref, w_ref, b_ref, o_ref):
    c = c_ref[...]
    a = c * jax.nn.sigmoid(c)
    o_ref[...] = jnp.dot(a, w_ref[...], preferred_element_type=F32,
                         precision=lax.Precision.HIGHEST) + b_ref[...]


def _mod_all(c_all, w_mod, b_mod):
    L, D, N = w_mod.shape
    R = c_all.shape[0]
    tn = min(1024, N)
    return pl.pallas_call(
        _mod_kernel,
        grid=(L, N // tn),
        in_specs=[pl.BlockSpec((R, D), lambda l, j: (0, 0)),
                  pl.BlockSpec((None, D, tn), lambda l, j: (l, 0, j)),
                  pl.BlockSpec((None, 1, tn), lambda l, j: (l, 0, j))],
        out_specs=pl.BlockSpec((None, R, tn), lambda l, j: (l, 0, j)),
        out_shape=jax.ShapeDtypeStruct((L, R, N), F32),
        compiler_params=_cparams("parallel", "parallel"),
        name="mod",
    )(c_all, w_mod, b_mod.reshape(L, 1, N))


def _rope(x, cos, sina, sinb):
    return x * cos + pltpu.roll(x, V7X_LANES - 16, 1) * sina + pltpu.roll(x, 16, 1) * sinb


def _inproj_kernel(x_ref, sh_ref, sc_ref, g1_ref, w_ref, gq_ref, wuq_ref, gkv_ref, wukv_ref,
                   cos_ref, sina_ref, sinb_ref,
                   qm_ref, km_ref, vm_ref, qs_ref, ks_ref, vs_ref, mq_ref, mk_ref, mv_ref, mo_ref,
                   gs_ref):
    tm = x_ref.shape[0]
    h = (_rms(x_ref[...]) * g1_ref[...]) * (1.0 + sc_ref[...]) + sh_ref[...]
    hb = h.astype(BF16)
    cos, sina, sinb = cos_ref[...], sina_ref[...], sinb_ref[...]
    rope = lambda t: _rope(t, cos, sina, sinb)
    proj = lambda c0, c1: _dot(hb, w_ref[:, c0:c1])
    low = lax.broadcasted_iota(jnp.int32, (tm, V7X_LANES), 1) < MLA_ROPE

    small = proj(C_SMALL, C_END)
    gs_ref[...] = small
    k_rope = jnp.where(low, rope(small), 0.0).astype(BF16)

    zqn = (_rms(proj(C_ZQ, C_ZKV)) * gq_ref[...]).astype(BF16)
    for hh in range(MLA_HEADS):
        qa = _dot(zqn, wuq_ref[:, 256 * hh:256 * (hh + 1)]) * MLA_QSCALE
        qm_ref[:, 256 * hh:256 * hh + 128] = qa[:, :128].astype(BF16)
        qm_ref[:, 256 * hh + 128:256 * (hh + 1)] = jnp.where(low, rope(qa[:, 128:]), 0.0).astype(BF16)

    zkvn = (_rms(proj(C_ZKV, C_SQ)) * gkv_ref[...]).astype(BF16)
    kv = _dot(zkvn, wukv_ref[...])
    for hh in range(MLA_HEADS):
        km_ref[:, 256 * hh:256 * hh + 128] = kv[:, 128 * hh:128 * (hh + 1)].astype(BF16)
        km_ref[:, 256 * hh + 128:256 * (hh + 1)] = k_rope
    vm_ref[...] = kv[:, MLA_HEADS * MLA_NOPE:].astype(BF16)

    for c in range(SWA_QW // 512):
        sq = proj(C_SQ + 512 * c, C_SQ + 512 * (c + 1))
        for p in range(4):
            blk = rope(sq[:, 128 * p:128 * (p + 1)]) * SWA_QSCALE
            qs_ref[:, 512 * c + 128 * p:512 * c + 128 * (p + 1)] = blk.astype(BF16)
    ks_ref[...] = rope(proj(C_SK, C_SV)).astype(BF16)
    vs_ref[...] = proj(C_SV, C_MQ).astype(BF16)

    mq_ref[...] = (proj(C_MQ, C_MK) * MLSTM_QSCALE).astype(BF16)
    mk_ref[...] = proj(C_MK, C_MV).astype(BF16)
    mv_ref[...] = proj(C_MV, C_MO).astype(BF16)
    mo_ref[...] = proj(C_MO, C_SMALL).astype(BF16)


def _inproj(x, sh, sc, g1, w_ext, gq, wuq, gkv, wukv, cos, sina, sinb, *, l, tm):
    M, D = x.shape
    S = cos.shape[0]
    R = M // sh.shape[0]
    nt = S // tm
    row = lambda i: (i, 0)
    modv = pl.BlockSpec((None, 1, D), lambda i: (i // (R // tm), 0, 0))
    tab = pl.BlockSpec((tm, V7X_LANES), lambda i: (i % nt, 0))
    widths = (MLA_QW, MLA_QW, MLA_VW, SWA_QW, SWA_KW, SWA_KW, MQW, MQW, MVW, MVW)
    out_shape = [jax.ShapeDtypeStruct((M, w), BF16) for w in widths]
    out_shape.append(jax.ShapeDtypeStruct((M, V7X_LANES), F32))
    out_specs = [pl.BlockSpec((tm, w), row) for w in widths] + [pl.BlockSpec((tm, V7X_LANES), row)]
    return pl.pallas_call(
        _inproj_kernel,
        grid=(M // tm,),
        in_specs=[pl.BlockSpec((tm, D), row), modv, modv, _resident((1, D)),
                  _layer_resident(w_ext, l), _resident(gq.shape), _layer_resident(wuq, l),
                  _resident(gkv.shape), _layer_resident(wukv, l), tab, tab, tab],
        out_specs=out_specs,
        out_shape=out_shape,
        compiler_params=_cparams("parallel"),
        name="inproj",
    )(x, sh, sc, g1, w_ext, gq, wuq, gkv, wukv, cos, sina, sinb)


def _mla_kernel(*refs, tk, nk):
    if nk:
        q_ref, k_ref, v_ref, kc_ref, vc_ref, o_ref, m_ref, l_ref, acc_ref = refs
    else:
        q_ref, kc_ref, vc_ref, o_ref, m_ref, l_ref, acc_ref = refs
    q = q_ref[...]
    tq = q.shape[0]
    W = V7X_LANES

    def update(kblk, vblk, first):
        ncol = kblk.shape[0] // W
        s = _dot_nt(q, kblk)
        smax = s[:, 0:W]
        for c in range(1, ncol):
            smax = jnp.maximum(smax, s[:, W * c:W * (c + 1)])
        m_new = jnp.broadcast_to(jnp.max(smax, axis=1, keepdims=True), (tq, W))
        if not first:
            m_prev = m_ref[...]
            m_new = jnp.maximum(m_prev, m_new)
            alpha = jnp.exp2(m_prev - m_new)
        lsum = None
        ps = []
        for c in range(ncol):
            pc = jnp.exp2(s[:, W * c:W * (c + 1)] - m_new)
            lsum = pc if lsum is None else lsum + pc
            ps.append(pc.astype(BF16))
        pv = _dot(jnp.concatenate(ps, axis=1), vblk)
        if first:
            l_ref[...] = lsum
            acc_ref[...] = pv
        else:
            l_ref[...] = alpha * l_ref[...] + lsum
            acc_ref[...] = alpha * acc_ref[...] + pv
        m_ref[...] = m_new

    update(kc_ref[...], vc_ref[...], True)
    if nk:
        def body(j, carry):
            off = pl.multiple_of(j * tk, tk)
            update(k_ref[pl.ds(off, tk), :], v_ref[pl.ds(off, tk), :], False)
            return carry
        lax.fori_loop(0, nk, body, 0)
    l = jnp.sum(l_ref[...], axis=1, keepdims=True)
    o_ref[...] = (acc_ref[...] / l).astype(o_ref.dtype)


def _mla(q, kc, vc, k=None, v=None, *, B, tq, tk):
    M = q.shape[0]
    T = M // B
    Lc = kc.shape[0] // B
    nq = T // tq
    H = MLA_HEADS
    qspec = pl.BlockSpec((tq, 256), lambda b, h, i: (b * nq + i, h))
    cspecs = [pl.BlockSpec((Lc, 256), lambda b, h, i: (b, h)),
              pl.BlockSpec((Lc, MLA_V), lambda b, h, i: (b, h))]
    if k is None:
        nk, in_specs, args = 0, [qspec] + cspecs, (q, kc, vc)
    else:
        nk = T // tk
        in_specs = [qspec, pl.BlockSpec((T, 256), lambda b, h, i: (b, h)),
                    pl.BlockSpec((T, MLA_V), lambda b, h, i: (b, h))] + cspecs
        args = (q, k, v, kc, vc)
    return pl.pallas_call(
        functools.partial(_mla_kernel, tk=tk, nk=nk),
        grid=(B, H, nq),
        in_specs=in_specs,
        out_specs=pl.BlockSpec((tq, MLA_V), lambda b, h, i: (b * nq + i, h)),
        out_shape=jax.ShapeDtypeStruct((M, MLA_VW), BF16),
        scratch_shapes=[pltpu.VMEM((tq, V7X_LANES), F32), pltpu.VMEM((tq, V7X_LANES), F32),
                        pltpu.VMEM((tq, MLA_V), F32)],
        compiler_params=_cparams("parallel", "parallel", "arbitrary"),
        name="mla_latent" if nk else "mla_ctx",
    )(*args)


def _swa_kernel(*refs, local, tq, S):
    if local:
        sink_ref, q_ref, k_ref, v_ref, kc_ref, vc_ref, o_ref = refs
    else:
        sink_ref, q_ref, kc_ref, vc_ref, o_ref = refs
    d = SWA_HEAD_DIM
    R = SWA_HEADS // SWA_KV_HEADS
    if local:
        span = tq + 2 * WINDOW
        q0 = pl.program_id(1) * tq
        w0 = pl.multiple_of(jnp.clip(q0 - WINDOW, 0, S - span), V7X_LANES)
        kwin = k_ref[pl.ds(w0, span), :]
        vwin = v_ref[pl.ds(w0, span), :]
        rel = (lax.broadcasted_iota(jnp.int32, (tq, span), 1)
               - lax.broadcasted_iota(jnp.int32, (tq, span), 0)) + (w0 - q0)
        valid = jnp.abs(rel) <= WINDOW
    kc = kc_ref[...]
    vc = vc_ref[...]
    for hq in range(SWA_HEADS):
        g = hq // R
        qh = q_ref[:, d * hq:d * (hq + 1)]
        sink = jnp.full((qh.shape[0], 1), sink_ref[hq] * LOG2E, F32)
        s_ctx = _dot_nt(qh, kc[:, d * g:d * (g + 1)])
        m = jnp.maximum(sink, jnp.max(s_ctx, axis=1, keepdims=True))
        if local:
            s_loc = jnp.where(valid, _dot_nt(qh, kwin[:, d * g:d * (g + 1)]), NEG_BIG)
            m = jnp.maximum(m, jnp.max(s_loc, axis=1, keepdims=True))
        p_ctx = jnp.exp2(s_ctx - m)
        den = jnp.exp2(sink - m) + jnp.sum(p_ctx, axis=1, keepdims=True)
        acc = _dot(p_ctx.astype(BF16), vc[:, d * g:d * (g + 1)])
        if local:
            p_loc = jnp.exp2(s_loc - m)
            den = den + jnp.sum(p_loc, axis=1, keepdims=True)
            acc = acc + _dot(p_loc.astype(BF16), vwin[:, d * g:d * (g + 1)])
        o_ref[:, d * hq:d * (hq + 1)] = (acc / den).astype(o_ref.dtype)


def _swa(sink, q, kc, vc, k=None, v=None, *, B, tq):
    M = q.shape[0]
    T = M // B
    Lc = kc.shape[0] // B
    local = k is not None
    if not local:
        tq = T
    nq = T // tq
    qspec = pl.BlockSpec((tq, SWA_QW), lambda b, i: (b * nq + i, 0))
    cspec = pl.BlockSpec((Lc, SWA_KW), lambda b, i: (b, 0))
    in_specs = [pl.BlockSpec(memory_space=pltpu.SMEM), qspec]
    args = [sink, q]
    if local:
        in_specs += [pl.BlockSpec((T, SWA_KW), lambda b, i: (b, 0))] * 2
        args += [k, v]
    in_specs += [cspec, cspec]
    args += [kc, vc]
    return pl.pallas_call(
        functools.partial(_swa_kernel, local=local, tq=tq, S=T),
        grid=(B, nq),
        in_specs=in_specs,
        out_specs=qspec,
        out_shape=jax.ShapeDtypeStruct((M, SWA_QW), BF16),
        compiler_params=_cparams("parallel", "parallel"),
        name="swa_latent" if local else "swa_ctx",
    )(*args)


def _log_sigmoid(x):
    return jnp.minimum(x, 0.0) - jnp.log1p(jnp.exp(-jnp.abs(x)))


def _mlstm_kernel(q_ref, k_ref, v_ref, g_ref, bias_ref, c0_ref, m0_ref,
                  h_ref, cf_ref, mf_ref, c_scr, m_scr, *, rev, nchunks):
    L = q_ref.shape[0]
    n = pl.program_id(1)

    @pl.when(n == 0)
    def _():
        c_scr[...] = c0_ref[...]
        m_scr[...] = m0_ref[...]

    pre = g_ref[...] + bias_ref[...]
    row = lax.broadcasted_iota(jnp.int32, (L, L), 0)
    col = lax.broadcasted_iota(jnp.int32, (L, L), 1)
    allowed = (col >= row) if rev else (col <= row)
    bcum = jnp.dot(allowed.astype(F32), _log_sigmoid(pre), preferred_element_type=F32,
                   precision=lax.Precision.HIGHEST)
    pre_t = pre.T
    bcum_t = bcum.T
    k_t = k_ref[...].astype(F32).T
    ones_col = (lax.broadcasted_iota(jnp.int32, (L, MLSTM_V), 1) == 0).astype(BF16)
    last = 0 if rev else L - 1
    dk, dv = MLSTM_QK, MLSTM_V
    for hh in range(MLSTM_HEADS):
        li = GATE_LANE0 + (2 * MLSTM_HEADS if rev else 0) + hh
        lf = li + MLSTM_HEADS
        b_col = bcum[:, lf:lf + 1]
        b_row = bcum_t[lf:lf + 1, :]
        i_row = pre_t[li:li + 1, :]
        g_tot = b_col[last:last + 1, :]
        m_in = m_scr[hh, 0:1, 0:1]
        c_in = c_scr[hh]

        d_log = jnp.where(allowed, b_col - b_row + i_row, -jnp.inf)
        inter_log = b_col + m_in
        m_t = jnp.maximum(inter_log, jnp.max(d_log, axis=1, keepdims=True))
        q = q_ref[:, dk * hh:dk * (hh + 1)]
        k = k_ref[:, dk * hh:dk * (hh + 1)]
        v_aug = jnp.concatenate([v_ref[:, dv * hh:dv * (hh + 1)], ones_col], axis=1)
        s = _dot_nt(q, k) * jnp.exp(d_log - m_t)
        tot = _dot(s.astype(BF16), v_aug) + jnp.exp(inter_log - m_t) * _dot(q, c_in.astype(BF16))
        den = tot[:, dv:dv + 1]
        h_ref[:, dv * hh:dv * (hh + 1)] = tot[:, :dv] / jnp.maximum(jnp.abs(den), jnp.exp(-m_t))

        a_row = g_tot - b_row + i_row
        a_max = jnp.max(a_row, axis=1, keepdims=True)
        kw_t = (k_t[dk * hh:dk * (hh + 1), :] * jnp.exp(a_row - a_max)).astype(BF16)
        d_c = _dot(kw_t, v_aug)
        m_new = jnp.maximum(g_tot + m_in, a_max)
        c_scr[hh] = jnp.exp(g_tot + m_in - m_new) * c_in + jnp.exp(a_max - m_new) * d_c
        m_scr[hh] = jnp.broadcast_to(m_new, m_scr.shape[1:])

    @pl.when(n == nchunks - 1)
    def _():
        cf_ref[...] = c_scr[...]
        mf_ref[...] = m_scr[...]


def _mlstm(q, k, v, g, bias, c0, m0, *, B, rev, L):
    M = q.shape[0]
    N = M // B // L
    H = MLSTM_HEADS
    chunk = (lambda b, n: (b * N + N - 1 - n, 0)) if rev else (lambda b, n: (b * N + n, 0))
    cspec = pl.BlockSpec((None, H, MLSTM_QK, 256), lambda b, n: (b, 0, 0, 0))
    mspec = pl.BlockSpec((None, H, 8, V7X_LANES), lambda b, n: (b, 0, 0, 0))
    return pl.pallas_call(
        functools.partial(_mlstm_kernel, rev=rev, nchunks=N),
        grid=(B, N),
        in_specs=[pl.BlockSpec((L, MQW), chunk), pl.BlockSpec((L, MQW), chunk),
                  pl.BlockSpec((L, MVW), chunk), pl.BlockSpec((L, V7X_LANES), chunk),
                  _resident(bias.shape), cspec, mspec],
        out_specs=[pl.BlockSpec((L, MVW), chunk), cspec, mspec],
        out_shape=[jax.ShapeDtypeStruct((M, MVW), F32),
                   jax.ShapeDtypeStruct(c0.shape, F32), jax.ShapeDtypeStruct(m0.shape, F32)],
        scratch_shapes=[pltpu.VMEM((H, MLSTM_QK, 256), F32), pltpu.VMEM((H, 8, V7X_LANES), F32)],
        compiler_params=_cparams("parallel", "arbitrary"),
        name="mlstm_bwd" if rev else "mlstm_fwd",
    )(q, k, v, g, bias, c0, m0)


def _outproj_kernel(ya_ref, ys_ref, hf_ref, hb_ref, mo_ref, gh_ref, w_ref, x_ref, gt_ref, o_ref, *, tn):
    dv = MLSTM_V
    parts = []
    for hh in range(MLSTM_HEADS):
        sl = slice(dv * hh, dv * (hh + 1))
        hn = _rms(hf_ref[:, sl] + hb_ref[:, sl]) * gh_ref[:, sl]
        parts.append((hn * jax.nn.sigmoid(mo_ref[:, sl].astype(F32))).astype(BF16))
    ym = jnp.concatenate(parts, axis=1)
    ya = ya_ref[...]
    ys = ys_ref[...]
    r1 = MLA_VW
    r2 = r1 + SWA_QW
    for c in range(o_ref.shape[1] // tn):
        cs = slice(tn * c, tn * (c + 1))
        acc = _dot(ya, w_ref[0:r1, cs]) + _dot(ys, w_ref[r1:r2, cs]) + _dot(ym, w_ref[r2:, cs])
        o_ref[:, cs] = x_ref[:, cs] + gt_ref[:, cs] * acc


def _outproj(ya, ys, hf, hb, mo, gh, w_out, x, gt, *, l, tm):
    M, D = x.shape
    R = M // gt.shape[0]
    row = lambda i: (i, 0)
    return pl.pallas_call(
        functools.partial(_outproj_kernel, tn=min(512, D)),
        grid=(M // tm,),
        in_specs=[pl.BlockSpec((tm, MLA_VW), row), pl.BlockSpec((tm, SWA_QW), row),
                  pl.BlockSpec((tm, MVW), row), pl.BlockSpec((tm, MVW), row),
                  pl.BlockSpec((tm, MVW), row), _resident(gh.shape), _layer_resident(w_out, l),
                  pl.BlockSpec((tm, D), row),
                  pl.BlockSpec((None, 1, D), lambda i: (i // (R // tm), 0, 0))],
        out_specs=pl.BlockSpec((tm, D), row),
        out_shape=jax.ShapeDtypeStruct((M, D), F32),
        compiler_params=_cparams("parallel"),
        name="outproj",
    )(ya, ys, hf, hb, mo, gh, w_out, x, gt)


def _ffn_kernel(x_ref, sh_ref, sc_ref, gt_ref, g2_ref, gfin_ref, w1_ref, w2_ref, o_ref, h_scr, *,
                nf, final):
    f = pl.program_id(1)

    @pl.when(f == 0)
    def _():
        h = (_rms(x_ref[...]) * g2_ref[...]) * (1.0 + sc_ref[...]) + sh_ref[...]
        h_scr[...] = h.astype(BF16)
        o_ref[...] = jnp.zeros_like(o_ref)

    u = jnp.maximum(_dot(h_scr[...], w1_ref[...]), 0.0)
    o_ref[...] += _dot((u * u).astype(BF16), w2_ref[...])

    @pl.when(f == nf - 1)
    def _():
        y = x_ref[...] + gt_ref[...] * o_ref[...]
        o_ref[...] = _rms(y) * gfin_ref[...] if final else y


def _ffn(x, sh, sc, gt, g2, gfin, w1, w2, *, l, tm, tf, final):
    M, D = x.shape
    FF = w1.shape[2]
    R = M // sh.shape[0]
    nf = FF // tf
    modv = pl.BlockSpec((None, 1, D), lambda i, f: (i // (R // tm), 0, 0))
    return pl.pallas_call(
        functools.partial(_ffn_kernel, nf=nf, final=final),
        grid=(M // tm, nf),
        in_specs=[pl.BlockSpec((tm, D), lambda i, f: (i, 0), pipeline_mode=pl.Buffered(1)),
                  modv, modv, modv, _resident((1, D)), _resident((1, D)),
                  pl.BlockSpec((None, D, tf), lambda i, f: (l, 0, f)),
                  pl.BlockSpec((None, tf, D), lambda i, f: (l, f, 0))],
        out_specs=pl.BlockSpec((tm, D), lambda i, f: (i, 0)),
        out_shape=jax.ShapeDtypeStruct((M, D), F32),
        scratch_shapes=[pltpu.VMEM((tm, D), BF16)],
        compiler_params=_cparams("parallel", "arbitrary"),
        name="ffn",
    )(x, sh, sc, gt, g2, gfin, w1, w2)


def _rope_tables(S):
    half = SWA_HEAD_DIM // 2
    pos = jnp.arange(S)
    inv = ROPE_BASE ** (-jnp.arange(0, half, 2, dtype=F32) / half)
    ar = (pos // GRID_W)[:, None].astype(F32) * inv
    ac = (pos % GRID_W)[:, None].astype(F32) * inv
    ang = jnp.concatenate([ar, ar, ac, ac] * 2, axis=-1)
    first = (jnp.arange(V7X_LANES) % 32) < 16
    sin = jnp.sin(ang)
    return jnp.cos(ang), jnp.where(first, -sin, 0.0), jnp.where(first, 0.0, sin)


def _permute_w_in(w_in):
    o = [0]
    for w in (MLA_Q_RANK, MLA_KV_RANK, MLA_ROPE, SWA_QW, SWA_KW, SWA_KW, MQW, MQW, MVW, N_GATES, MVW):
        o.append(o[-1] + w)
    seg = lambda i: w_in[..., o[i]:o[i + 1]]
    pad = jnp.zeros(w_in.shape[:-1] + (V7X_LANES - MLA_ROPE - N_GATES,), w_in.dtype)
    cols = [seg(0), seg(1), seg(3), seg(4), seg(5), seg(6), seg(7), seg(8), seg(10), seg(2), seg(9), pad]
    return jnp.concatenate(cols, axis=-1).astype(BF16)


def _permute_w_uq(w_uq):
    L, Rk, _ = w_uq.shape
    w = w_uq.reshape(L, Rk, MLA_HEADS, MLA_NOPE + MLA_ROPE)
    w = jnp.pad(w, ((0, 0), (0, 0), (0, 0), (0, 256 - MLA_NOPE - MLA_ROPE)))
    return w.reshape(L, Rk, MLA_QW).astype(BF16)


def _permute_w_ukv(w_ukv):
    L, Rk, _ = w_ukv.shape
    w = w_ukv.reshape(L, Rk, MLA_HEADS, MLA_NOPE + MLA_V)
    k = w[..., :MLA_NOPE].reshape(L, Rk, MLA_HEADS * MLA_NOPE)
    v = w[..., MLA_NOPE:].reshape(L, Rk, MLA_VW)
    return jnp.concatenate([k, v], axis=-1).astype(BF16)


def _row_tile(rows, want):
    return want if rows % want == 0 else rows


def kernel(x, c, ctx, c_ctx, w_mod, b_mod, g_norm1, g_norm2, w_in, mla_g_q, mla_w_uq, mla_g_kv,
           mla_w_ukv, swa_sink, mlstm_gate_bias, mlstm_g_h, w_out, w_ff1, w_ff2, g_final):
    B, S, D = x.shape
    Lc = ctx.shape[1]
    depth = w_in.shape[0]
    H = MLSTM_HEADS

    rows = -(-(B + 1) // 8) * 8
    c_all = jnp.concatenate([c, c_ctx[None, :], jnp.zeros((rows - B - 1, D), F32)], axis=0)
    mod = _mod_all(c_all, w_mod, b_mod)
    mod6 = mod.reshape(depth, rows, 6, D)

    w_in_p = _permute_w_in(w_in)
    w_uq_p = _permute_w_uq(mla_w_uq)
    w_ukv_p = _permute_w_ukv(mla_w_ukv)
    w_out_b = w_out.astype(BF16)
    w1_b = w_ff1.astype(BF16)
    w2_b = w_ff2.astype(BF16)
    cos, sina, sinb = _rope_tables(S)
    ones_t = jnp.ones((Lc, V7X_LANES), F32)
    zeros_t = jnp.zeros((Lc, V7X_LANES), F32)
    bias_lanes = jnp.pad(mlstm_gate_bias.reshape(depth, 1, N_GATES),
                         ((0, 0), (0, 0), (GATE_LANE0, V7X_LANES - GATE_LANE0 - N_GATES)))
    c_zero = jnp.zeros((B, H, MLSTM_QK, 256), F32)
    m_zero = jnp.zeros((B, H, 8, V7X_LANES), F32)

    tm = _row_tile(S, 512)
    tm_ffn = _row_tile(S, 1024)
    tf = min(1024, w_ff1.shape[2])
    gfin = g_final[None, :]
    tq_mla = _row_tile(S, 1024)
    tk_mla = _row_tile(S, 1024)
    tq_swa = _row_tile(S, 256)
    chunk = _row_tile(S, 256)

    xs = x.reshape(B * S, D)
    xc = ctx.reshape(B * Lc, D)
    for l in range(depth):
        need_ctx = l < depth - 1
        vec = lambda j: mod6[l, :B, j][:, None, :]
        vecc = lambda j: mod6[l, B:B + 1, j][:, None, :]
        g1 = g_norm1[l][None, :]
        g2 = g_norm2[l][None, :]
        gq = mla_g_q[l][None, :]
        gkv = mla_g_kv[l][None, :]
        gh = mlstm_g_h[l].reshape(1, MVW)
        inproj = functools.partial(_inproj, g1=g1, w_ext=w_in_p, gq=gq, wuq=w_uq_p, gkv=gkv,
                                   wukv=w_ukv_p, l=l)
        ffn = functools.partial(_ffn, g2=g2, gfin=gfin, w1=w1_b, w2=w2_b, l=l, tf=tf)
        (qmc, kmc, vmc, qsc, ksc, vsc, mqc, mkc, mvc, moc, gsc) = inproj(
            xc, vecc(0), vecc(1), cos=ones_t, sina=zeros_t, sinb=zeros_t, tm=Lc)
        hfc, cf, mf = _mlstm(mqc, mkc, mvc, gsc, bias_lanes[l], c_zero, m_zero, B=B, rev=False, L=Lc)
        hbc, cb, mb = _mlstm(mqc, mkc, mvc, gsc, bias_lanes[l], c_zero, m_zero, B=B, rev=True, L=Lc)

        (qm, km, vm, qs, ks, vs, mq, mk, mv, mo, gs) = inproj(
            xs, vec(0), vec(1), cos=cos, sina=sina, sinb=sinb, tm=tm)
        y_mla = _mla(qm, kmc, vmc, km, vm, B=B, tq=tq_mla, tk=tk_mla)
        y_swa = _swa(swa_sink[l], qs, ksc, vsc, ks, vs, B=B, tq=tq_swa)
        hf, _, _ = _mlstm(mq, mk, mv, gs, bias_lanes[l], cf, mf, B=B, rev=False, L=chunk)
        hb, _, _ = _mlstm(mq, mk, mv, gs, bias_lanes[l], cb, mb, B=B, rev=True, L=chunk)
        xs = _outproj(y_mla, y_swa, hf, hb, mo, gh, w_out_b, xs, vec(2), l=l, tm=tm)
        xs = ffn(xs, vec(3), vec(4), vec(5), tm=tm_ffn, final=not need_ctx)
        if need_ctx:
            yc_mla = _mla(qmc, kmc, vmc, B=B, tq=Lc, tk=Lc)
            yc_swa = _swa(swa_sink[l], qsc, ksc, vsc, B=B, tq=Lc)
            xc = _outproj(yc_mla, yc_swa, hfc, hbc, moc, gh, w_out_b, xc, vecc(2), l=l, tm=Lc)
            xc = ffn(xc, vecc(3), vecc(4), vecc(5), tm=B * Lc, final=False)
    return xs.reshape(B, S, D)
```

```python
import functools
import math

import jax
import jax.numpy as jnp
from jax import lax
from jax.experimental import pallas as pl
from jax.experimental.pallas import tpu as pltpu

F32 = jnp.float32
BF16 = jnp.bfloat16

GRID_W = 64
ROPE_BASE = 10000.0
NORM_EPS = 1e-6
MLA_HEADS = 4
MLA_Q_RANK = 512
MLA_KV_RANK = 256
MLA_NOPE = 128
MLA_ROPE = 64
MLA_V = 128
SWA_HEADS = 16
SWA_KV_HEADS = 2
SWA_HEAD_DIM = 64
WINDOW = 128
MLSTM_HEADS = 4
MLSTM_QK = 64
MLSTM_V = 128
N_GATES = 4 * MLSTM_HEADS

V7X_LANES = 128
V7X_VMEM_LIMIT_BYTES = 60 * 1024 * 1024

LOG2E = math.log2(math.e)
MLA_QSCALE = (MLA_NOPE + MLA_ROPE) ** -0.5 * LOG2E
SWA_QSCALE = SWA_HEAD_DIM ** -0.5 * LOG2E
MLSTM_QSCALE = MLSTM_QK ** -0.5
NEG_BIG = -1e30

C_ZQ = 0
C_ZKV = C_ZQ + MLA_Q_RANK
C_SQ = C_ZKV + MLA_KV_RANK
C_SK = C_SQ + SWA_HEADS * SWA_HEAD_DIM
C_SV = C_SK + SWA_KV_HEADS * SWA_HEAD_DIM
C_MQ = C_SV + SWA_KV_HEADS * SWA_HEAD_DIM
C_MK = C_MQ + MLSTM_HEADS * MLSTM_QK
C_MV = C_MK + MLSTM_HEADS * MLSTM_QK
C_MO = C_MV + MLSTM_HEADS * MLSTM_V
C_SMALL = C_MO + MLSTM_HEADS * MLSTM_V
C_END = C_SMALL + V7X_LANES
GATE_LANE0 = MLA_ROPE

MLA_QW = MLA_HEADS * 256
MLA_VW = MLA_HEADS * MLA_V
SWA_QW = SWA_HEADS * SWA_HEAD_DIM
SWA_KW = SWA_KV_HEADS * SWA_HEAD_DIM
MQW = MLSTM_HEADS * MLSTM_QK
MVW = MLSTM_HEADS * MLSTM_V


def _cparams(*sem):
    return pltpu.CompilerParams(dimension_semantics=sem, vmem_limit_bytes=V7X_VMEM_LIMIT_BYTES)


def _resident(shape):
    nd = len(shape)
    return pl.BlockSpec(shape, lambda *_: (0,) * nd, pipeline_mode=pl.Buffered(1))


def _layer_resident(stacked, l):
    nd = stacked.ndim - 1
    return pl.BlockSpec((None,) + stacked.shape[1:], lambda *_: (l,) + (0,) * nd,
                        pipeline_mode=pl.Buffered(1))


def _dot(a, b):
    return jnp.dot(a, b, preferred_element_type=F32)


def _dot_nt(a, b):
    return lax.dot_general(a, b, (((1,), (1,)), ((), ())), preferred_element_type=F32)


def _rms(x):
    return x * lax.rsqrt(jnp.mean(x * x, axis=-1, keepdims=True) + NORM_EPS)


def _mod_kernel(c_ref, w_ref, b_ref, o_ref):
    c = c_ref[...]
    a = c * jax.nn.sigmoid(c)
    o_ref[...] = jnp.dot(a, w_ref[...], preferred_element_type=F32,
                         precision=lax.Precision.HIGHEST) + b_ref[...]


def _mod_all(c_all, w_mod, b_mod):
    L, D, N = w_mod.shape
    R = c_all.shape[0]
    tn = min(1024, N)
    return pl.pallas_call(
        _mod_kernel,
        grid=(L, N // tn),
        in_specs=[pl.BlockSpec((R, D), lambda l, j: (0, 0)),
                  pl.BlockSpec((None, D, tn), lambda l, j: (l, 0, j)),
                  pl.BlockSpec((None, 1, tn), lambda l, j: (l, 0, j))],
        out_specs=pl.BlockSpec((None, R, tn), lambda l, j: (l, 0, j)),
        out_shape=jax.ShapeDtypeStruct((L, R, N), F32),
        compiler_params=_cparams("parallel", "parallel"),
        name="mod",
    )(c_all, w_mod, b_mod.reshape(L, 1, N))


def _rope(x, cos, sina, sinb):
    return x * cos + pltpu.roll(x, V7X_LANES - 16, 1) * sina + pltpu.roll(x, 16, 1) * sinb


def _inproj_kernel(x_ref, sh_ref, sc_ref, g1_ref, w_ref, gq_ref, wuq_ref, gkv_ref, wukv_ref,
                   cos_ref, sina_ref, sinb_ref,
                   qm_ref, km_ref, vm_ref, qs_ref, ks_ref, vs_ref, mq_ref, mk_ref, mv_ref, mo_ref,
                   gs_ref):
    tm = x_ref.shape[0]
    h = (_rms(x_ref[...]) * g1_ref[...]) * (1.0 + sc_ref[...]) + sh_ref[...]
    hb = h.astype(BF16)
    cos, sina, sinb = cos_ref[...], sina_ref[...], sinb_ref[...]
    rope = lambda t: _rope(t, cos, sina, sinb)
    proj = lambda c0, c1: _dot(hb, w_ref[:, c0:c1])
    low = lax.broadcasted_iota(jnp.int32, (tm, V7X_LANES), 1) < MLA_ROPE

    small = proj(C_SMALL, C_END)
    gs_ref[...] = small
    k_rope = jnp.where(low, rope(small), 0.0).astype(BF16)

    zqn = (_rms(proj(C_ZQ, C_ZKV)) * gq_ref[...]).astype(BF16)
    for hh in range(MLA_HEADS):
        qa = _dot(zqn, wuq_ref[:, 256 * hh:256 * (hh + 1)]) * MLA_QSCALE
        qm_ref[:, 256 * hh:256 * hh + 128] = qa[:, :128].astype(BF16)
        qm_ref[:, 256 * hh + 128:256 * (hh + 1)] = jnp.where(low, rope(qa[:, 128:]), 0.0).astype(BF16)

    zkvn = (_rms(proj(C_ZKV, C_SQ)) * gkv_ref[...]).astype(BF16)
    kv = _dot(zkvn, wukv_ref[...])
    for hh in range(MLA_HEADS):
        km_ref[:, 256 * hh:256 * hh + 128] = kv[:, 128 * hh:128 * (hh + 1)].astype(BF16)
        km_ref[:, 256 * hh + 128:256 * (hh + 1)] = k_rope
    vm_ref[...] = kv[:, MLA_HEADS * MLA_NOPE:].astype(BF16)

    for c in range(SWA_QW // 512):
        sq = proj(C_SQ + 512 * c, C_SQ + 512 * (c + 1))
        for p in range(4):
            blk = rope(sq[:, 128 * p:128 * (p + 1)]) * SWA_QSCALE
            qs_ref[:, 512 * c + 128 * p:512 * c + 128 * (p + 1)] = blk.astype(BF16)
    for ref, val in ((ks_ref, rope(proj(C_SK, C_SV))), (vs_ref, proj(C_SV, C_MQ))):
        ref[:, :SWA_KW] = val.astype(BF16)
        ref[:, SWA_KW:] = pltpu.roll(val, SWA_HEAD_DIM, 1).astype(BF16)

    mq_ref[...] = (proj(C_MQ, C_MK) * MLSTM_QSCALE).astype(BF16)
    mk_ref[...] = proj(C_MK, C_MV).astype(BF16)
    mv_ref[...] = proj(C_MV, C_MO).astype(BF16)
    mo_ref[...] = proj(C_MO, C_SMALL).astype(BF16)


def _inproj(x, sh, sc, g1, w_ext, gq, wuq, gkv, wukv, cos, sina, sinb, *, l, tm):
    M, D = x.shape
    S = cos.shape[0]
    R = M // sh.shape[0]
    nt = S // tm
    row = lambda i: (i, 0)
    modv = pl.BlockSpec((None, 1, D), lambda i: (i // (R // tm), 0, 0))
    tab = pl.BlockSpec((tm, V7X_LANES), lambda i: (i % nt, 0))
    widths = (MLA_QW, MLA_QW, MLA_VW, SWA_QW, 2 * SWA_KW, 2 * SWA_KW, MQW, MQW, MVW, MVW)
    out_shape = [jax.ShapeDtypeStruct((M, w), BF16) for w in widths]
    out_shape.append(jax.ShapeDtypeStruct((M, V7X_LANES), F32))
    out_specs = [pl.BlockSpec((tm, w), row) for w in widths] + [pl.BlockSpec((tm, V7X_LANES), row)]
    return pl.pallas_call(
        _inproj_kernel,
        grid=(M // tm,),
        in_specs=[pl.BlockSpec((tm, D), row), modv, modv, _resident((1, D)),
                  _layer_resident(w_ext, l), _resident(gq.shape), _layer_resident(wuq, l),
                  _resident(gkv.shape), _layer_resident(wukv, l), tab, tab, tab],
        out_specs=out_specs,
        out_shape=out_shape,
        compiler_params=_cparams("parallel"),
        name="inproj",
    )(x, sh, sc, g1, w_ext, gq, wuq, gkv, wukv, cos, sina, sinb)


def _mla_kernel(*refs, tk, nk):
    if nk:
        q_ref, k_ref, v_ref, kc_ref, vc_ref, o_ref, m_ref, l_ref, acc_ref = refs
    else:
        q_ref, kc_ref, vc_ref, o_ref, m_ref, l_ref, acc_ref = refs
    q = q_ref[...]
    tq = q.shape[0]
    W = V7X_LANES

    def update(kblk, vblk, first):
        ncol = kblk.shape[0] // W
        s = _dot_nt(q, kblk)
        smax = s[:, 0:W]
        for c in range(1, ncol):
            smax = jnp.maximum(smax, s[:, W * c:W * (c + 1)])
        m_new = jnp.broadcast_to(jnp.max(smax, axis=1, keepdims=True), (tq, W))
        if not first:
            m_prev = m_ref[...]
            m_new = jnp.maximum(m_prev, m_new)
            alpha = jnp.exp2(m_prev - m_new)
        lsum = None
        ps = []
        for c in range(ncol):
            pc = jnp.exp2(s[:, W * c:W * (c + 1)] - m_new)
            lsum = pc if lsum is None else lsum + pc
            ps.append(pc.astype(BF16))
        pv = _dot(jnp.concatenate(ps, axis=1), vblk)
        if first:
            l_ref[...] = lsum
            acc_ref[...] = pv
        else:
            l_ref[...] = alpha * l_ref[...] + lsum
            acc_ref[...] = alpha * acc_ref[...] + pv
        m_ref[...] = m_new

    update(kc_ref[...], vc_ref[...], True)
    if nk:
        def body(j, carry):
            off = pl.multiple_of(j * tk, tk)
            update(k_ref[pl.ds(off, tk), :], v_ref[pl.ds(off, tk), :], False)
            return carry
        lax.fori_loop(0, nk, body, 0)
    l = jnp.sum(l_ref[...], axis=1, keepdims=True)
    o_ref[...] = (acc_ref[...] / l).astype(o_ref.dtype)


def _mla(q, kc, vc, k=None, v=None, *, B, tq, tk):
    M = q.shape[0]
    T = M // B
    Lc = kc.shape[0] // B
    nq = T // tq
    H = MLA_HEADS
    qspec = pl.BlockSpec((tq, 256), lambda b, h, i: (b * nq + i, h))
    cspecs = [pl.BlockSpec((Lc, 256), lambda b, h, i: (b, h)),
              pl.BlockSpec((Lc, MLA_V), lambda b, h, i: (b, h))]
    if k is None:
        nk, in_specs, args = 0, [qspec] + cspecs, (q, kc, vc)
    else:
        nk = T // tk
        in_specs = [qspec, pl.BlockSpec((T, 256), lambda b, h, i: (b, h)),
                    pl.BlockSpec((T, MLA_V), lambda b, h, i: (b, h))] + cspecs
        args = (q, k, v, kc, vc)
    return pl.pallas_call(
        functools.partial(_mla_kernel, tk=tk, nk=nk),
        grid=(B, H, nq),
        in_specs=in_specs,
        out_specs=pl.BlockSpec((tq, MLA_V), lambda b, h, i: (b * nq + i, h)),
        out_shape=jax.ShapeDtypeStruct((M, MLA_VW), BF16),
        scratch_shapes=[pltpu.VMEM((tq, V7X_LANES), F32), pltpu.VMEM((tq, V7X_LANES), F32),
                        pltpu.VMEM((tq, MLA_V), F32)],
        compiler_params=_cparams("parallel", "parallel", "arbitrary"),
        name="mla_latent" if nk else "mla_ctx",
    )(*args)


SWA_SUB = 128
SWA_SPAN = SWA_SUB + 2 * WINDOW


def _swa_kernel(*refs, local, tq, S):
    if local:
        sink_ref, q_ref, k_ref, v_ref, kc_ref, vc_ref, o_ref = refs
    else:
        sink_ref, q_ref, kc_ref, vc_ref, o_ref = refs
    W = V7X_LANES
    d = SWA_HEAD_DIM
    ppg = SWA_HEADS // SWA_KV_HEADS // 2
    Lc = kc_ref.shape[0]

    def variants(x2):
        low = lax.broadcasted_iota(jnp.int32, (x2.shape[0], W), 1) < d
        x, xs = x2[:, :W], x2[:, W:]
        zero = jnp.zeros_like(x)
        return {(0, 0): jnp.where(low, x, zero), (1, 1): jnp.where(low, zero, x),
                (0, 1): jnp.where(low, zero, xs), (1, 0): jnp.where(low, xs, zero)}

    def with_ones(vv):
        return {ge: jnp.concatenate([t, jnp.ones_like(t)], axis=1) for ge, t in vv.items()}

    kcv = variants(kc_ref[...])
    vcv = with_ones(variants(vc_ref[...]))
    for j in range(tq // SWA_SUB):
        r = slice(SWA_SUB * j, SWA_SUB * (j + 1))
        if local:
            qpos = pl.program_id(1) * tq + SWA_SUB * j
            r0 = pl.multiple_of(jnp.clip(qpos - WINDOW, 0, S - SWA_SPAN), W)
            kv = variants(k_ref[pl.ds(r0, SWA_SPAN), :])
            vv = with_ones(variants(v_ref[pl.ds(r0, SWA_SPAN), :]))
            rel = (lax.broadcasted_iota(jnp.int32, (SWA_SUB, SWA_SPAN), 1)
                   - lax.broadcasted_iota(jnp.int32, (SWA_SUB, SWA_SPAN), 0)) + (r0 - qpos)
            bias = jnp.where(jnp.abs(rel) <= WINDOW, 0.0, NEG_BIG)
            bias = jnp.concatenate([bias] * ppg, axis=0)
        for g in range(SWA_KV_HEADS):
            cols = [slice(W * (ppg * g + pp), W * (ppg * g + pp + 1)) for pp in range(ppg)]
            qs = jnp.concatenate([q_ref[r, c] for c in cols], axis=0)
            pair_out = None
            for e in range(2):
                sink = jnp.concatenate(
                    [jnp.full((SWA_SUB, W), sink_ref[2 * (ppg * g + pp) + e] * LOG2E, F32)
                     for pp in range(ppg)], axis=0)
                s_ctx = _dot_nt(qs, kcv[g, e])
                blocks = [s_ctx[:, W * c:W * (c + 1)] for c in range(Lc // W)]
                if local:
                    s_loc = _dot_nt(qs, kv[g, e]) + bias
                    blocks += [s_loc[:, W * c:W * (c + 1)] for c in range(SWA_SPAN // W)]
                mx = blocks[0]
                for blk in blocks[1:]:
                    mx = jnp.maximum(mx, blk)
                m = jnp.maximum(sink, jnp.broadcast_to(jnp.max(mx, axis=1, keepdims=True), mx.shape))
                ps = [jnp.exp2(blk - m).astype(BF16) for blk in blocks]
                out = _dot(jnp.concatenate(ps[:Lc // W], axis=1), vcv[g, e])
                if local:
                    out = out + _dot(jnp.concatenate(ps[Lc // W:], axis=1), vv[g, e])
                o = out[:, :W] / (out[:, W:] + jnp.exp2(sink - m))
                pair_out = o if pair_out is None else pair_out + o
            for pp, c in enumerate(cols):
                o_ref[r, c] = pair_out[SWA_SUB * pp:SWA_SUB * (pp + 1)].astype(o_ref.dtype)


def _swa(sink, q, kc, vc, k=None, v=None, *, B, tq):
    M = q.shape[0]
    T = M // B
    Lc = kc.shape[0] // B
    local = k is not None
    if not local:
        tq = T
    nq = T // tq
    qspec = pl.BlockSpec((tq, SWA_QW), lambda b, i: (b * nq + i, 0))
    cspec = pl.BlockSpec((Lc, 2 * SWA_KW), lambda b, i: (b, 0))
    in_specs = [pl.BlockSpec(memory_space=pltpu.SMEM), qspec]
    args = [sink, q]
    if local:
        in_specs += [pl.BlockSpec((T, 2 * SWA_KW), lambda b, i: (b, 0))] * 2
        args += [k, v]
    in_specs += [cspec, cspec]
    args += [kc, vc]
    return pl.pallas_call(
        functools.partial(_swa_kernel, local=local, tq=tq, S=T),
        grid=(B, nq),
        in_specs=in_specs,
        out_specs=qspec,
        out_shape=jax.ShapeDtypeStruct((M, SWA_QW), BF16),
        compiler_params=_cparams("parallel", "parallel"),
        name="swa_latent" if local else "swa_ctx",
    )(*args)


def _log_sigmoid(x):
    return jnp.minimum(x, 0.0) - jnp.log1p(jnp.exp(-jnp.abs(x)))


def _mlstm_kernel(q_ref, k_ref, v_ref, g_ref, bias_ref, c0_ref, m0_ref,
                  h_ref, cf_ref, mf_ref, c_scr, m_scr, *, rev, nchunks):
    L = q_ref.shape[0]
    n = pl.program_id(1)

    @pl.when(n == 0)
    def _():
        c_scr[...] = c0_ref[...]
        m_scr[...] = m0_ref[...]

    pre = g_ref[...] + bias_ref[...]
    row = lax.broadcasted_iota(jnp.int32, (L, L), 0)
    col = lax.broadcasted_iota(jnp.int32, (L, L), 1)
    allowed = (col >= row) if rev else (col <= row)
    bcum = jnp.dot(allowed.astype(F32), _log_sigmoid(pre), preferred_element_type=F32,
                   precision=lax.Precision.HIGHEST)
    lane0 = GATE_LANE0 + (2 * MLSTM_HEADS if rev else 0)
    onehot = (lax.broadcasted_iota(jnp.int32, (8, V7X_LANES), 1)
              == lax.broadcasted_iota(jnp.int32, (8, V7X_LANES), 0) + lane0).astype(F32)
    to_rows = lambda t: lax.dot_general(onehot, t, (((1,), (1,)), ((), ())), preferred_element_type=F32,
                                        precision=lax.Precision.HIGHEST)
    pre_rows = to_rows(pre)
    bcum_rows = to_rows(bcum)
    W = V7X_LANES
    rep = lambda t, n: jnp.concatenate([t] * n, axis=1)
    ones_blk = jnp.ones((L, W), BF16)
    last = 0 if rev else L - 1
    dk, dv = MLSTM_QK, MLSTM_V
    for hh in range(MLSTM_HEADS):
        li = lane0 + hh
        lf = li + MLSTM_HEADS
        b_rep = jnp.broadcast_to(bcum[:, lf:lf + 1], (L, W))
        i_rep = jnp.broadcast_to(pre[:, li:li + 1], (L, W))
        b_row = bcum_rows[MLSTM_HEADS + hh:MLSTM_HEADS + hh + 1, :]
        i_row = pre_rows[hh:hh + 1, :]
        g_tot = b_rep[last:last + 1, :]
        m_in = m_scr[hh, 0:1, :]
        c_in = c_scr[hh]

        d_log = jnp.where(allowed, rep(b_rep, L // W) - b_row + i_row, -jnp.inf)
        inter_log = b_rep + m_in
        m_t = jnp.maximum(inter_log, jnp.broadcast_to(jnp.max(d_log, axis=1, keepdims=True), (L, W)))
        q = q_ref[:, dk * hh:dk * (hh + 1)]
        k = k_ref[:, dk * hh:dk * (hh + 1)]
        v_aug = jnp.concatenate([v_ref[:, dv * hh:dv * (hh + 1)], ones_blk], axis=1)
        s = _dot_nt(q, k) * jnp.exp(d_log - rep(m_t, L // W))
        tot = (_dot(s.astype(BF16), v_aug)
               + rep(jnp.exp(inter_log - m_t), 2) * _dot(q, c_in.astype(BF16)))
        h_ref[:, dv * hh:dv * (hh + 1)] = tot[:, :dv] / jnp.maximum(jnp.abs(tot[:, dv:]), jnp.exp(-m_t))

        a = g_tot - b_rep + i_rep
        a_max = jnp.max(a, axis=0, keepdims=True)
        wv = (rep(jnp.exp(a - a_max), 2) * v_aug.astype(F32)).astype(BF16)
        d_c = lax.dot_general(k, wv, (((0,), (0,)), ((), ())),
                              preferred_element_type=F32)
        m_new = jnp.maximum(g_tot + m_in, a_max)
        c_scr[hh] = (rep(jnp.exp(g_tot + m_in - m_new), 2) * c_in
                     + rep(jnp.exp(a_max - m_new), 2) * d_c)
        m_scr[hh] = jnp.broadcast_to(m_new, m_scr.shape[1:])

    @pl.when(n == nchunks - 1)
    def _():
        cf_ref[...] = c_scr[...]
        mf_ref[...] = m_scr[...]


def _mlstm(q, k, v, g, bias, c0, m0, *, B, rev, L):
    M = q.shape[0]
    N = M // B // L
    H = MLSTM_HEADS
    chunk = (lambda b, n: (b * N + N - 1 - n, 0)) if rev else (lambda b, n: (b * N + n, 0))
    cspec = pl.BlockSpec((None, H, MLSTM_QK, 256), lambda b, n: (b, 0, 0, 0))
    mspec = pl.BlockSpec((None, H, 8, V7X_LANES), lambda b, n: (b, 0, 0, 0))
    return pl.pallas_call(
        functools.partial(_mlstm_kernel, rev=rev, nchunks=N),
        grid=(B, N),
        in_specs=[pl.BlockSpec((L, MQW), chunk), pl.BlockSpec((L, MQW), chunk),
                  pl.BlockSpec((L, MVW), chunk), pl.BlockSpec((L, V7X_LANES), chunk),
                  _resident(bias.shape), cspec, mspec],
        out_specs=[pl.BlockSpec((L, MVW), chunk), cspec, mspec],
        out_shape=[jax.ShapeDtypeStruct((M, MVW), F32),
                   jax.ShapeDtypeStruct(c0.shape, F32), jax.ShapeDtypeStruct(m0.shape, F32)],
        scratch_shapes=[pltpu.VMEM((H, MLSTM_QK, 256), F32), pltpu.VMEM((H, 8, V7X_LANES), F32)],
        compiler_params=_cparams("parallel", "arbitrary"),
        name="mlstm_bwd" if rev else "mlstm_fwd",
    )(q, k, v, g, bias, c0, m0)


def _outproj_kernel(ya_ref, ys_ref, hf_ref, hb_ref, mo_ref, gh_ref, w_ref, x_ref, gt_ref, o_ref, *, tn):
    dv = MLSTM_V
    parts = []
    for hh in range(MLSTM_HEADS):
        sl = slice(dv * hh, dv * (hh + 1))
        hn = _rms(hf_ref[:, sl] + hb_ref[:, sl]) * gh_ref[:, sl]
        parts.append((hn * jax.nn.sigmoid(mo_ref[:, sl].astype(F32))).astype(BF16))
    ym = jnp.concatenate(parts, axis=1)
    ya = ya_ref[...]
    ys = ys_ref[...]
    r1 = MLA_VW
    r2 = r1 + SWA_QW
    for c in range(o_ref.shape[1] // tn):
        cs = slice(tn * c, tn * (c + 1))
        acc = _dot(ya, w_ref[0:r1, cs]) + _dot(ys, w_ref[r1:r2, cs]) + _dot(ym, w_ref[r2:, cs])
        o_ref[:, cs] = x_ref[:, cs] + gt_ref[:, cs] * acc


def _outproj(ya, ys, hf, hb, mo, gh, w_out, x, gt, *, l, tm):
    M, D = x.shape
    R = M // gt.shape[0]
    row = lambda i: (i, 0)
    return pl.pallas_call(
        functools.partial(_outproj_kernel, tn=min(512, D)),
        grid=(M // tm,),
        in_specs=[pl.BlockSpec((tm, MLA_VW), row), pl.BlockSpec((tm, SWA_QW), row),
                  pl.BlockSpec((tm, MVW), row), pl.BlockSpec((tm, MVW), row),
                  pl.BlockSpec((tm, MVW), row), _resident(gh.shape), _layer_resident(w_out, l),
                  pl.BlockSpec((tm, D), row),
                  pl.BlockSpec((None, 1, D), lambda i: (i // (R // tm), 0, 0))],
        out_specs=pl.BlockSpec((tm, D), row),
        out_shape=jax.ShapeDtypeStruct((M, D), F32),
        compiler_params=_cparams("parallel"),
        name="outproj",
    )(ya, ys, hf, hb, mo, gh, w_out, x, gt)


def _ffn_kernel(x_ref, sh_ref, sc_ref, gt_ref, g2_ref, gfin_ref, w1_ref, w2_ref, o_ref, h_scr, *,
                nf, final):
    f = pl.program_id(1)

    @pl.when(f == 0)
    def _():
        h = (_rms(x_ref[...]) * g2_ref[...]) * (1.0 + sc_ref[...]) + sh_ref[...]
        h_scr[...] = h.astype(BF16)
        o_ref[...] = jnp.zeros_like(o_ref)

    u = jnp.maximum(_dot(h_scr[...], w1_ref[...]), 0.0)
    o_ref[...] += _dot((u * u).astype(BF16), w2_ref[...])

    @pl.when(f == nf - 1)
    def _():
        y = x_ref[...] + gt_ref[...] * o_ref[...]
        o_ref[...] = _rms(y) * gfin_ref[...] if final else y


def _ffn(x, sh, sc, gt, g2, gfin, w1, w2, *, l, tm, tf, xbuf, final):
    M, D = x.shape
    FF = w1.shape[2]
    R = M // sh.shape[0]
    nf = FF // tf
    modv = pl.BlockSpec((None, 1, D), lambda i, f: (i // (R // tm), 0, 0))
    return pl.pallas_call(
        functools.partial(_ffn_kernel, nf=nf, final=final),
        grid=(M // tm, nf),
        in_specs=[pl.BlockSpec((tm, D), lambda i, f: (i, 0), pipeline_mode=pl.Buffered(xbuf)),
                  modv, modv, modv, _resident((1, D)), _resident((1, D)),
                  pl.BlockSpec((None, D, tf), lambda i, f: (l, 0, f)),
                  pl.BlockSpec((None, tf, D), lambda i, f: (l, f, 0))],
        out_specs=pl.BlockSpec((tm, D), lambda i, f: (i, 0)),
        out_shape=jax.ShapeDtypeStruct((M, D), F32),
        scratch_shapes=[pltpu.VMEM((tm, D), BF16)],
        compiler_params=_cparams("parallel", "arbitrary"),
        name="ffn",
    )(x, sh, sc, gt, g2, gfin, w1, w2)


def _rope_tables(S):
    half = SWA_HEAD_DIM // 2
    pos = jnp.arange(S)
    inv = ROPE_BASE ** (-jnp.arange(0, half, 2, dtype=F32) / half)
    ar = (pos // GRID_W)[:, None].astype(F32) * inv
    ac = (pos % GRID_W)[:, None].astype(F32) * inv
    ang = jnp.concatenate([ar, ar, ac, ac] * 2, axis=-1)
    first = (jnp.arange(V7X_LANES) % 32) < 16
    sin = jnp.sin(ang)
    return jnp.cos(ang), jnp.where(first, -sin, 0.0), jnp.where(first, 0.0, sin)


def _permute_w_in(w_in):
    o = [0]
    for w in (MLA_Q_RANK, MLA_KV_RANK, MLA_ROPE, SWA_QW, SWA_KW, SWA_KW, MQW, MQW, MVW, N_GATES, MVW):
        o.append(o[-1] + w)
    seg = lambda i: w_in[..., o[i]:o[i + 1]]
    pad = jnp.zeros(w_in.shape[:-1] + (V7X_LANES - MLA_ROPE - N_GATES,), w_in.dtype)
    cols = [seg(0), seg(1), seg(3), seg(4), seg(5), seg(6), seg(7), seg(8), seg(10), seg(2), seg(9), pad]
    return jnp.concatenate(cols, axis=-1).astype(BF16)


def _permute_w_uq(w_uq):
    L, Rk, _ = w_uq.shape
    w = w_uq.reshape(L, Rk, MLA_HEADS, MLA_NOPE + MLA_ROPE)
    w = jnp.pad(w, ((0, 0), (0, 0), (0, 0), (0, 256 - MLA_NOPE - MLA_ROPE)))
    return w.reshape(L, Rk, MLA_QW).astype(BF16)


def _permute_w_ukv(w_ukv):
    L, Rk, _ = w_ukv.shape
    w = w_ukv.reshape(L, Rk, MLA_HEADS, MLA_NOPE + MLA_V)
    k = w[..., :MLA_NOPE].reshape(L, Rk, MLA_HEADS * MLA_NOPE)
    v = w[..., MLA_NOPE:].reshape(L, Rk, MLA_VW)
    return jnp.concatenate([k, v], axis=-1).astype(BF16)


def _row_tile(rows, want):
    return want if rows % want == 0 else rows


def kernel(x, c, ctx, c_ctx, w_mod, b_mod, g_norm1, g_norm2, w_in, mla_g_q, mla_w_uq, mla_g_kv,
           mla_w_ukv, swa_sink, mlstm_gate_bias, mlstm_g_h, w_out, w_ff1, w_ff2, g_final):
    B, S, D = x.shape
    Lc = ctx.shape[1]
    depth = w_in.shape[0]
    H = MLSTM_HEADS

    rows = -(-(B + 1) // 8) * 8
    c_all = jnp.concatenate([c, c_ctx[None, :], jnp.zeros((rows - B - 1, D), F32)], axis=0)
    mod = _mod_all(c_all, w_mod, b_mod)
    mod6 = mod.reshape(depth, rows, 6, D)

    w_in_p = _permute_w_in(w_in)
    w_uq_p = _permute_w_uq(mla_w_uq)
    w_ukv_p = _permute_w_ukv(mla_w_ukv)
    w_out_b = w_out.astype(BF16)
    w1_b = w_ff1.astype(BF16)
    w2_b = w_ff2.astype(BF16)
    cos, sina, sinb = _rope_tables(S)
    ones_t = jnp.ones((Lc, V7X_LANES), F32)
    zeros_t = jnp.zeros((Lc, V7X_LANES), F32)
    bias_lanes = jnp.pad(mlstm_gate_bias.reshape(depth, 1, N_GATES),
                         ((0, 0), (0, 0), (GATE_LANE0, V7X_LANES - GATE_LANE0 - N_GATES)))
    c_zero = jnp.zeros((B, H, MLSTM_QK, 256), F32)
    m_zero = jnp.zeros((B, H, 8, V7X_LANES), F32)

    tm = _row_tile(S, 512)
    FF = w_ff1.shape[2]
    ffn_cfgs = [(1024, 512, 2), (1024, 1024, 1), (512, 1024, 2), (1024, 512, 1)]
    mla_cfgs = [(1024, 1024), (512, 2048), (2048, 512), (1024, 2048)]
    gfin = g_final[None, :]
    tq_swa = _row_tile(S, 256)
    chunk = _row_tile(S, 256)

    xs = x.reshape(B * S, D)
    xc = ctx.reshape(B * Lc, D)
    for l in range(depth):
        need_ctx = l < depth - 1
        vec = lambda j: mod6[l, :B, j][:, None, :]
        vecc = lambda j: mod6[l, B:B + 1, j][:, None, :]
        g1 = g_norm1[l][None, :]
        g2 = g_norm2[l][None, :]
        gq = mla_g_q[l][None, :]
        gkv = mla_g_kv[l][None, :]
        gh = mlstm_g_h[l].reshape(1, MVW)
        inproj = functools.partial(_inproj, g1=g1, w_ext=w_in_p, gq=gq, wuq=w_uq_p, gkv=gkv,
                                   wukv=w_ukv_p, l=l)
        tm_ffn, tf, xbuf = ffn_cfgs[l % len(ffn_cfgs)]
        tm_ffn, tf = _row_tile(S, tm_ffn), min(tf, FF)
        tq_mla, tk_mla = (_row_tile(S, t) for t in mla_cfgs[l % len(mla_cfgs)])
        ffn = functools.partial(_ffn, g2=g2, gfin=gfin, w1=w1_b, w2=w2_b, l=l, tf=tf)
        (qmc, kmc, vmc, qsc, ksc, vsc, mqc, mkc, mvc, moc, gsc) = inproj(
            xc, vecc(0), vecc(1), cos=ones_t, sina=zeros_t, sinb=zeros_t, tm=Lc)
        hfc, cf, mf = _mlstm(mqc, mkc, mvc, gsc, bias_lanes[l], c_zero, m_zero, B=B, rev=False, L=Lc)
        hbc, cb, mb = _mlstm(mqc, mkc, mvc, gsc, bias_lanes[l], c_zero, m_zero, B=B, rev=True, L=Lc)

        (qm, km, vm, qs, ks, vs, mq, mk, mv, mo, gs) = inproj(
            xs, vec(0), vec(1), cos=cos, sina=sina, sinb=sinb, tm=tm)
        y_mla = _mla(qm, kmc, vmc, km, vm, B=B, tq=tq_mla, tk=tk_mla)
        y_swa = _swa(swa_sink[l], qs, ksc, vsc, ks, vs, B=B, tq=tq_swa)
        hf, _, _ = _mlstm(mq, mk, mv, gs, bias_lanes[l], cf, mf, B=B, rev=False, L=chunk)
        hb, _, _ = _mlstm(mq, mk, mv, gs, bias_lanes[l], cb, mb, B=B, rev=True, L=chunk)
        xs = _outproj(y_mla, y_swa, hf, hb, mo, gh, w_out_b, xs, vec(2), l=l, tm=tm)
        xs = ffn(xs, vec(3), vec(4), vec(5), tm=tm_ffn, xbuf=xbuf, final=not need_ctx)
        if need_ctx:
            yc_mla = _mla(qmc, kmc, vmc, B=B, tq=Lc, tk=Lc)
            yc_swa = _swa(swa_sink[l], qsc, ksc, vsc, B=B, tq=Lc)
            xc = _outproj(yc_mla, yc_swa, hfc, hbc, moc, gh, w_out_b, xc, vecc(2), l=l, tm=Lc)
            xc = ffn(xc, vecc(3), vecc(4), vecc(5), tm=B * Lc, tf=min(512, FF), xbuf=2, final=False)
    return xs.reshape(B, S, D)
```

```python
import functools
import math

import jax
import jax.numpy as jnp
from jax import lax
from jax.experimental import pallas as pl
from jax.experimental.pallas import tpu as pltpu

F32 = jnp.float32
BF16 = jnp.bfloat16

GRID_W = 64
ROPE_BASE = 10000.0
NORM_EPS = 1e-6
MLA_HEADS = 4
MLA_Q_RANK = 512
MLA_KV_RANK = 256
MLA_NOPE = 128
MLA_ROPE = 64
MLA_V = 128
SWA_HEADS = 16
SWA_KV_HEADS = 2
SWA_HEAD_DIM = 64
WINDOW = 128
MLSTM_HEADS = 4
MLSTM_QK = 64
MLSTM_V = 128
N_GATES = 4 * MLSTM_HEADS

V7X_LANES = 128
V7X_VMEM_LIMIT_BYTES = 60 * 1024 * 1024

LOG2E = math.log2(math.e)
MLA_QSCALE = (MLA_NOPE + MLA_ROPE) ** -0.5 * LOG2E
SWA_QSCALE = SWA_HEAD_DIM ** -0.5 * LOG2E
MLSTM_QSCALE = MLSTM_QK ** -0.5
NEG_BIG = -1e30

C_ZQ = 0
C_ZKV = C_ZQ + MLA_Q_RANK
C_SQ = C_ZKV + MLA_KV_RANK
C_SK = C_SQ + SWA_HEADS * SWA_HEAD_DIM
C_SV = C_SK + SWA_KV_HEADS * SWA_HEAD_DIM
C_MQ = C_SV + SWA_KV_HEADS * SWA_HEAD_DIM
C_MK = C_MQ + MLSTM_HEADS * MLSTM_QK
C_MV = C_MK + MLSTM_HEADS * MLSTM_QK
C_MO = C_MV + MLSTM_HEADS * MLSTM_V
C_SMALL = C_MO + MLSTM_HEADS * MLSTM_V
C_END = C_SMALL + V7X_LANES
GATE_LANE0 = MLA_ROPE

MLA_QW = MLA_HEADS * 256
MLA_VW = MLA_HEADS * MLA_V
SWA_QW = SWA_HEADS * SWA_HEAD_DIM
SWA_KW = SWA_KV_HEADS * SWA_HEAD_DIM
MQW = MLSTM_HEADS * MLSTM_QK
MVW = MLSTM_HEADS * MLSTM_V


def _cparams(*sem):
    return pltpu.CompilerParams(dimension_semantics=sem, vmem_limit_bytes=V7X_VMEM_LIMIT_BYTES)


def _resident(shape):
    nd = len(shape)
    return pl.BlockSpec(shape, lambda *_: (0,) * nd, pipeline_mode=pl.Buffered(1))


def _layer_resident(stacked, l):
    nd = stacked.ndim - 1
    return pl.BlockSpec((None,) + stacked.shape[1:], lambda *_: (l,) + (0,) * nd,
                        pipeline_mode=pl.Buffered(1))


def _dot(a, b):
    return jnp.dot(a, b, preferred_element_type=F32)


def _dot_nt(a, b):
    return lax.dot_general(a, b, (((1,), (1,)), ((), ())), preferred_element_type=F32)


def _rms(x):
    return x * lax.rsqrt(jnp.mean(x * x, axis=-1, keepdims=True) + NORM_EPS)


def _mod_kernel(c_ref, w_ref, b_ref, o_ref):
    c = c_ref[...]
    a = c * jax.nn.sigmoid(c)
    o_ref[...] = jnp.dot(a, w_ref[...], preferred_element_type=F32,
                         precision=lax.Precision.HIGHEST) + b_ref[...]


def _mod_all(c_all, w_mod, b_mod):
    L, D, N = w_mod.shape
    R = c_all.shape[0]
    tn = min(1024, N)
    return pl.pallas_call(
        _mod_kernel,
        grid=(L, N // tn),
        in_specs=[pl.BlockSpec((R, D), lambda l, j: (0, 0)),
                  pl.BlockSpec((None, D, tn), lambda l, j: (l, 0, j)),
                  pl.BlockSpec((None, 1, tn), lambda l, j: (l, 0, j))],
        out_specs=pl.BlockSpec((None, R, tn), lambda l, j: (l, 0, j)),
        out_shape=jax.ShapeDtypeStruct((L, R, N), F32),
        compiler_params=_cparams("parallel", "parallel"),
        name="mod",
    )(c_all, w_mod, b_mod.reshape(L, 1, N))


def _rope(x, cos, sina, sinb):
    return x * cos + pltpu.roll(x, V7X_LANES - 16, 1) * sina + pltpu.roll(x, 16, 1) * sinb


def _inproj_kernel(x_ref, sh_ref, sc_ref, g1_ref, w_ref, gq_ref, wuq_ref, gkv_ref, wukv_ref,
                   cos_ref, sina_ref, sinb_ref,
                   qm_ref, km_ref, vm_ref, qs_ref, ks_ref, vs_ref, mq_ref, mk_ref, mv_ref, mo_ref,
                   gs_ref):
    tm = x_ref.shape[0]
    h = (_rms(x_ref[...]) * g1_ref[...]) * (1.0 + sc_ref[...]) + sh_ref[...]
    hb = h.astype(BF16)
    cos, sina, sinb = cos_ref[...], sina_ref[...], sinb_ref[...]
    rope = lambda t: _rope(t, cos, sina, sinb)
    proj = lambda c0, c1: _dot(hb, w_ref[:, c0:c1])
    low = lax.broadcasted_iota(jnp.int32, (tm, V7X_LANES), 1) < MLA_ROPE

    small = proj(C_SMALL, C_END)
    gs_ref[...] = small
    k_rope = jnp.where(low, rope(small), 0.0).astype(BF16)

    zqn = (_rms(proj(C_ZQ, C_ZKV)) * gq_ref[...]).astype(BF16)
    for hh in range(MLA_HEADS):
        qa = _dot(zqn, wuq_ref[:, 256 * hh:256 * (hh + 1)]) * MLA_QSCALE
        qm_ref[:, 256 * hh:256 * hh + 128] = qa[:, :128].astype(BF16)
        qm_ref[:, 256 * hh + 128:256 * (hh + 1)] = jnp.where(low, rope(qa[:, 128:]), 0.0).astype(BF16)

    zkvn = (_rms(proj(C_ZKV, C_SQ)) * gkv_ref[...]).astype(BF16)
    kv = _dot(zkvn, wukv_ref[...])
    for hh in range(MLA_HEADS):
        km_ref[:, 256 * hh:256 * hh + 128] = kv[:, 128 * hh:128 * (hh + 1)].astype(BF16)
        km_ref[:, 256 * hh + 128:256 * (hh + 1)] = k_rope
    vm_ref[...] = kv[:, MLA_HEADS * MLA_NOPE:].astype(BF16)

    for c in range(SWA_QW // 512):
        sq = proj(C_SQ + 512 * c, C_SQ + 512 * (c + 1))
        for p in range(4):
            blk = rope(sq[:, 128 * p:128 * (p + 1)]) * SWA_QSCALE
            qs_ref[:, 512 * c + 128 * p:512 * c + 128 * (p + 1)] = blk.astype(BF16)
    for ref, val in ((ks_ref, rope(proj(C_SK, C_SV))), (vs_ref, proj(C_SV, C_MQ))):
        ref[:, :SWA_KW] = val.astype(BF16)
        ref[:, SWA_KW:] = pltpu.roll(val, SWA_HEAD_DIM, 1).astype(BF16)

    mq_ref[...] = (proj(C_MQ, C_MK) * MLSTM_QSCALE).astype(BF16)
    mk_ref[...] = proj(C_MK, C_MV).astype(BF16)
    mv_ref[...] = proj(C_MV, C_MO).astype(BF16)
    mo_ref[...] = proj(C_MO, C_SMALL).astype(BF16)


def _inproj(x, sh, sc, g1, w_ext, gq, wuq, gkv, wukv, cos, sina, sinb, *, l, tm):
    M, D = x.shape
    S = cos.shape[0]
    R = M // sh.shape[0]
    nt = S // tm
    row = lambda i: (i, 0)
    modv = pl.BlockSpec((None, 1, D), lambda i: (i // (R // tm), 0, 0))
    tab = pl.BlockSpec((tm, V7X_LANES), lambda i: (i % nt, 0))
    widths = (MLA_QW, MLA_QW, MLA_VW, SWA_QW, 2 * SWA_KW, 2 * SWA_KW, MQW, MQW, MVW, MVW)
    out_shape = [jax.ShapeDtypeStruct((M, w), BF16) for w in widths]
    out_shape.append(jax.ShapeDtypeStruct((M, V7X_LANES), F32))
    out_specs = [pl.BlockSpec((tm, w), row) for w in widths] + [pl.BlockSpec((tm, V7X_LANES), row)]
    return pl.pallas_call(
        _inproj_kernel,
        grid=(M // tm,),
        in_specs=[pl.BlockSpec((tm, D), row), modv, modv, _resident((1, D)),
                  _layer_resident(w_ext, l), _resident(gq.shape), _layer_resident(wuq, l),
                  _resident(gkv.shape), _layer_resident(wukv, l), tab, tab, tab],
        out_specs=out_specs,
        out_shape=out_shape,
        compiler_params=_cparams("parallel"),
        name="inproj",
    )(x, sh, sc, g1, w_ext, gq, wuq, gkv, wukv, cos, sina, sinb)


def _mla_kernel(*refs, tk, nk, unroll):
    if nk:
        q_ref, k_ref, v_ref, kc_ref, vc_ref, o_ref, m_ref, l_ref, acc_ref = refs
    else:
        q_ref, kc_ref, vc_ref, o_ref, m_ref, l_ref, acc_ref = refs
    q = q_ref[...]
    tq = q.shape[0]
    W = V7X_LANES

    def update(kblk, vblk, first):
        ncol = kblk.shape[0] // W
        s = _dot_nt(q, kblk)
        smax = s[:, 0:W]
        for c in range(1, ncol):
            smax = jnp.maximum(smax, s[:, W * c:W * (c + 1)])
        m_new = jnp.broadcast_to(jnp.max(smax, axis=1, keepdims=True), (tq, W))
        if not first:
            m_prev = m_ref[...]
            m_new = jnp.maximum(m_prev, m_new)
            alpha = jnp.exp2(m_prev - m_new)
        lsum = None
        ps = []
        for c in range(ncol):
            pc = jnp.exp2(s[:, W * c:W * (c + 1)] - m_new)
            lsum = pc if lsum is None else lsum + pc
            ps.append(pc.astype(BF16))
        pv = _dot(jnp.concatenate(ps, axis=1), vblk)
        if first:
            l_ref[...] = lsum
            acc_ref[...] = pv
        else:
            l_ref[...] = alpha * l_ref[...] + lsum
            acc_ref[...] = alpha * acc_ref[...] + pv
        m_ref[...] = m_new

    update(kc_ref[...], vc_ref[...], True)
    if nk:
        def body(j, carry):
            off = pl.multiple_of(j * tk, tk)
            update(k_ref[pl.ds(off, tk), :], v_ref[pl.ds(off, tk), :], False)
            return carry
        lax.fori_loop(0, nk, body, 0, unroll=unroll)
    l = jnp.sum(l_ref[...], axis=1, keepdims=True)
    o_ref[...] = (acc_ref[...] / l).astype(o_ref.dtype)


def _mla(q, kc, vc, k=None, v=None, *, B, tq, tk, unroll=False):
    M = q.shape[0]
    T = M // B
    Lc = kc.shape[0] // B
    nq = T // tq
    H = MLA_HEADS
    qspec = pl.BlockSpec((tq, 256), lambda b, h, i: (b * nq + i, h))
    cspecs = [pl.BlockSpec((Lc, 256), lambda b, h, i: (b, h)),
              pl.BlockSpec((Lc, MLA_V), lambda b, h, i: (b, h))]
    if k is None:
        nk, in_specs, args = 0, [qspec] + cspecs, (q, kc, vc)
    else:
        nk = T // tk
        in_specs = [qspec, pl.BlockSpec((T, 256), lambda b, h, i: (b, h)),
                    pl.BlockSpec((T, MLA_V), lambda b, h, i: (b, h))] + cspecs
        args = (q, k, v, kc, vc)
    return pl.pallas_call(
        functools.partial(_mla_kernel, tk=tk, nk=nk, unroll=unroll),
        grid=(B, H, nq),
        in_specs=in_specs,
        out_specs=pl.BlockSpec((tq, MLA_V), lambda b, h, i: (b * nq + i, h)),
        out_shape=jax.ShapeDtypeStruct((M, MLA_VW), BF16),
        scratch_shapes=[pltpu.VMEM((tq, V7X_LANES), F32), pltpu.VMEM((tq, V7X_LANES), F32),
                        pltpu.VMEM((tq, MLA_V), F32)],
        compiler_params=_cparams("parallel", "parallel", "arbitrary"),
        name="mla_latent" if nk else "mla_ctx",
    )(*args)


SWA_SUB = 128
SWA_SPAN = SWA_SUB + 2 * WINDOW


def _swa_kernel(*refs, local, tq, S):
    if local:
        sink_ref, q_ref, k_ref, v_ref, kc_ref, vc_ref, o_ref = refs
    else:
        sink_ref, q_ref, kc_ref, vc_ref, o_ref = refs
    W = V7X_LANES
    d = SWA_HEAD_DIM
    ppg = SWA_HEADS // SWA_KV_HEADS // 2
    Lc = kc_ref.shape[0]

    def variants(x2):
        low = lax.broadcasted_iota(jnp.int32, (x2.shape[0], W), 1) < d
        x, xs = x2[:, :W], x2[:, W:]
        zero = jnp.zeros_like(x)
        return {(0, 0): jnp.where(low, x, zero), (1, 1): jnp.where(low, zero, x),
                (0, 1): jnp.where(low, zero, xs), (1, 0): jnp.where(low, xs, zero)}

    def with_ones(vv):
        return {ge: jnp.concatenate([t, jnp.ones_like(t)], axis=1) for ge, t in vv.items()}

    kcv = variants(kc_ref[...])
    vcv = with_ones(variants(vc_ref[...]))
    for j in range(tq // SWA_SUB):
        r = slice(SWA_SUB * j, SWA_SUB * (j + 1))
        if local:
            qpos = pl.program_id(1) * tq + SWA_SUB * j
            r0 = pl.multiple_of(jnp.clip(qpos - WINDOW, 0, S - SWA_SPAN), W)
            kv = variants(k_ref[pl.ds(r0, SWA_SPAN), :])
            vv = with_ones(variants(v_ref[pl.ds(r0, SWA_SPAN), :]))
            rel = (lax.broadcasted_iota(jnp.int32, (SWA_SUB, SWA_SPAN), 1)
                   - lax.broadcasted_iota(jnp.int32, (SWA_SUB, SWA_SPAN), 0)) + (r0 - qpos)
            bias = jnp.where(jnp.abs(rel) <= WINDOW, 0.0, NEG_BIG)
            bias = jnp.concatenate([bias] * ppg, axis=0)
        for g in range(SWA_KV_HEADS):
            cols = [slice(W * (ppg * g + pp), W * (ppg * g + pp + 1)) for pp in range(ppg)]
            qs = jnp.concatenate([q_ref[r, c] for c in cols], axis=0)
            pair_out = None
            for e in range(2):
                sink = jnp.concatenate(
                    [jnp.full((SWA_SUB, W), sink_ref[2 * (ppg * g + pp) + e] * LOG2E, F32)
                     for pp in range(ppg)], axis=0)
                s_ctx = _dot_nt(qs, kcv[g, e])
                blocks = [s_ctx[:, W * c:W * (c + 1)] for c in range(Lc // W)]
                if local:
                    s_loc = _dot_nt(qs, kv[g, e]) + bias
                    blocks += [s_loc[:, W * c:W * (c + 1)] for c in range(SWA_SPAN // W)]
                mx = blocks[0]
                for blk in blocks[1:]:
                    mx = jnp.maximum(mx, blk)
                m = jnp.maximum(sink, jnp.broadcast_to(jnp.max(mx, axis=1, keepdims=True), mx.shape))
                ps = [jnp.exp2(blk - m).astype(BF16) for blk in blocks]
                out = _dot(jnp.concatenate(ps[:Lc // W], axis=1), vcv[g, e])
                if local:
                    out = out + _dot(jnp.concatenate(ps[Lc // W:], axis=1), vv[g, e])
                o = out[:, :W] / (out[:, W:] + jnp.exp2(sink - m))
                pair_out = o if pair_out is None else pair_out + o
            for pp, c in enumerate(cols):
                o_ref[r, c] = pair_out[SWA_SUB * pp:SWA_SUB * (pp + 1)].astype(o_ref.dtype)


def _swa(sink, q, kc, vc, k=None, v=None, *, B, tq):
    M = q.shape[0]
    T = M // B
    Lc = kc.shape[0] // B
    local = k is not None
    if not local:
        tq = T
    nq = T // tq
    qspec = pl.BlockSpec((tq, SWA_QW), lambda b, i: (b * nq + i, 0))
    cspec = pl.BlockSpec((Lc, 2 * SWA_KW), lambda b, i: (b, 0))
    in_specs = [pl.BlockSpec(memory_space=pltpu.SMEM), qspec]
    args = [sink, q]
    if local:
        in_specs += [pl.BlockSpec((T, 2 * SWA_KW), lambda b, i: (b, 0))] * 2
        args += [k, v]
    in_specs += [cspec, cspec]
    args += [kc, vc]
    return pl.pallas_call(
        functools.partial(_swa_kernel, local=local, tq=tq, S=T),
        grid=(B, nq),
        in_specs=in_specs,
        out_specs=qspec,
        out_shape=jax.ShapeDtypeStruct((M, SWA_QW), BF16),
        compiler_params=_cparams("parallel", "parallel"),
        name="swa_latent" if local else "swa_ctx",
    )(*args)


def _log_sigmoid(x):
    return jnp.minimum(x, 0.0) - jnp.log1p(jnp.exp(-jnp.abs(x)))


def _mlstm_chunk(q_ref, k_ref, v_ref, g_ref, bias_ref, h_ref, c_scr, m_scr, *, rev):
    L = q_ref.shape[0]
    pre = g_ref[...] + bias_ref[...]
    row = lax.broadcasted_iota(jnp.int32, (L, L), 0)
    col = lax.broadcasted_iota(jnp.int32, (L, L), 1)
    allowed = (col >= row) if rev else (col <= row)
    bcum = jnp.dot(allowed.astype(F32), _log_sigmoid(pre), preferred_element_type=F32,
                   precision=lax.Precision.HIGHEST)
    lane0 = GATE_LANE0 + (2 * MLSTM_HEADS if rev else 0)
    onehot = (lax.broadcasted_iota(jnp.int32, (8, V7X_LANES), 1)
              == lax.broadcasted_iota(jnp.int32, (8, V7X_LANES), 0) + lane0).astype(F32)
    to_rows = lambda t: lax.dot_general(onehot, t, (((1,), (1,)), ((), ())), preferred_element_type=F32,
                                        precision=lax.Precision.HIGHEST)
    pre_rows = to_rows(pre)
    bcum_rows = to_rows(bcum)
    W = V7X_LANES
    rep = lambda t, n: jnp.concatenate([t] * n, axis=1)
    ones_blk = jnp.ones((L, W), BF16)
    last = 0 if rev else L - 1
    dk, dv = MLSTM_QK, MLSTM_V
    for hh in range(MLSTM_HEADS):
        li = lane0 + hh
        lf = li + MLSTM_HEADS
        b_rep = jnp.broadcast_to(bcum[:, lf:lf + 1], (L, W))
        i_rep = jnp.broadcast_to(pre[:, li:li + 1], (L, W))
        b_row = bcum_rows[MLSTM_HEADS + hh:MLSTM_HEADS + hh + 1, :]
        i_row = pre_rows[hh:hh + 1, :]
        g_tot = b_rep[last:last + 1, :]
        m_in = m_scr[hh, 0:1, :]
        c_in = c_scr[hh]

        d_log = jnp.where(allowed, rep(b_rep, L // W) - b_row + i_row, -jnp.inf)
        inter_log = b_rep + m_in
        m_t = jnp.maximum(inter_log, jnp.broadcast_to(jnp.max(d_log, axis=1, keepdims=True), (L, W)))
        q = q_ref[:, dk * hh:dk * (hh + 1)]
        k = k_ref[:, dk * hh:dk * (hh + 1)]
        v_aug = jnp.concatenate([v_ref[:, dv * hh:dv * (hh + 1)], ones_blk], axis=1)
        s = _dot_nt(q, k) * jnp.exp(d_log - rep(m_t, L // W))
        tot = (_dot(s.astype(BF16), v_aug)
               + rep(jnp.exp(inter_log - m_t), 2) * _dot(q, c_in.astype(BF16)))
        h_ref[:, dv * hh:dv * (hh + 1)] = tot[:, :dv] / jnp.maximum(jnp.abs(tot[:, dv:]), jnp.exp(-m_t))

        a = g_tot - b_rep + i_rep
        a_max = jnp.max(a, axis=0, keepdims=True)
        wv = (rep(jnp.exp(a - a_max), 2) * v_aug.astype(F32)).astype(BF16)
        d_c = lax.dot_general(k, wv, (((0,), (0,)), ((), ())),
                              preferred_element_type=F32)
        m_new = jnp.maximum(g_tot + m_in, a_max)
        c_scr[hh] = (rep(jnp.exp(g_tot + m_in - m_new), 2) * c_in
                     + rep(jnp.exp(a_max - m_new), 2) * d_c)
        m_scr[hh] = jnp.broadcast_to(m_new, m_scr.shape[1:])


def _mlstm_kernel(qf_ref, kf_ref, vf_ref, gf_ref, qb_ref, kb_ref, vb_ref, gb_ref, bias_ref, c0_ref, m0_ref,
                  hf_ref, hb_ref, cf_ref, mf_ref, c_scr, m_scr, *, nchunks):
    n = pl.program_id(1)

    @pl.when(n == 0)
    def _():
        c_scr[...] = c0_ref[...]
        m_scr[...] = m0_ref[...]

    _mlstm_chunk(qf_ref, kf_ref, vf_ref, gf_ref, bias_ref, hf_ref, c_scr.at[0], m_scr.at[0], rev=False)
    _mlstm_chunk(qb_ref, kb_ref, vb_ref, gb_ref, bias_ref, hb_ref, c_scr.at[1], m_scr.at[1], rev=True)

    @pl.when(n == nchunks - 1)
    def _():
        cf_ref[...] = c_scr[...]
        mf_ref[...] = m_scr[...]


def _mlstm(q, k, v, g, bias, c0, m0, *, B, L):
    M = q.shape[0]
    N = M // B // L
    H = MLSTM_HEADS
    fwd = lambda b, n: (b * N + n, 0)
    bwd = lambda b, n: (b * N + N - 1 - n, 0)
    data = lambda im: [pl.BlockSpec((L, MQW), im), pl.BlockSpec((L, MQW), im),
                       pl.BlockSpec((L, MVW), im), pl.BlockSpec((L, V7X_LANES), im)]
    cspec = pl.BlockSpec((None, 2, H, MLSTM_QK, 256), lambda b, n: (b, 0, 0, 0, 0))
    mspec = pl.BlockSpec((None, 2, H, 8, V7X_LANES), lambda b, n: (b, 0, 0, 0, 0))
    return pl.pallas_call(
        functools.partial(_mlstm_kernel, nchunks=N),
        grid=(B, N),
        in_specs=data(fwd) + data(bwd) + [_resident(bias.shape), cspec, mspec],
        out_specs=[pl.BlockSpec((L, MVW), fwd), pl.BlockSpec((L, MVW), bwd), cspec, mspec],
        out_shape=[jax.ShapeDtypeStruct((M, MVW), F32), jax.ShapeDtypeStruct((M, MVW), F32),
                   jax.ShapeDtypeStruct(c0.shape, F32), jax.ShapeDtypeStruct(m0.shape, F32)],
        scratch_shapes=[pltpu.VMEM((2, H, MLSTM_QK, 256), F32), pltpu.VMEM((2, H, 8, V7X_LANES), F32)],
        compiler_params=_cparams("parallel", "arbitrary"),
        name="mlstm",
    )(q, k, v, g, q, k, v, g, bias, c0, m0)


def _outproj_kernel(ya_ref, ys_ref, hf_ref, hb_ref, mo_ref, gh_ref, w_ref, x_ref, gt_ref, o_ref, *, tn):
    dv = MLSTM_V
    parts = []
    for hh in range(MLSTM_HEADS):
        sl = slice(dv * hh, dv * (hh + 1))
        hn = _rms(hf_ref[:, sl] + hb_ref[:, sl]) * gh_ref[:, sl]
        parts.append((hn * jax.nn.sigmoid(mo_ref[:, sl].astype(F32))).astype(BF16))
    ym = jnp.concatenate(parts, axis=1)
    ya = ya_ref[...]
    ys = ys_ref[...]
    r1 = MLA_VW
    r2 = r1 + SWA_QW
    for c in range(o_ref.shape[1] // tn):
        cs = slice(tn * c, tn * (c + 1))
        acc = _dot(ya, w_ref[0:r1, cs]) + _dot(ys, w_ref[r1:r2, cs]) + _dot(ym, w_ref[r2:, cs])
        o_ref[:, cs] = x_ref[:, cs] + gt_ref[:, cs] * acc


def _outproj(ya, ys, hf, hb, mo, gh, w_out, x, gt, *, l, tm):
    M, D = x.shape
    R = M // gt.shape[0]
    row = lambda i: (i, 0)
    return pl.pallas_call(
        functools.partial(_outproj_kernel, tn=min(512, D)),
        grid=(M // tm,),
        in_specs=[pl.BlockSpec((tm, MLA_VW), row), pl.BlockSpec((tm, SWA_QW), row),
                  pl.BlockSpec((tm, MVW), row), pl.BlockSpec((tm, MVW), row),
                  pl.BlockSpec((tm, MVW), row), _resident(gh.shape), _layer_resident(w_out, l),
                  pl.BlockSpec((tm, D), row),
                  pl.BlockSpec((None, 1, D), lambda i: (i // (R // tm), 0, 0))],
        out_specs=pl.BlockSpec((tm, D), row),
        out_shape=jax.ShapeDtypeStruct((M, D), F32),
        compiler_params=_cparams("parallel"),
        name="outproj",
    )(ya, ys, hf, hb, mo, gh, w_out, x, gt)


def _ffn_kernel(x_ref, sh_ref, sc_ref, gt_ref, g2_ref, gfin_ref, w1_ref, w2_ref, o_ref, h_scr, *,
                nf, final):
    f = pl.program_id(1)

    @pl.when(f == 0)
    def _():
        h = (_rms(x_ref[...]) * g2_ref[...]) * (1.0 + sc_ref[...]) + sh_ref[...]
        h_scr[...] = h.astype(BF16)
        o_ref[...] = jnp.zeros_like(o_ref)

    u = jnp.maximum(_dot(h_scr[...], w1_ref[...]), 0.0)
    o_ref[...] += _dot((u * u).astype(BF16), w2_ref[...])

    @pl.when(f == nf - 1)
    def _():
        y = x_ref[...] + gt_ref[...] * o_ref[...]
        o_ref[...] = _rms(y) * gfin_ref[...] if final else y


def _ffn(x, sh, sc, gt, g2, gfin, w1, w2, *, l, tm, tf, xbuf, final):
    M, D = x.shape
    FF = w1.shape[2]
    R = M // sh.shape[0]
    nf = FF // tf
    modv = pl.BlockSpec((None, 1, D), lambda i, f: (i // (R // tm), 0, 0))
    return pl.pallas_call(
        functools.partial(_ffn_kernel, nf=nf, final=final),
        grid=(M // tm, nf),
        in_specs=[pl.BlockSpec((tm, D), lambda i, f: (i, 0), pipeline_mode=pl.Buffered(xbuf)),
                  modv, modv, modv, _resident((1, D)), _resident((1, D)),
                  pl.BlockSpec((None, D, tf), lambda i, f: (l, 0, f)),
                  pl.BlockSpec((None, tf, D), lambda i, f: (l, f, 0))],
        out_specs=pl.BlockSpec((tm, D), lambda i, f: (i, 0)),
        out_shape=jax.ShapeDtypeStruct((M, D), F32),
        scratch_shapes=[pltpu.VMEM((tm, D), BF16)],
        compiler_params=_cparams("parallel", "arbitrary"),
        name="ffn",
    )(x, sh, sc, gt, g2, gfin, w1, w2)


def _rope_tables(S):
    half = SWA_HEAD_DIM // 2
    pos = jnp.arange(S)
    inv = ROPE_BASE ** (-jnp.arange(0, half, 2, dtype=F32) / half)
    ar = (pos // GRID_W)[:, None].astype(F32) * inv
    ac = (pos % GRID_W)[:, None].astype(F32) * inv
    ang = jnp.concatenate([ar, ar, ac, ac] * 2, axis=-1)
    first = (jnp.arange(V7X_LANES) % 32) < 16
    sin = jnp.sin(ang)
    return jnp.cos(ang), jnp.where(first, -sin, 0.0), jnp.where(first, 0.0, sin)


def _permute_w_in(w_in):
    o = [0]
    for w in (MLA_Q_RANK, MLA_KV_RANK, MLA_ROPE, SWA_QW, SWA_KW, SWA_KW, MQW, MQW, MVW, N_GATES, MVW):
        o.append(o[-1] + w)
    seg = lambda i: w_in[..., o[i]:o[i + 1]]
    pad = jnp.zeros(w_in.shape[:-1] + (V7X_LANES - MLA_ROPE - N_GATES,), w_in.dtype)
    cols = [seg(0), seg(1), seg(3), seg(4), seg(5), seg(6), seg(7), seg(8), seg(10), seg(2), seg(9), pad]
    return jnp.concatenate(cols, axis=-1).astype(BF16)


def _permute_w_uq(w_uq):
    L, Rk, _ = w_uq.shape
    w = w_uq.reshape(L, Rk, MLA_HEADS, MLA_NOPE + MLA_ROPE)
    w = jnp.pad(w, ((0, 0), (0, 0), (0, 0), (0, 256 - MLA_NOPE - MLA_ROPE)))
    return w.reshape(L, Rk, MLA_QW).astype(BF16)


def _permute_w_ukv(w_ukv):
    L, Rk, _ = w_ukv.shape
    w = w_ukv.reshape(L, Rk, MLA_HEADS, MLA_NOPE + MLA_V)
    k = w[..., :MLA_NOPE].reshape(L, Rk, MLA_HEADS * MLA_NOPE)
    v = w[..., MLA_NOPE:].reshape(L, Rk, MLA_VW)
    return jnp.concatenate([k, v], axis=-1).astype(BF16)


def _row_tile(rows, want):
    return want if rows % want == 0 else rows


def kernel(x, c, ctx, c_ctx, w_mod, b_mod, g_norm1, g_norm2, w_in, mla_g_q, mla_w_uq, mla_g_kv,
           mla_w_ukv, swa_sink, mlstm_gate_bias, mlstm_g_h, w_out, w_ff1, w_ff2, g_final):
    B, S, D = x.shape
    Lc = ctx.shape[1]
    depth = w_in.shape[0]
    H = MLSTM_HEADS

    rows = -(-(B + 1) // 8) * 8
    c_all = jnp.concatenate([c, c_ctx[None, :], jnp.zeros((rows - B - 1, D), F32)], axis=0)
    mod = _mod_all(c_all, w_mod, b_mod)
    mod6 = mod.reshape(depth, rows, 6, D)

    w_in_p = _permute_w_in(w_in)
    w_uq_p = _permute_w_uq(mla_w_uq)
    w_ukv_p = _permute_w_ukv(mla_w_ukv)
    w_out_b = w_out.astype(BF16)
    w1_b = w_ff1.astype(BF16)
    w2_b = w_ff2.astype(BF16)
    cos, sina, sinb = _rope_tables(S)
    ones_t = jnp.ones((Lc, V7X_LANES), F32)
    zeros_t = jnp.zeros((Lc, V7X_LANES), F32)
    bias_lanes = jnp.pad(mlstm_gate_bias.reshape(depth, 1, N_GATES),
                         ((0, 0), (0, 0), (GATE_LANE0, V7X_LANES - GATE_LANE0 - N_GATES)))
    c_zero = jnp.zeros((B, 2, H, MLSTM_QK, 256), F32)
    m_zero = jnp.zeros((B, 2, H, 8, V7X_LANES), F32)

    tm = _row_tile(S, 512)
    FF = w_ff1.shape[2]
    ffn_cfgs = [(512, 1024, 2), (512, 2048, 2), (512, 1024, 2), (512, 1024, 2)]
    mla_cfgs = [(1024, 2048, False), (1024, 2048, True), (1024, 4096, False), (2048, 2048, False)]
    gfin = g_final[None, :]
    tq_swa = _row_tile(S, 512)
    chunk = _row_tile(S, 256)

    xs = x.reshape(B * S, D)
    xc = ctx.reshape(B * Lc, D)
    for l in range(depth):
        need_ctx = l < depth - 1
        vec = lambda j: mod6[l, :B, j][:, None, :]
        vecc = lambda j: mod6[l, B:B + 1, j][:, None, :]
        g1 = g_norm1[l][None, :]
        g2 = g_norm2[l][None, :]
        gq = mla_g_q[l][None, :]
        gkv = mla_g_kv[l][None, :]
        gh = mlstm_g_h[l].reshape(1, MVW)
        inproj = functools.partial(_inproj, g1=g1, w_ext=w_in_p, gq=gq, wuq=w_uq_p, gkv=gkv,
                                   wukv=w_ukv_p, l=l)
        tm_ffn, tf, xbuf = ffn_cfgs[l % len(ffn_cfgs)]
        tm_ffn, tf = _row_tile(S, tm_ffn), min(tf, FF)
        tq_mla, tk_mla, unroll_mla = mla_cfgs[l % len(mla_cfgs)]
        tq_mla, tk_mla = _row_tile(S, tq_mla), _row_tile(S, tk_mla)
        ffn = functools.partial(_ffn, g2=g2, gfin=gfin, w1=w1_b, w2=w2_b, l=l, tf=tf)
        (qmc, kmc, vmc, qsc, ksc, vsc, mqc, mkc, mvc, moc, gsc) = inproj(
            xc, vecc(0), vecc(1), cos=ones_t, sina=zeros_t, sinb=zeros_t, tm=Lc)
        hfc, hbc, c_st, m_st = _mlstm(mqc, mkc, mvc, gsc, bias_lanes[l], c_zero, m_zero, B=B, L=Lc)

        (qm, km, vm, qs, ks, vs, mq, mk, mv, mo, gs) = inproj(
            xs, vec(0), vec(1), cos=cos, sina=sina, sinb=sinb, tm=tm)
        y_mla = _mla(qm, kmc, vmc, km, vm, B=B, tq=tq_mla, tk=tk_mla, unroll=unroll_mla)
        y_swa = _swa(swa_sink[l], qs, ksc, vsc, ks, vs, B=B, tq=tq_swa)
        hf, hb, _, _ = _mlstm(mq, mk, mv, gs, bias_lanes[l], c_st, m_st, B=B, L=chunk)
        xs = _outproj(y_mla, y_swa, hf, hb, mo, gh, w_out_b, xs, vec(2), l=l, tm=tm)
        xs = ffn(xs, vec(3), vec(4), vec(5), tm=tm_ffn, xbuf=xbuf, final=not need_ctx)
        if need_ctx:
            yc_mla = _mla(qmc, kmc, vmc, B=B, tq=Lc, tk=Lc)
            yc_swa = _swa(swa_sink[l], qsc, ksc, vsc, B=B, tq=Lc)
            xc = _outproj(yc_mla, yc_swa, hfc, hbc, moc, gh, w_out_b, xc, vecc(2), l=l, tm=Lc)
            xc = ffn(xc, vecc(3), vecc(4), vecc(5), tm=B * Lc, tf=min(512, FF), xbuf=2, final=False)
    return xs.reshape(B, S, D)
```

```python
import functools
import math

import jax
import jax.numpy as jnp
from jax import lax
from jax.experimental import pallas as pl
from jax.experimental.pallas import tpu as pltpu

F32 = jnp.float32
BF16 = jnp.bfloat16

GRID_W = 64
ROPE_BASE = 10000.0
NORM_EPS = 1e-6
MLA_HEADS = 4
MLA_Q_RANK = 512
MLA_KV_RANK = 256
MLA_NOPE = 128
MLA_ROPE = 64
MLA_V = 128
SWA_HEADS = 16
SWA_KV_HEADS = 2
SWA_HEAD_DIM = 64
WINDOW = 128
MLSTM_HEADS = 4
MLSTM_QK = 64
MLSTM_V = 128
N_GATES = 4 * MLSTM_HEADS

V7X_LANES = 128
V7X_VMEM_LIMIT_BYTES = 60 * 1024 * 1024

LOG2E = math.log2(math.e)
MLA_QSCALE = (MLA_NOPE + MLA_ROPE) ** -0.5 * LOG2E
SWA_QSCALE = SWA_HEAD_DIM ** -0.5 * LOG2E
MLSTM_QSCALE = MLSTM_QK ** -0.5
NEG_BIG = -1e30

C_ZQ = 0
C_ZKV = C_ZQ + MLA_Q_RANK
C_SQ = C_ZKV + MLA_KV_RANK
C_SK = C_SQ + SWA_HEADS * SWA_HEAD_DIM
C_SV = C_SK + SWA_KV_HEADS * SWA_HEAD_DIM
C_MQ = C_SV + SWA_KV_HEADS * SWA_HEAD_DIM
C_MK = C_MQ + MLSTM_HEADS * MLSTM_QK
C_MV = C_MK + MLSTM_HEADS * MLSTM_QK
C_MO = C_MV + MLSTM_HEADS * MLSTM_V
C_SMALL = C_MO + MLSTM_HEADS * MLSTM_V
C_END = C_SMALL + V7X_LANES
GATE_LANE0 = MLA_ROPE

MLA_QW = MLA_HEADS * 256
MLA_VW = MLA_HEADS * MLA_V
SWA_QW = SWA_HEADS * SWA_HEAD_DIM
SWA_KW = SWA_KV_HEADS * SWA_HEAD_DIM
MQW = MLSTM_HEADS * MLSTM_QK
MVW = MLSTM_HEADS * MLSTM_V


def _cparams(*sem):
    return pltpu.CompilerParams(dimension_semantics=sem, vmem_limit_bytes=V7X_VMEM_LIMIT_BYTES)


def _resident(shape):
    nd = len(shape)
    return pl.BlockSpec(shape, lambda *_: (0,) * nd, pipeline_mode=pl.Buffered(1))


def _layer_resident(stacked, l):
    nd = stacked.ndim - 1
    return pl.BlockSpec((None,) + stacked.shape[1:], lambda *_: (l,) + (0,) * nd,
                        pipeline_mode=pl.Buffered(1))


def _dot(a, b):
    return jnp.dot(a, b, preferred_element_type=F32)


def _dot_nt(a, b):
    return lax.dot_general(a, b, (((1,), (1,)), ((), ())), preferred_element_type=F32)


def _rms(x):
    return x * lax.rsqrt(jnp.mean(x * x, axis=-1, keepdims=True) + NORM_EPS)


def _mod_kernel(c_ref, w_ref, b_ref, o_ref):
    c = c_ref[...]
    a = c * jax.nn.sigmoid(c)
    o_ref[...] = jnp.dot(a, w_ref[...], preferred_element_type=F32,
                         precision=lax.Precision.HIGHEST) + b_ref[...]


def _mod_all(c_all, w_mod, b_mod):
    L, D, N = w_mod.shape
    R = c_all.shape[0]
    tn = min(1024, N)
    return pl.pallas_call(
        _mod_kernel,
        grid=(L, N // tn),
        in_specs=[pl.BlockSpec((R, D), lambda l, j: (0, 0)),
                  pl.BlockSpec((None, D, tn), lambda l, j: (l, 0, j)),
                  pl.BlockSpec((None, 1, tn), lambda l, j: (l, 0, j))],
        out_specs=pl.BlockSpec((None, R, tn), lambda l, j: (l, 0, j)),
        out_shape=jax.ShapeDtypeStruct((L, R, N), F32),
        compiler_params=_cparams("parallel", "parallel"),
        name="mod",
    )(c_all, w_mod, b_mod.reshape(L, 1, N))


def _rope(x, cos, sina, sinb):
    return x * cos + pltpu.roll(x, V7X_LANES - 16, 1) * sina + pltpu.roll(x, 16, 1) * sinb


def _inproj_kernel(x_ref, sh_ref, sc_ref, g1_ref, w_ref, gq_ref, wuq_ref, gkv_ref, wukv_ref,
                   cos_ref, sina_ref, sinb_ref,
                   qm_ref, km_ref, vm_ref, qs_ref, ks_ref, vs_ref, mq_ref, mk_ref, mv_ref, mo_ref,
                   gs_ref):
    tm = x_ref.shape[0]
    h = (_rms(x_ref[...]) * g1_ref[...]) * (1.0 + sc_ref[...]) + sh_ref[...]
    hb = h.astype(BF16)
    cos, sina, sinb = cos_ref[...], sina_ref[...], sinb_ref[...]
    rope = lambda t: _rope(t, cos, sina, sinb)
    proj = lambda c0, c1: _dot(hb, w_ref[:, c0:c1])
    low = lax.broadcasted_iota(jnp.int32, (tm, V7X_LANES), 1) < MLA_ROPE

    small = proj(C_SMALL, C_END)
    gs_ref[...] = small
    k_rope = jnp.where(low, rope(small), 0.0).astype(BF16)

    zqn = (_rms(proj(C_ZQ, C_ZKV)) * gq_ref[...]).astype(BF16)
    for hh in range(MLA_HEADS):
        qa = _dot(zqn, wuq_ref[:, 256 * hh:256 * (hh + 1)]) * MLA_QSCALE
        qm_ref[:, 256 * hh:256 * hh + 128] = qa[:, :128].astype(BF16)
        qm_ref[:, 256 * hh + 128:256 * (hh + 1)] = jnp.where(low, rope(qa[:, 128:]), 0.0).astype(BF16)

    zkvn = (_rms(proj(C_ZKV, C_SQ)) * gkv_ref[...]).astype(BF16)
    kv = _dot(zkvn, wukv_ref[...])
    for hh in range(MLA_HEADS):
        km_ref[:, 256 * hh:256 * hh + 128] = kv[:, 128 * hh:128 * (hh + 1)].astype(BF16)
        km_ref[:, 256 * hh + 128:256 * (hh + 1)] = k_rope
    vm_ref[...] = kv[:, MLA_HEADS * MLA_NOPE:].astype(BF16)

    for c in range(SWA_QW // 512):
        sq = proj(C_SQ + 512 * c, C_SQ + 512 * (c + 1))
        for p in range(4):
            blk = rope(sq[:, 128 * p:128 * (p + 1)]) * SWA_QSCALE
            qs_ref[:, 512 * c + 128 * p:512 * c + 128 * (p + 1)] = blk.astype(BF16)
    for ref, val in ((ks_ref, rope(proj(C_SK, C_SV))), (vs_ref, proj(C_SV, C_MQ))):
        ref[:, :SWA_KW] = val.astype(BF16)
        ref[:, SWA_KW:] = pltpu.roll(val, SWA_HEAD_DIM, 1).astype(BF16)

    mq_ref[...] = (proj(C_MQ, C_MK) * MLSTM_QSCALE).astype(BF16)
    mk_ref[...] = proj(C_MK, C_MV).astype(BF16)
    mv_ref[...] = proj(C_MV, C_MO).astype(BF16)
    mo_ref[...] = proj(C_MO, C_SMALL).astype(BF16)


def _inproj(x, sh, sc, g1, w_ext, gq, wuq, gkv, wukv, cos, sina, sinb, *, l, tm):
    M, D = x.shape
    S = cos.shape[0]
    R = M // sh.shape[0]
    nt = S // tm
    row = lambda i: (i, 0)
    modv = pl.BlockSpec((None, 1, D), lambda i: (i // (R // tm), 0, 0))
    tab = pl.BlockSpec((tm, V7X_LANES), lambda i: (i % nt, 0))
    widths = (MLA_QW, MLA_QW, MLA_VW, SWA_QW, 2 * SWA_KW, 2 * SWA_KW, MQW, MQW, MVW, MVW)
    out_shape = [jax.ShapeDtypeStruct((M, w), BF16) for w in widths]
    out_shape.append(jax.ShapeDtypeStruct((M, V7X_LANES), F32))
    out_specs = [pl.BlockSpec((tm, w), row) for w in widths] + [pl.BlockSpec((tm, V7X_LANES), row)]
    return pl.pallas_call(
        _inproj_kernel,
        grid=(M // tm,),
        in_specs=[pl.BlockSpec((tm, D), row), modv, modv, _resident((1, D)),
                  _layer_resident(w_ext, l), _resident(gq.shape), _layer_resident(wuq, l),
                  _resident(gkv.shape), _layer_resident(wukv, l), tab, tab, tab],
        out_specs=out_specs,
        out_shape=out_shape,
        compiler_params=_cparams("parallel"),
        name="inproj",
    )(x, sh, sc, g1, w_ext, gq, wuq, gkv, wukv, cos, sina, sinb)


def _mla_kernel(*refs, tk, nk, unroll):
    if nk:
        q_ref, k_ref, v_ref, kc_ref, vc_ref, o_ref, m_ref, l_ref, acc_ref = refs
    else:
        q_ref, kc_ref, vc_ref, o_ref, m_ref, l_ref, acc_ref = refs
    q = q_ref[...]
    tq = q.shape[0]
    W = V7X_LANES

    def update(kblk, vblk, first):
        ncol = kblk.shape[0] // W
        s = _dot_nt(q, kblk)
        smax = s[:, 0:W]
        for c in range(1, ncol):
            smax = jnp.maximum(smax, s[:, W * c:W * (c + 1)])
        m_new = jnp.broadcast_to(jnp.max(smax, axis=1, keepdims=True), (tq, W))
        if not first:
            m_prev = m_ref[...]
            m_new = jnp.maximum(m_prev, m_new)
            alpha = jnp.exp2(m_prev - m_new)
        lsum = None
        ps = []
        for c in range(ncol):
            pc = jnp.exp2(s[:, W * c:W * (c + 1)] - m_new)
            lsum = pc if lsum is None else lsum + pc
            ps.append(pc.astype(BF16))
        pv = _dot(jnp.concatenate(ps, axis=1), vblk)
        if first:
            l_ref[...] = lsum
            acc_ref[...] = pv
        else:
            l_ref[...] = alpha * l_ref[...] + lsum
            acc_ref[...] = alpha * acc_ref[...] + pv
        m_ref[...] = m_new

    update(kc_ref[...], vc_ref[...], True)
    if nk:
        def body(j, carry):
            off = pl.multiple_of(j * tk, tk)
            update(k_ref[pl.ds(off, tk), :], v_ref[pl.ds(off, tk), :], False)
            return carry
        lax.fori_loop(0, nk, body, 0, unroll=unroll)
    l = jnp.sum(l_ref[...], axis=1, keepdims=True)
    o_ref[...] = (acc_ref[...] / l).astype(o_ref.dtype)


def _mla(q, kc, vc, k=None, v=None, *, B, tq, tk, unroll=False):
    M = q.shape[0]
    T = M // B
    Lc = kc.shape[0] // B
    nq = T // tq
    H = MLA_HEADS
    qspec = pl.BlockSpec((tq, 256), lambda b, h, i: (b * nq + i, h))
    cspecs = [pl.BlockSpec((Lc, 256), lambda b, h, i: (b, h)),
              pl.BlockSpec((Lc, MLA_V), lambda b, h, i: (b, h))]
    if k is None:
        nk, in_specs, args = 0, [qspec] + cspecs, (q, kc, vc)
    else:
        nk = T // tk
        in_specs = [qspec, pl.BlockSpec((T, 256), lambda b, h, i: (b, h)),
                    pl.BlockSpec((T, MLA_V), lambda b, h, i: (b, h))] + cspecs
        args = (q, k, v, kc, vc)
    return pl.pallas_call(
        functools.partial(_mla_kernel, tk=tk, nk=nk, unroll=unroll),
        grid=(B, H, nq),
        in_specs=in_specs,
        out_specs=pl.BlockSpec((tq, MLA_V), lambda b, h, i: (b * nq + i, h)),
        out_shape=jax.ShapeDtypeStruct((M, MLA_VW), BF16),
        scratch_shapes=[pltpu.VMEM((tq, V7X_LANES), F32), pltpu.VMEM((tq, V7X_LANES), F32),
                        pltpu.VMEM((tq, MLA_V), F32)],
        compiler_params=_cparams("parallel", "parallel", "arbitrary"),
        name="mla_latent" if nk else "mla_ctx",
    )(*args)


SWA_SUB = 128
SWA_SPAN = SWA_SUB + 2 * WINDOW


def _swa_kernel(*refs, local, tq, S):
    if local:
        sink_ref, q_ref, k_ref, v_ref, kc_ref, vc_ref, o_ref = refs
    else:
        sink_ref, q_ref, kc_ref, vc_ref, o_ref = refs
    W = V7X_LANES
    d = SWA_HEAD_DIM
    ppg = SWA_HEADS // SWA_KV_HEADS // 2
    Lc = kc_ref.shape[0]

    def variants(x2):
        low = lax.broadcasted_iota(jnp.int32, (x2.shape[0], W), 1) < d
        x, xs = x2[:, :W], x2[:, W:]
        zero = jnp.zeros_like(x)
        return {(0, 0): jnp.where(low, x, zero), (1, 1): jnp.where(low, zero, x),
                (0, 1): jnp.where(low, zero, xs), (1, 0): jnp.where(low, xs, zero)}

    def with_ones(vv):
        return {ge: jnp.concatenate([t, jnp.ones_like(t)], axis=1) for ge, t in vv.items()}

    kcv = variants(kc_ref[...])
    vcv = with_ones(variants(vc_ref[...]))
    for j in range(tq // SWA_SUB):
        r = slice(SWA_SUB * j, SWA_SUB * (j + 1))
        if local:
            qpos = pl.program_id(1) * tq + SWA_SUB * j
            r0 = pl.multiple_of(jnp.clip(qpos - WINDOW, 0, S - SWA_SPAN), W)
            kv = variants(k_ref[pl.ds(r0, SWA_SPAN), :])
            vv = with_ones(variants(v_ref[pl.ds(r0, SWA_SPAN), :]))
            rel = (lax.broadcasted_iota(jnp.int32, (SWA_SUB, SWA_SPAN), 1)
                   - lax.broadcasted_iota(jnp.int32, (SWA_SUB, SWA_SPAN), 0)) + (r0 - qpos)
            bias = jnp.where(jnp.abs(rel) <= WINDOW, 0.0, NEG_BIG)
            bias = jnp.concatenate([bias] * ppg, axis=0)
        for g in range(SWA_KV_HEADS):
            cols = [slice(W * (ppg * g + pp), W * (ppg * g + pp + 1)) for pp in range(ppg)]
            qs = jnp.concatenate([q_ref[r, c] for c in cols], axis=0)
            pair_out = None
            for e in range(2):
                sink = jnp.concatenate(
                    [jnp.full((SWA_SUB, W), sink_ref[2 * (ppg * g + pp) + e] * LOG2E, F32)
                     for pp in range(ppg)], axis=0)
                s_ctx = _dot_nt(qs, kcv[g, e])
                blocks = [s_ctx[:, W * c:W * (c + 1)] for c in range(Lc // W)]
                if local:
                    s_loc = _dot_nt(qs, kv[g, e]) + bias
                    blocks += [s_loc[:, W * c:W * (c + 1)] for c in range(SWA_SPAN // W)]
                mx = blocks[0]
                for blk in blocks[1:]:
                    mx = jnp.maximum(mx, blk)
                m = jnp.maximum(sink, jnp.broadcast_to(jnp.max(mx, axis=1, keepdims=True), mx.shape))
                ps = [jnp.exp2(blk - m).astype(BF16) for blk in blocks]
                out = _dot(jnp.concatenate(ps[:Lc // W], axis=1), vcv[g, e])
                if local:
                    out = out + _dot(jnp.concatenate(ps[Lc // W:], axis=1), vv[g, e])
                o = out[:, :W] / (out[:, W:] + jnp.exp2(sink - m))
                pair_out = o if pair_out is None else pair_out + o
            for pp, c in enumerate(cols):
                o_ref[r, c] = pair_out[SWA_SUB * pp:SWA_SUB * (pp + 1)].astype(o_ref.dtype)


def _swa(sink, q, kc, vc, k=None, v=None, *, B, tq):
    M = q.shape[0]
    T = M // B
    Lc = kc.shape[0] // B
    local = k is not None
    if not local:
        tq = T
    nq = T // tq
    qspec = pl.BlockSpec((tq, SWA_QW), lambda b, i: (b * nq + i, 0))
    cspec = pl.BlockSpec((Lc, 2 * SWA_KW), lambda b, i: (b, 0))
    in_specs = [pl.BlockSpec(memory_space=pltpu.SMEM), qspec]
    args = [sink, q]
    if local:
        in_specs += [pl.BlockSpec((T, 2 * SWA_KW), lambda b, i: (b, 0))] * 2
        args += [k, v]
    in_specs += [cspec, cspec]
    args += [kc, vc]
    return pl.pallas_call(
        functools.partial(_swa_kernel, local=local, tq=tq, S=T),
        grid=(B, nq),
        in_specs=in_specs,
        out_specs=qspec,
        out_shape=jax.ShapeDtypeStruct((M, SWA_QW), BF16),
        compiler_params=_cparams("parallel", "parallel"),
        name="swa_latent" if local else "swa_ctx",
    )(*args)


def _log_sigmoid(x):
    return jnp.minimum(x, 0.0) - jnp.log1p(jnp.exp(-jnp.abs(x)))


def _split3(x):
    hi = x.astype(BF16)
    r1 = x - hi.astype(F32)
    mid = r1.astype(BF16)
    return hi, mid, (r1 - mid.astype(F32)).astype(BF16)


def _mlstm_chunk(q_ref, k_ref, v_ref, g_ref, bias_ref, h_ref, c_scr, m_scr, *, rev):
    L = q_ref.shape[0]
    W = V7X_LANES
    H = MLSTM_HEADS
    dk, dv = MLSTM_QK, MLSTM_V
    pre = g_ref[...] + bias_ref[...]
    row = lax.broadcasted_iota(jnp.int32, (L, L), 0)
    col = lax.broadcasted_iota(jnp.int32, (L, L), 1)
    allowed = (col >= row) if rev else (col <= row)
    tri = jnp.where(allowed, 1.0, 0.0).astype(BF16)
    mask_bias = jnp.where(allowed, 0.0, -jnp.inf)
    acc3 = _dot(tri, jnp.concatenate(_split3(_log_sigmoid(pre)), axis=1))
    bcum = acc3[:, :W] + acc3[:, W:2 * W] + acc3[:, 2 * W:]
    lane0 = GATE_LANE0 + (2 * H if rev else 0)
    z = pre - pltpu.roll(bcum, W - H, 1)
    onehot = jnp.where(lax.broadcasted_iota(jnp.int32, (8, W), 1)
                       == lax.broadcasted_iota(jnp.int32, (8, W), 0) + lane0, 1.0, 0.0).astype(BF16)
    zr = _dot_nt(onehot, jnp.concatenate(_split3(z), axis=0))
    y_rows = zr[:, :L] + zr[:, L:2 * L] + zr[:, 2 * L:]
    rep = lambda t, n: jnp.concatenate([t] * n, axis=1)
    ones_blk = jnp.ones((L, W), BF16)
    last = 0 if rev else L - 1
    for hh in range(H):
        li = lane0 + hh
        b_rep = jnp.broadcast_to(bcum[:, li + H:li + H + 1], (L, W))
        i_rep = jnp.broadcast_to(pre[:, li:li + 1], (L, W))
        g_tot = b_rep[last:last + 1, :]
        m_in = m_scr[hh, 0:1, :]
        c_in = c_scr[hh]

        y = y_rows[hh:hh + 1, :] + mask_bias
        c = -jnp.maximum(m_in, jnp.broadcast_to(jnp.max(y, axis=1, keepdims=True), (L, W)))
        q = q_ref[:, dk * hh:dk * (hh + 1)]
        k = k_ref[:, dk * hh:dk * (hh + 1)]
        v_aug = jnp.concatenate([v_ref[:, dv * hh:dv * (hh + 1)], ones_blk], axis=1)
        s = _dot_nt(q, k) * jnp.exp(y + rep(c, L // W))
        tot = _dot(s.astype(BF16), v_aug) + rep(jnp.exp(m_in + c), 2) * _dot(q, c_in.astype(BF16))
        h_ref[:, dv * hh:dv * (hh + 1)] = tot[:, :dv] / jnp.maximum(jnp.abs(tot[:, dv:]), jnp.exp(c - b_rep))

        a = g_tot - b_rep + i_rep
        a_max = jnp.max(a, axis=0, keepdims=True)
        kw = (k.astype(F32) * jnp.exp(a - a_max)[:, :dk]).astype(BF16)
        d_c = lax.dot_general(kw, v_aug, (((0,), (0,)), ((), ())),
                              preferred_element_type=F32)
        m_new = jnp.maximum(g_tot + m_in, a_max)
        c_scr[hh] = (rep(jnp.exp(g_tot + m_in - m_new), 2) * c_in
                     + rep(jnp.exp(a_max - m_new), 2) * d_c)
        m_scr[hh] = jnp.broadcast_to(m_new, m_scr.shape[1:])


def _mlstm_kernel(qf_ref, kf_ref, vf_ref, gf_ref, qb_ref, kb_ref, vb_ref, gb_ref, bias_ref, c0_ref, m0_ref,
                  hf_ref, hb_ref, cf_ref, mf_ref, c_scr, m_scr, *, nchunks):
    n = pl.program_id(1)

    @pl.when(n == 0)
    def _():
        c_scr[...] = c0_ref[...]
        m_scr[...] = m0_ref[...]

    _mlstm_chunk(qf_ref, kf_ref, vf_ref, gf_ref, bias_ref, hf_ref, c_scr.at[0], m_scr.at[0], rev=False)
    _mlstm_chunk(qb_ref, kb_ref, vb_ref, gb_ref, bias_ref, hb_ref, c_scr.at[1], m_scr.at[1], rev=True)

    @pl.when(n == nchunks - 1)
    def _():
        cf_ref[...] = c_scr[...]
        mf_ref[...] = m_scr[...]


def _mlstm(q, k, v, g, bias, c0, m0, *, B, L):
    M = q.shape[0]
    N = M // B // L
    H = MLSTM_HEADS
    fwd = lambda b, n: (b * N + n, 0)
    bwd = lambda b, n: (b * N + N - 1 - n, 0)
    data = lambda im: [pl.BlockSpec((L, MQW), im), pl.BlockSpec((L, MQW), im),
                       pl.BlockSpec((L, MVW), im), pl.BlockSpec((L, V7X_LANES), im)]
    cspec = pl.BlockSpec((None, 2, H, MLSTM_QK, 256), lambda b, n: (b, 0, 0, 0, 0))
    mspec = pl.BlockSpec((None, 2, H, 8, V7X_LANES), lambda b, n: (b, 0, 0, 0, 0))
    return pl.pallas_call(
        functools.partial(_mlstm_kernel, nchunks=N),
        grid=(B, N),
        in_specs=data(fwd) + data(bwd) + [_resident(bias.shape), cspec, mspec],
        out_specs=[pl.BlockSpec((L, MVW), fwd), pl.BlockSpec((L, MVW), bwd), cspec, mspec],
        out_shape=[jax.ShapeDtypeStruct((M, MVW), F32), jax.ShapeDtypeStruct((M, MVW), F32),
                   jax.ShapeDtypeStruct(c0.shape, F32), jax.ShapeDtypeStruct(m0.shape, F32)],
        scratch_shapes=[pltpu.VMEM((2, H, MLSTM_QK, 256), F32), pltpu.VMEM((2, H, 8, V7X_LANES), F32)],
        compiler_params=_cparams("parallel", "arbitrary"),
        name="mlstm",
    )(q, k, v, g, q, k, v, g, bias, c0, m0)


def _outproj_kernel(ya_ref, ys_ref, hf_ref, hb_ref, mo_ref, gh_ref, w_ref, x_ref, gt_ref, o_ref, *, tn):
    dv = MLSTM_V
    parts = []
    for hh in range(MLSTM_HEADS):
        sl = slice(dv * hh, dv * (hh + 1))
        hn = _rms(hf_ref[:, sl] + hb_ref[:, sl]) * gh_ref[:, sl]
        parts.append((hn * jax.nn.sigmoid(mo_ref[:, sl].astype(F32))).astype(BF16))
    ym = jnp.concatenate(parts, axis=1)
    ya = ya_ref[...]
    ys = ys_ref[...]
    r1 = MLA_VW
    r2 = r1 + SWA_QW
    for c in range(o_ref.shape[1] // tn):
        cs = slice(tn * c, tn * (c + 1))
        acc = _dot(ya, w_ref[0:r1, cs]) + _dot(ys, w_ref[r1:r2, cs]) + _dot(ym, w_ref[r2:, cs])
        o_ref[:, cs] = x_ref[:, cs] + gt_ref[:, cs] * acc


def _outproj(ya, ys, hf, hb, mo, gh, w_out, x, gt, *, l, tm):
    M, D = x.shape
    R = M // gt.shape[0]
    row = lambda i: (i, 0)
    return pl.pallas_call(
        functools.partial(_outproj_kernel, tn=min(512, D)),
        grid=(M // tm,),
        in_specs=[pl.BlockSpec((tm, MLA_VW), row), pl.BlockSpec((tm, SWA_QW), row),
                  pl.BlockSpec((tm, MVW), row), pl.BlockSpec((tm, MVW), row),
                  pl.BlockSpec((tm, MVW), row), _resident(gh.shape), _layer_resident(w_out, l),
                  pl.BlockSpec((tm, D), row),
                  pl.BlockSpec((None, 1, D), lambda i: (i // (R // tm), 0, 0))],
        out_specs=pl.BlockSpec((tm, D), row),
        out_shape=jax.ShapeDtypeStruct((M, D), F32),
        compiler_params=_cparams("parallel"),
        name="outproj",
    )(ya, ys, hf, hb, mo, gh, w_out, x, gt)


def _ffn_kernel(x_ref, sh_ref, sc_ref, gt_ref, g2_ref, gfin_ref, w1_ref, w2_ref, o_ref, h_scr, *,
                nf, final):
    f = pl.program_id(1)

    @pl.when(f == 0)
    def _():
        h = (_rms(x_ref[...]) * g2_ref[...]) * (1.0 + sc_ref[...]) + sh_ref[...]
        h_scr[...] = h.astype(BF16)
        o_ref[...] = jnp.zeros_like(o_ref)

    u = jnp.maximum(_dot(h_scr[...], w1_ref[...]), 0.0)
    o_ref[...] += _dot((u * u).astype(BF16), w2_ref[...])

    @pl.when(f == nf - 1)
    def _():
        y = x_ref[...] + gt_ref[...] * o_ref[...]
        o_ref[...] = _rms(y) * gfin_ref[...] if final else y


def _ffn(x, sh, sc, gt, g2, gfin, w1, w2, *, l, tm, tf, xbuf, final):
    M, D = x.shape
    FF = w1.shape[2]
    R = M // sh.shape[0]
    nf = FF // tf
    modv = pl.BlockSpec((None, 1, D), lambda i, f: (i // (R // tm), 0, 0))
    return pl.pallas_call(
        functools.partial(_ffn_kernel, nf=nf, final=final),
        grid=(M // tm, nf),
        in_specs=[pl.BlockSpec((tm, D), lambda i, f: (i, 0), pipeline_mode=pl.Buffered(xbuf)),
                  modv, modv, modv, _resident((1, D)), _resident((1, D)),
                  pl.BlockSpec((None, D, tf), lambda i, f: (l, 0, f)),
                  pl.BlockSpec((None, tf, D), lambda i, f: (l, f, 0))],
        out_specs=pl.BlockSpec((tm, D), lambda i, f: (i, 0)),
        out_shape=jax.ShapeDtypeStruct((M, D), F32),
        scratch_shapes=[pltpu.VMEM((tm, D), BF16)],
        compiler_params=_cparams("parallel", "arbitrary"),
        name="ffn",
    )(x, sh, sc, gt, g2, gfin, w1, w2)


def _rope_tables(S):
    half = SWA_HEAD_DIM // 2
    pos = jnp.arange(S)
    inv = ROPE_BASE ** (-jnp.arange(0, half, 2, dtype=F32) / half)
    ar = (pos // GRID_W)[:, None].astype(F32) * inv
    ac = (pos % GRID_W)[:, None].astype(F32) * inv
    ang = jnp.concatenate([ar, ar, ac, ac] * 2, axis=-1)
    first = (jnp.arange(V7X_LANES) % 32) < 16
    sin = jnp.sin(ang)
    return jnp.cos(ang), jnp.where(first, -sin, 0.0), jnp.where(first, 0.0, sin)


def _permute_w_in(w_in):
    o = [0]
    for w in (MLA_Q_RANK, MLA_KV_RANK, MLA_ROPE, SWA_QW, SWA_KW, SWA_KW, MQW, MQW, MVW, N_GATES, MVW):
        o.append(o[-1] + w)
    seg = lambda i: w_in[..., o[i]:o[i + 1]]
    pad = jnp.zeros(w_in.shape[:-1] + (V7X_LANES - MLA_ROPE - N_GATES,), w_in.dtype)
    cols = [seg(0), seg(1), seg(3), seg(4), seg(5), seg(6), seg(7), seg(8), seg(10), seg(2), seg(9), pad]
    return jnp.concatenate(cols, axis=-1).astype(BF16)


def _permute_w_uq(w_uq):
    L, Rk, _ = w_uq.shape
    w = w_uq.reshape(L, Rk, MLA_HEADS, MLA_NOPE + MLA_ROPE)
    w = jnp.pad(w, ((0, 0), (0, 0), (0, 0), (0, 256 - MLA_NOPE - MLA_ROPE)))
    return w.reshape(L, Rk, MLA_QW).astype(BF16)


def _permute_w_ukv(w_ukv):
    L, Rk, _ = w_ukv.shape
    w = w_ukv.reshape(L, Rk, MLA_HEADS, MLA_NOPE + MLA_V)
    k = w[..., :MLA_NOPE].reshape(L, Rk, MLA_HEADS * MLA_NOPE)
    v = w[..., MLA_NOPE:].reshape(L, Rk, MLA_VW)
    return jnp.concatenate([k, v], axis=-1).astype(BF16)


def _row_tile(rows, want):
    return want if rows % want == 0 else rows


def kernel(x, c, ctx, c_ctx, w_mod, b_mod, g_norm1, g_norm2, w_in, mla_g_q, mla_w_uq, mla_g_kv,
           mla_w_ukv, swa_sink, mlstm_gate_bias, mlstm_g_h, w_out, w_ff1, w_ff2, g_final):
    B, S, D = x.shape
    Lc = ctx.shape[1]
    depth = w_in.shape[0]
    H = MLSTM_HEADS

    rows = -(-(B + 1) // 8) * 8
    c_all = jnp.concatenate([c, c_ctx[None, :], jnp.zeros((rows - B - 1, D), F32)], axis=0)
    mod = _mod_all(c_all, w_mod, b_mod)
    mod6 = mod.reshape(depth, rows, 6, D)

    w_in_p = _permute_w_in(w_in)
    w_uq_p = _permute_w_uq(mla_w_uq)
    w_ukv_p = _permute_w_ukv(mla_w_ukv)
    w_out_b = w_out.astype(BF16)
    w1_b = w_ff1.astype(BF16)
    w2_b = w_ff2.astype(BF16)
    cos, sina, sinb = _rope_tables(S)
    ones_t = jnp.ones((Lc, V7X_LANES), F32)
    zeros_t = jnp.zeros((Lc, V7X_LANES), F32)
    bias_lanes = jnp.pad(mlstm_gate_bias.reshape(depth, 1, N_GATES),
                         ((0, 0), (0, 0), (GATE_LANE0, V7X_LANES - GATE_LANE0 - N_GATES)))
    c_zero = jnp.zeros((B, 2, H, MLSTM_QK, 256), F32)
    m_zero = jnp.zeros((B, 2, H, 8, V7X_LANES), F32)

    tm = _row_tile(S, 512)
    FF = w_ff1.shape[2]
    ffn_cfgs = [(512, 2048, 2)]
    mla_cfgs = [(4096, 1024, False), (2048, 2048, False), (2048, 2048, True), (2048, 2048, False)]
    gfin = g_final[None, :]
    tq_swa = _row_tile(S, 512)
    chunk = _row_tile(S, 256)

    xs = x.reshape(B * S, D)
    xc = ctx.reshape(B * Lc, D)
    for l in range(depth):
        need_ctx = l < depth - 1
        vec = lambda j: mod6[l, :B, j][:, None, :]
        vecc = lambda j: mod6[l, B:B + 1, j][:, None, :]
        g1 = g_norm1[l][None, :]
        g2 = g_norm2[l][None, :]
        gq = mla_g_q[l][None, :]
        gkv = mla_g_kv[l][None, :]
        gh = mlstm_g_h[l].reshape(1, MVW)
        inproj = functools.partial(_inproj, g1=g1, w_ext=w_in_p, gq=gq, wuq=w_uq_p, gkv=gkv,
                                   wukv=w_ukv_p, l=l)
        tm_ffn, tf, xbuf = ffn_cfgs[l % len(ffn_cfgs)]
        tm_ffn, tf = _row_tile(S, tm_ffn), min(tf, FF)
        tq_mla, tk_mla, unroll_mla = mla_cfgs[l % len(mla_cfgs)]
        tq_mla, tk_mla = _row_tile(S, tq_mla), _row_tile(S, tk_mla)
        ffn = functools.partial(_ffn, g2=g2, gfin=gfin, w1=w1_b, w2=w2_b, l=l, tf=tf)
        (qmc, kmc, vmc, qsc, ksc, vsc, mqc, mkc, mvc, moc, gsc) = inproj(
            xc, vecc(0), vecc(1), cos=ones_t, sina=zeros_t, sinb=zeros_t, tm=Lc)
        hfc, hbc, c_st, m_st = _mlstm(mqc, mkc, mvc, gsc, bias_lanes[l], c_zero, m_zero, B=B, L=Lc)

        (qm, km, vm, qs, ks, vs, mq, mk, mv, mo, gs) = inproj(
            xs, vec(0), vec(1), cos=cos, sina=sina, sinb=sinb, tm=tm)
        y_mla = _mla(qm, kmc, vmc, km, vm, B=B, tq=tq_mla, tk=tk_mla, unroll=unroll_mla)
        y_swa = _swa(swa_sink[l], qs, ksc, vsc, ks, vs, B=B, tq=tq_swa)
        hf, hb, _, _ = _mlstm(mq, mk, mv, gs, bias_lanes[l], c_st, m_st, B=B, L=chunk)
        xs = _outproj(y_mla, y_swa, hf, hb, mo, gh, w_out_b, xs, vec(2), l=l, tm=tm)
        xs = ffn(xs, vec(3), vec(4), vec(5), tm=tm_ffn, xbuf=xbuf, final=not need_ctx)
        if need_ctx:
            yc_mla = _mla(qmc, kmc, vmc, B=B, tq=Lc, tk=Lc)
            yc_swa = _swa(swa_sink[l], qsc, ksc, vsc, B=B, tq=Lc)
            xc = _outproj(yc_mla, yc_swa, hfc, hbc, moc, gh, w_out_b, xc, vecc(2), l=l, tm=Lc)
            xc = ffn(xc, vecc(3), vecc(4), vecc(5), tm=B * Lc, tf=min(512, FF), xbuf=2, final=False)
    return xs.reshape(B, S, D)
```

```python
import functools
import math

import jax
import jax.numpy as jnp
from jax import lax
from jax.experimental import pallas as pl
from jax.experimental.pallas import tpu as pltpu

F32 = jnp.float32
BF16 = jnp.bfloat16

GRID_W = 64
ROPE_BASE = 10000.0
NORM_EPS = 1e-6
MLA_HEADS = 4
MLA_Q_RANK = 512
MLA_KV_RANK = 256
MLA_NOPE = 128
MLA_ROPE = 64
MLA_V = 128
SWA_HEADS = 16
SWA_KV_HEADS = 2
SWA_HEAD_DIM = 64
WINDOW = 128
MLSTM_HEADS = 4
MLSTM_QK = 64
MLSTM_V = 128
N_GATES = 4 * MLSTM_HEADS

V7X_LANES = 128
V7X_VMEM_LIMIT_BYTES = 60 * 1024 * 1024

LOG2E = math.log2(math.e)
MLA_QSCALE = (MLA_NOPE + MLA_ROPE) ** -0.5 * LOG2E
SWA_QSCALE = SWA_HEAD_DIM ** -0.5 * LOG2E
MLSTM_QSCALE = MLSTM_QK ** -0.5
NEG_BIG = -1e30

C_ZQ = 0
C_ZKV = C_ZQ + MLA_Q_RANK
C_SQ = C_ZKV + MLA_KV_RANK
C_SK = C_SQ + SWA_HEADS * SWA_HEAD_DIM
C_SV = C_SK + SWA_KV_HEADS * SWA_HEAD_DIM
C_MQ = C_SV + SWA_KV_HEADS * SWA_HEAD_DIM
C_MK = C_MQ + MLSTM_HEADS * MLSTM_QK
C_MV = C_MK + MLSTM_HEADS * MLSTM_QK
C_MO = C_MV + MLSTM_HEADS * MLSTM_V
C_SMALL = C_MO + MLSTM_HEADS * MLSTM_V
C_END = C_SMALL + V7X_LANES
GATE_LANE0 = MLA_ROPE

MLA_QW = MLA_HEADS * 256
MLA_VW = MLA_HEADS * MLA_V
SWA_QW = SWA_HEADS * SWA_HEAD_DIM
SWA_KW = SWA_KV_HEADS * SWA_HEAD_DIM
MQW = MLSTM_HEADS * MLSTM_QK
MVW = MLSTM_HEADS * MLSTM_V


def _cparams(*sem):
    return pltpu.CompilerParams(dimension_semantics=sem, vmem_limit_bytes=V7X_VMEM_LIMIT_BYTES)


def _resident(shape):
    nd = len(shape)
    return pl.BlockSpec(shape, lambda *_: (0,) * nd, pipeline_mode=pl.Buffered(1))


def _layer_resident(stacked, l):
    nd = stacked.ndim - 1
    return pl.BlockSpec((None,) + stacked.shape[1:], lambda *_: (l,) + (0,) * nd,
                        pipeline_mode=pl.Buffered(1))


def _dot(a, b):
    return jnp.dot(a, b, preferred_element_type=F32)


def _dot_nt(a, b):
    return lax.dot_general(a, b, (((1,), (1,)), ((), ())), preferred_element_type=F32)


def _rms(x):
    return x * lax.rsqrt(jnp.mean(x * x, axis=-1, keepdims=True) + NORM_EPS)


def _mod_kernel(c_ref, w_ref, b_ref, o_ref):
    c = c_ref[...]
    a = c * jax.nn.sigmoid(c)
    o_ref[...] = jnp.dot(a, w_ref[...], preferred_element_type=F32,
                         precision=lax.Precision.HIGHEST) + b_ref[...]


def _mod_all(c_all, w_mod, b_mod):
    L, D, N = w_mod.shape
    R = c_all.shape[0]
    tn = min(1024, N)
    return pl.pallas_call(
        _mod_kernel,
        grid=(L, N // tn),
        in_specs=[pl.BlockSpec((R, D), lambda l, j: (0, 0)),
                  pl.BlockSpec((None, D, tn), lambda l, j: (l, 0, j)),
                  pl.BlockSpec((None, 1, tn), lambda l, j: (l, 0, j))],
        out_specs=pl.BlockSpec((None, R, tn), lambda l, j: (l, 0, j)),
        out_shape=jax.ShapeDtypeStruct((L, R, N), F32),
        compiler_params=_cparams("parallel", "parallel"),
        name="mod",
    )(c_all, w_mod, b_mod.reshape(L, 1, N))


def _rope(x, cos, sina, sinb):
    return x * cos + pltpu.roll(x, V7X_LANES - 16, 1) * sina + pltpu.roll(x, 16, 1) * sinb


def _inproj_kernel(x_ref, sh_ref, sc_ref, g1_ref, w_ref, gq_ref, wuq_ref, gkv_ref, wukv_ref,
                   cos_ref, sina_ref, sinb_ref,
                   qm_ref, km_ref, vm_ref, qs_ref, ks_ref, vs_ref, mq_ref, mk_ref, mv_ref, mo_ref,
                   gs_ref):
    tm = x_ref.shape[0]
    h = (_rms(x_ref[...]) * g1_ref[...]) * (1.0 + sc_ref[...]) + sh_ref[...]
    hb = h.astype(BF16)
    cos, sina, sinb = cos_ref[...], sina_ref[...], sinb_ref[...]
    rope = lambda t: _rope(t, cos, sina, sinb)
    proj = lambda c0, c1: _dot(hb, w_ref[:, c0:c1])
    low = lax.broadcasted_iota(jnp.int32, (tm, V7X_LANES), 1) < MLA_ROPE

    small = proj(C_SMALL, C_END)
    gs_ref[...] = small
    k_rope = jnp.where(low, rope(small), 0.0).astype(BF16)

    zqn = (_rms(proj(C_ZQ, C_ZKV)) * gq_ref[...]).astype(BF16)
    for hh in range(MLA_HEADS):
        qa = _dot(zqn, wuq_ref[:, 256 * hh:256 * (hh + 1)]) * MLA_QSCALE
        qm_ref[:, 256 * hh:256 * hh + 128] = qa[:, :128].astype(BF16)
        qm_ref[:, 256 * hh + 128:256 * (hh + 1)] = jnp.where(low, rope(qa[:, 128:]), 0.0).astype(BF16)

    zkvn = (_rms(proj(C_ZKV, C_SQ)) * gkv_ref[...]).astype(BF16)
    kv = _dot(zkvn, wukv_ref[...])
    for hh in range(MLA_HEADS):
        km_ref[:, 256 * hh:256 * hh + 128] = kv[:, 128 * hh:128 * (hh + 1)].astype(BF16)
        km_ref[:, 256 * hh + 128:256 * (hh + 1)] = k_rope
    vm_ref[...] = kv[:, MLA_HEADS * MLA_NOPE:].astype(BF16)

    for c in range(SWA_QW // 512):
        sq = proj(C_SQ + 512 * c, C_SQ + 512 * (c + 1))
        for p in range(4):
            blk = rope(sq[:, 128 * p:128 * (p + 1)]) * SWA_QSCALE
            qs_ref[:, 512 * c + 128 * p:512 * c + 128 * (p + 1)] = blk.astype(BF16)
    for ref, val in ((ks_ref, rope(proj(C_SK, C_SV))), (vs_ref, proj(C_SV, C_MQ))):
        ref[:, :SWA_KW] = val.astype(BF16)
        ref[:, SWA_KW:] = pltpu.roll(val, SWA_HEAD_DIM, 1).astype(BF16)

    mq_ref[...] = (proj(C_MQ, C_MK) * MLSTM_QSCALE).astype(BF16)
    mk_ref[...] = proj(C_MK, C_MV).astype(BF16)
    mv_ref[...] = proj(C_MV, C_MO).astype(BF16)
    mo_ref[...] = proj(C_MO, C_SMALL).astype(BF16)


def _inproj(x, sh, sc, g1, w_ext, gq, wuq, gkv, wukv, cos, sina, sinb, *, l, tm):
    M, D = x.shape
    S = cos.shape[0]
    R = M // sh.shape[0]
    nt = S // tm
    row = lambda i: (i, 0)
    modv = pl.BlockSpec((None, 1, D), lambda i: (i // (R // tm), 0, 0))
    tab = pl.BlockSpec((tm, V7X_LANES), lambda i: (i % nt, 0))
    widths = (MLA_QW, MLA_QW, MLA_VW, SWA_QW, 2 * SWA_KW, 2 * SWA_KW, MQW, MQW, MVW, MVW)
    out_shape = [jax.ShapeDtypeStruct((M, w), BF16) for w in widths]
    out_shape.append(jax.ShapeDtypeStruct((M, V7X_LANES), F32))
    out_specs = [pl.BlockSpec((tm, w), row) for w in widths] + [pl.BlockSpec((tm, V7X_LANES), row)]
    return pl.pallas_call(
        _inproj_kernel,
        grid=(M // tm,),
        in_specs=[pl.BlockSpec((tm, D), row), modv, modv, _resident((1, D)),
                  _layer_resident(w_ext, l), _resident(gq.shape), _layer_resident(wuq, l),
                  _resident(gkv.shape), _layer_resident(wukv, l), tab, tab, tab],
        out_specs=out_specs,
        out_shape=out_shape,
        compiler_params=_cparams("parallel"),
        name="inproj",
    )(x, sh, sc, g1, w_ext, gq, wuq, gkv, wukv, cos, sina, sinb)


def _mla_kernel(*refs, tk, nk):
    if nk:
        q_ref, k_ref, v_ref, kc_ref, vc_ref, o_ref, m_ref, l_ref, acc_ref = refs
    else:
        q_ref, kc_ref, vc_ref, o_ref, m_ref, l_ref, acc_ref = refs
    q = q_ref[...]
    tq = q.shape[0]
    W = V7X_LANES

    def update(kblk, vblk, first):
        ncol = kblk.shape[0] // W
        s = _dot_nt(q, kblk)
        smax = s[:, 0:W]
        for c in range(1, ncol):
            smax = jnp.maximum(smax, s[:, W * c:W * (c + 1)])
        m_new = jnp.broadcast_to(jnp.max(smax, axis=1, keepdims=True), (tq, W))
        if not first:
            m_prev = m_ref[...]
            m_new = jnp.maximum(m_prev, m_new)
            alpha = jnp.exp2(m_prev - m_new)
        lsum = None
        ps = []
        for c in range(ncol):
            pc = jnp.exp2(s[:, W * c:W * (c + 1)] - m_new)
            lsum = pc if lsum is None else lsum + pc
            ps.append(pc.astype(BF16))
        pv = _dot(jnp.concatenate(ps, axis=1), vblk)
        if first:
            l_ref[...] = lsum
            acc_ref[...] = pv
        else:
            l_ref[...] = alpha * l_ref[...] + lsum
            acc_ref[...] = alpha * acc_ref[...] + pv
        m_ref[...] = m_new

    update(kc_ref[...], vc_ref[...], True)
    if nk:
        def body(j, carry):
            off = pl.multiple_of(j * tk, tk)
            update(k_ref[pl.ds(off, tk), :], v_ref[pl.ds(off, tk), :], False)
            return carry
        lax.fori_loop(0, nk, body, 0)
    l = jnp.sum(l_ref[...], axis=1, keepdims=True)
    o_ref[...] = (acc_ref[...] / l).astype(o_ref.dtype)


def _mla(q, kc, vc, k=None, v=None, *, B, tq, tk):
    M = q.shape[0]
    T = M // B
    Lc = kc.shape[0] // B
    nq = T // tq
    H = MLA_HEADS
    qspec = pl.BlockSpec((tq, 256), lambda b, h, i: (b * nq + i, h))
    cspecs = [pl.BlockSpec((Lc, 256), lambda b, h, i: (b, h)),
              pl.BlockSpec((Lc, MLA_V), lambda b, h, i: (b, h))]
    if k is None:
        nk, in_specs, args = 0, [qspec] + cspecs, (q, kc, vc)
    else:
        nk = T // tk
        in_specs = [qspec, pl.BlockSpec((T, 256), lambda b, h, i: (b, h)),
                    pl.BlockSpec((T, MLA_V), lambda b, h, i: (b, h))] + cspecs
        args = (q, k, v, kc, vc)
    return pl.pallas_call(
        functools.partial(_mla_kernel, tk=tk, nk=nk),
        grid=(B, H, nq),
        in_specs=in_specs,
        out_specs=pl.BlockSpec((tq, MLA_V), lambda b, h, i: (b * nq + i, h)),
        out_shape=jax.ShapeDtypeStruct((M, MLA_VW), BF16),
        scratch_shapes=[pltpu.VMEM((tq, V7X_LANES), F32), pltpu.VMEM((tq, V7X_LANES), F32),
                        pltpu.VMEM((tq, MLA_V), F32)],
        compiler_params=_cparams("parallel", "parallel", "arbitrary"),
        name="mla_latent" if nk else "mla_ctx",
    )(*args)


SWA_SUB = 128
SWA_SPAN = SWA_SUB + 2 * WINDOW


def _swa_kernel(*refs, local, tq, S):
    if local:
        sink_ref, q_ref, k_ref, v_ref, kc_ref, vc_ref, o_ref = refs
    else:
        sink_ref, q_ref, kc_ref, vc_ref, o_ref = refs
    W = V7X_LANES
    d = SWA_HEAD_DIM
    ppg = SWA_HEADS // SWA_KV_HEADS // 2
    Lc = kc_ref.shape[0]

    def variants(x2):
        low = lax.broadcasted_iota(jnp.int32, (x2.shape[0], W), 1) < d
        x, xs = x2[:, :W], x2[:, W:]
        zero = jnp.zeros_like(x)
        return {(0, 0): jnp.where(low, x, zero), (1, 1): jnp.where(low, zero, x),
                (0, 1): jnp.where(low, zero, xs), (1, 0): jnp.where(low, xs, zero)}

    def with_ones(vv):
        return {ge: jnp.concatenate([t, jnp.ones_like(t)], axis=1) for ge, t in vv.items()}

    kcv = variants(kc_ref[...])
    vcv = with_ones(variants(vc_ref[...]))
    for j in range(tq // SWA_SUB):
        r = slice(SWA_SUB * j, SWA_SUB * (j + 1))
        if local:
            qpos = pl.program_id(1) * tq + SWA_SUB * j
            r0 = pl.multiple_of(jnp.clip(qpos - WINDOW, 0, S - SWA_SPAN), W)
            kv = variants(k_ref[pl.ds(r0, SWA_SPAN), :])
            vv = with_ones(variants(v_ref[pl.ds(r0, SWA_SPAN), :]))
            rel = (lax.broadcasted_iota(jnp.int32, (SWA_SUB, SWA_SPAN), 1)
                   - lax.broadcasted_iota(jnp.int32, (SWA_SUB, SWA_SPAN), 0)) + (r0 - qpos)
            bias = jnp.where(jnp.abs(rel) <= WINDOW, 0.0, NEG_BIG)
            bias = jnp.concatenate([bias] * ppg, axis=0)
        for g in range(SWA_KV_HEADS):
            cols = [slice(W * (ppg * g + pp), W * (ppg * g + pp + 1)) for pp in range(ppg)]
            qs = jnp.concatenate([q_ref[r, c] for c in cols], axis=0)
            pair_out = None
            for e in range(2):
                sink = jnp.concatenate(
                    [jnp.full((SWA_SUB, W), sink_ref[2 * (ppg * g + pp) + e] * LOG2E, F32)
                     for pp in range(ppg)], axis=0)
                s_ctx = _dot_nt(qs, kcv[g, e])
                blocks = [s_ctx[:, W * c:W * (c + 1)] for c in range(Lc // W)]
                if local:
                    s_loc = _dot_nt(qs, kv[g, e]) + bias
                    blocks += [s_loc[:, W * c:W * (c + 1)] for c in range(SWA_SPAN // W)]
                mx = blocks[0]
                for blk in blocks[1:]:
                    mx = jnp.maximum(mx, blk)
                m = jnp.maximum(sink, jnp.broadcast_to(jnp.max(mx, axis=1, keepdims=True), mx.shape))
                ps = [jnp.exp2(blk - m).astype(BF16) for blk in blocks]
                out = _dot(jnp.concatenate(ps[:Lc // W], axis=1), vcv[g, e])
                if local:
                    out = out + _dot(jnp.concatenate(ps[Lc // W:], axis=1), vv[g, e])
                o = out[:, :W] / (out[:, W:] + jnp.exp2(sink - m))
                pair_out = o if pair_out is None else pair_out + o
            for pp, c in enumerate(cols):
                o_ref[r, c] = pair_out[SWA_SUB * pp:SWA_SUB * (pp + 1)].astype(o_ref.dtype)


def _swa(sink, q, kc, vc, k=None, v=None, *, B, tq):
    M = q.shape[0]
    T = M // B
    Lc = kc.shape[0] // B
    local = k is not None
    if not local:
        tq = T
    nq = T // tq
    qspec = pl.BlockSpec((tq, SWA_QW), lambda b, i: (b * nq + i, 0))
    cspec = pl.BlockSpec((Lc, 2 * SWA_KW), lambda b, i: (b, 0))
    in_specs = [pl.BlockSpec(memory_space=pltpu.SMEM), qspec]
    args = [sink, q]
    if local:
        in_specs += [pl.BlockSpec((T, 2 * SWA_KW), lambda b, i: (b, 0))] * 2
        args += [k, v]
    in_specs += [cspec, cspec]
    args += [kc, vc]
    return pl.pallas_call(
        functools.partial(_swa_kernel, local=local, tq=tq, S=T),
        grid=(B, nq),
        in_specs=in_specs,
        out_specs=qspec,
        out_shape=jax.ShapeDtypeStruct((M, SWA_QW), BF16),
        compiler_params=_cparams("parallel", "parallel"),
        name="swa_latent" if local else "swa_ctx",
    )(*args)


def _log_sigmoid(x):
    return jnp.minimum(x, 0.0) - jnp.log1p(jnp.exp(-jnp.abs(x)))


def _split3(x):
    hi = x.astype(BF16)
    r1 = x - hi.astype(F32)
    mid = r1.astype(BF16)
    return hi, mid, (r1 - mid.astype(F32)).astype(BF16)


def _mlstm_chunk(q_ref, k_ref, v_ref, g_ref, bias_ref, h_ref, c_scr, m_scr, *, rev):
    L = q_ref.shape[0]
    W = V7X_LANES
    H = MLSTM_HEADS
    dk, dv = MLSTM_QK, MLSTM_V
    pre = g_ref[...] + bias_ref[...]
    row = lax.broadcasted_iota(jnp.int32, (L, L), 0)
    col = lax.broadcasted_iota(jnp.int32, (L, L), 1)
    allowed = (col >= row) if rev else (col <= row)
    tri = jnp.where(allowed, 1.0, 0.0).astype(BF16)
    mask_bias = jnp.where(allowed, 0.0, -jnp.inf)
    acc3 = _dot(tri, jnp.concatenate(_split3(_log_sigmoid(pre)), axis=1))
    bcum = acc3[:, :W] + acc3[:, W:2 * W] + acc3[:, 2 * W:]
    lane0 = GATE_LANE0 + (2 * H if rev else 0)
    z = pre - pltpu.roll(bcum, W - H, 1)
    onehot = jnp.where(lax.broadcasted_iota(jnp.int32, (8, W), 1)
                       == lax.broadcasted_iota(jnp.int32, (8, W), 0) + lane0, 1.0, 0.0).astype(BF16)
    zr = _dot_nt(onehot, jnp.concatenate(_split3(z), axis=0))
    y_rows = zr[:, :L] + zr[:, L:2 * L] + zr[:, 2 * L:]
    rep = lambda t, n: jnp.concatenate([t] * n, axis=1)
    stack = lambda f: jnp.concatenate([f(hh) for hh in range(H)], axis=0)
    head = lambda t, hh: t[L * hh:L * (hh + 1)]
    ones_blk = jnp.ones((L, W), BF16)
    last = 0 if rev else L - 1

    b_rep = stack(lambda hh: jnp.broadcast_to(bcum[:, lane0 + H + hh:lane0 + H + hh + 1], (L, W)))
    i_rep = stack(lambda hh: jnp.broadcast_to(pre[:, lane0 + hh:lane0 + hh + 1], (L, W)))
    m_in = stack(lambda hh: jnp.broadcast_to(m_scr[hh, 0:1, :], (L, W)))
    g_tot = stack(lambda hh: jnp.broadcast_to(head(b_rep, hh)[last:last + 1, :], (L, W)))
    y = stack(lambda hh: y_rows[hh:hh + 1, :] + mask_bias)
    c = -jnp.maximum(m_in, jnp.broadcast_to(jnp.max(y, axis=1, keepdims=True), (H * L, W)))
    qs = [q_ref[:, dk * hh:dk * (hh + 1)] for hh in range(H)]
    ks = [k_ref[:, dk * hh:dk * (hh + 1)] for hh in range(H)]
    v_aug = [jnp.concatenate([v_ref[:, dv * hh:dv * (hh + 1)], ones_blk], axis=1) for hh in range(H)]
    s = (stack(lambda hh: _dot_nt(qs[hh], ks[hh])) * jnp.exp(y + rep(c, L // W))).astype(BF16)
    c_in = [c_scr[hh] for hh in range(H)]
    tot = (stack(lambda hh: _dot(head(s, hh), v_aug[hh]))
           + rep(jnp.exp(m_in + c), 2) * stack(lambda hh: _dot(qs[hh], c_in[hh].astype(BF16))))
    hout = tot[:, :dv] / jnp.maximum(jnp.abs(tot[:, dv:]), jnp.exp(c - b_rep))
    for hh in range(H):
        h_ref[:, dv * hh:dv * (hh + 1)] = head(hout, hh)

    a = g_tot - b_rep + i_rep
    a_max = stack(lambda hh: jnp.broadcast_to(jnp.max(head(a, hh), axis=0, keepdims=True), (L, W)))
    w = jnp.exp(a - a_max)
    m_new = jnp.maximum(g_tot + m_in, a_max)
    decay = jnp.exp(g_tot + m_in - m_new)
    grow = jnp.exp(a_max - m_new)
    for hh in range(H):
        kw = (ks[hh].astype(F32) * head(w, hh)[:, :dk]).astype(BF16)
        d_c = lax.dot_general(kw, v_aug[hh], (((0,), (0,)), ((), ())),
                              preferred_element_type=F32)
        c_scr[hh] = (rep(head(decay, hh)[:dk], 2) * c_in[hh] + rep(head(grow, hh)[:dk], 2) * d_c)
        m_scr[hh] = head(m_new, hh)[:m_scr.shape[1]]


def _mlstm_kernel(qf_ref, kf_ref, vf_ref, gf_ref, qb_ref, kb_ref, vb_ref, gb_ref, bias_ref, c0_ref, m0_ref,
                  hf_ref, hb_ref, cf_ref, mf_ref, c_scr, m_scr, *, nchunks):
    n = pl.program_id(1)

    @pl.when(n == 0)
    def _():
        c_scr[...] = c0_ref[...]
        m_scr[...] = m0_ref[...]

    _mlstm_chunk(qf_ref, kf_ref, vf_ref, gf_ref, bias_ref, hf_ref, c_scr.at[0], m_scr.at[0], rev=False)
    _mlstm_chunk(qb_ref, kb_ref, vb_ref, gb_ref, bias_ref, hb_ref, c_scr.at[1], m_scr.at[1], rev=True)

    @pl.when(n == nchunks - 1)
    def _():
        cf_ref[...] = c_scr[...]
        mf_ref[...] = m_scr[...]


def _mlstm(q, k, v, g, bias, c0, m0, *, B, L):
    M = q.shape[0]
    N = M // B // L
    H = MLSTM_HEADS
    fwd = lambda b, n: (b * N + n, 0)
    bwd = lambda b, n: (b * N + N - 1 - n, 0)
    data = lambda im: [pl.BlockSpec((L, MQW), im), pl.BlockSpec((L, MQW), im),
                       pl.BlockSpec((L, MVW), im), pl.BlockSpec((L, V7X_LANES), im)]
    cspec = pl.BlockSpec((None, 2, H, MLSTM_QK, 256), lambda b, n: (b, 0, 0, 0, 0))
    mspec = pl.BlockSpec((None, 2, H, 8, V7X_LANES), lambda b, n: (b, 0, 0, 0, 0))
    return pl.pallas_call(
        functools.partial(_mlstm_kernel, nchunks=N),
        grid=(B, N),
        in_specs=data(fwd) + data(bwd) + [_resident(bias.shape), cspec, mspec],
        out_specs=[pl.BlockSpec((L, MVW), fwd), pl.BlockSpec((L, MVW), bwd), cspec, mspec],
        out_shape=[jax.ShapeDtypeStruct((M, MVW), F32), jax.ShapeDtypeStruct((M, MVW), F32),
                   jax.ShapeDtypeStruct(c0.shape, F32), jax.ShapeDtypeStruct(m0.shape, F32)],
        scratch_shapes=[pltpu.VMEM((2, H, MLSTM_QK, 256), F32), pltpu.VMEM((2, H, 8, V7X_LANES), F32)],
        compiler_params=_cparams("parallel", "arbitrary"),
        name="mlstm",
    )(q, k, v, g, q, k, v, g, bias, c0, m0)


def _outproj_kernel(ya_ref, ys_ref, hf_ref, hb_ref, mo_ref, gh_ref, w_ref, x_ref, gt_ref,
                    sh_ref, sc_ref, g2_ref, o_ref, h2_ref, *, tn):
    dv = MLSTM_V
    parts = []
    for hh in range(MLSTM_HEADS):
        sl = slice(dv * hh, dv * (hh + 1))
        hn = _rms(hf_ref[:, sl] + hb_ref[:, sl]) * gh_ref[:, sl]
        parts.append((hn * jax.nn.sigmoid(mo_ref[:, sl].astype(F32))).astype(BF16))
    ym = jnp.concatenate(parts, axis=1)
    ya = ya_ref[...]
    ys = ys_ref[...]
    r1 = MLA_VW
    r2 = r1 + SWA_QW
    D = o_ref.shape[1]
    ssq = None
    for c in range(D // tn):
        cs = slice(tn * c, tn * (c + 1))
        acc = _dot(ya, w_ref[0:r1, cs]) + _dot(ys, w_ref[r1:r2, cs]) + _dot(ym, w_ref[r2:, cs])
        xn = x_ref[:, cs] + gt_ref[:, cs] * acc
        o_ref[:, cs] = xn
        part = jnp.sum(xn * xn, axis=-1, keepdims=True)
        ssq = part if ssq is None else ssq + part
    inv = lax.rsqrt(ssq * (1.0 / D) + NORM_EPS)
    for c in range(D // tn):
        cs = slice(tn * c, tn * (c + 1))
        h2 = (o_ref[:, cs] * inv * g2_ref[:, cs]) * (1.0 + sc_ref[:, cs]) + sh_ref[:, cs]
        h2_ref[:, cs] = h2.astype(BF16)


def _outproj(ya, ys, hf, hb, mo, gh, w_out, x, gt, sh, sc, g2, *, l, tm):
    M, D = x.shape
    R = M // gt.shape[0]
    row = lambda i: (i, 0)
    modv = pl.BlockSpec((None, 1, D), lambda i: (i // (R // tm), 0, 0))
    return pl.pallas_call(
        functools.partial(_outproj_kernel, tn=min(512, D)),
        grid=(M // tm,),
        in_specs=[pl.BlockSpec((tm, MLA_VW), row), pl.BlockSpec((tm, SWA_QW), row),
                  pl.BlockSpec((tm, MVW), row), pl.BlockSpec((tm, MVW), row),
                  pl.BlockSpec((tm, MVW), row), _resident(gh.shape), _layer_resident(w_out, l),
                  pl.BlockSpec((tm, D), row), modv, modv, modv, _resident((1, D))],
        out_specs=[pl.BlockSpec((tm, D), row), pl.BlockSpec((tm, D), row)],
        out_shape=[jax.ShapeDtypeStruct((M, D), F32), jax.ShapeDtypeStruct((M, D), BF16)],
        compiler_params=_cparams("parallel"),
        name="outproj",
    )(ya, ys, hf, hb, mo, gh, w_out, x, gt, sh, sc, g2)


def _ffn_kernel(x_ref, h_ref, gt_ref, gfin_ref, w1_ref, w2_ref, o_ref, *, nf, final):
    f = pl.program_id(1)
    u = jnp.maximum(_dot(h_ref[...], w1_ref[...]), 0.0)
    a = (u * u).astype(BF16)

    @pl.when(f == 0)
    def _():
        o_ref[...] = _dot(a, w2_ref[...])

    @pl.when(f != 0)
    def _():
        o_ref[...] += _dot(a, w2_ref[...])

    @pl.when(f == nf - 1)
    def _():
        D = o_ref.shape[1]
        tn = min(512, D)
        ssq = None
        for c in range(D // tn):
            cs = slice(tn * c, tn * (c + 1))
            y = x_ref[:, cs] + gt_ref[:, cs] * o_ref[:, cs]
            o_ref[:, cs] = y
            if final:
                part = jnp.sum(y * y, axis=-1, keepdims=True)
                ssq = part if ssq is None else ssq + part
        if final:
            inv = lax.rsqrt(ssq * (1.0 / D) + NORM_EPS)
            for c in range(D // tn):
                cs = slice(tn * c, tn * (c + 1))
                o_ref[:, cs] = o_ref[:, cs] * inv * gfin_ref[:, cs]


def _ffn(x, h2, gt, gfin, w1, w2, *, l, tm, tf, final):
    M, D = x.shape
    FF = w1.shape[2]
    R = M // gt.shape[0]
    nf = FF // tf
    rows = lambda i, f: (i, 0)
    return pl.pallas_call(
        functools.partial(_ffn_kernel, nf=nf, final=final),
        grid=(M // tm, nf),
        in_specs=[pl.BlockSpec((tm, D), rows), pl.BlockSpec((tm, D), rows),
                  pl.BlockSpec((None, 1, D), lambda i, f: (i // (R // tm), 0, 0)), _resident((1, D)),
                  pl.BlockSpec((None, D, tf), lambda i, f: (l, 0, f)),
                  pl.BlockSpec((None, tf, D), lambda i, f: (l, f, 0))],
        out_specs=pl.BlockSpec((tm, D), rows),
        out_shape=jax.ShapeDtypeStruct((M, D), F32),
        compiler_params=_cparams("parallel", "arbitrary"),
        name="ffn",
    )(x, h2, gt, gfin, w1, w2)


def _rope_tables(S):
    half = SWA_HEAD_DIM // 2
    pos = jnp.arange(S)
    inv = ROPE_BASE ** (-jnp.arange(0, half, 2, dtype=F32) / half)
    ar = (pos // GRID_W)[:, None].astype(F32) * inv
    ac = (pos % GRID_W)[:, None].astype(F32) * inv
    ang = jnp.concatenate([ar, ar, ac, ac] * 2, axis=-1)
    first = (jnp.arange(V7X_LANES) % 32) < 16
    sin = jnp.sin(ang)
    return jnp.cos(ang), jnp.where(first, -sin, 0.0), jnp.where(first, 0.0, sin)


def _permute_w_in(w_in):
    o = [0]
    for w in (MLA_Q_RANK, MLA_KV_RANK, MLA_ROPE, SWA_QW, SWA_KW, SWA_KW, MQW, MQW, MVW, N_GATES, MVW):
        o.append(o[-1] + w)
    seg = lambda i: w_in[..., o[i]:o[i + 1]]
    pad = jnp.zeros(w_in.shape[:-1] + (V7X_LANES - MLA_ROPE - N_GATES,), w_in.dtype)
    cols = [seg(0), seg(1), seg(3), seg(4), seg(5), seg(6), seg(7), seg(8), seg(10), seg(2), seg(9), pad]
    return jnp.concatenate(cols, axis=-1).astype(BF16)


def _permute_w_uq(w_uq):
    L, Rk, _ = w_uq.shape
    w = w_uq.reshape(L, Rk, MLA_HEADS, MLA_NOPE + MLA_ROPE)
    w = jnp.pad(w, ((0, 0), (0, 0), (0, 0), (0, 256 - MLA_NOPE - MLA_ROPE)))
    return w.reshape(L, Rk, MLA_QW).astype(BF16)


def _permute_w_ukv(w_ukv):
    L, Rk, _ = w_ukv.shape
    w = w_ukv.reshape(L, Rk, MLA_HEADS, MLA_NOPE + MLA_V)
    k = w[..., :MLA_NOPE].reshape(L, Rk, MLA_HEADS * MLA_NOPE)
    v = w[..., MLA_NOPE:].reshape(L, Rk, MLA_VW)
    return jnp.concatenate([k, v], axis=-1).astype(BF16)


def _row_tile(rows, want):
    return want if rows % want == 0 else rows


def kernel(x, c, ctx, c_ctx, w_mod, b_mod, g_norm1, g_norm2, w_in, mla_g_q, mla_w_uq, mla_g_kv,
           mla_w_ukv, swa_sink, mlstm_gate_bias, mlstm_g_h, w_out, w_ff1, w_ff2, g_final):
    B, S, D = x.shape
    Lc = ctx.shape[1]
    depth = w_in.shape[0]
    H = MLSTM_HEADS

    rows = -(-(B + 1) // 8) * 8
    c_all = jnp.concatenate([c, c_ctx[None, :], jnp.zeros((rows - B - 1, D), F32)], axis=0)
    mod = _mod_all(c_all, w_mod, b_mod)
    mod6 = mod.reshape(depth, rows, 6, D)

    w_in_p = _permute_w_in(w_in)
    w_uq_p = _permute_w_uq(mla_w_uq)
    w_ukv_p = _permute_w_ukv(mla_w_ukv)
    w_out_b = w_out.astype(BF16)
    w1_b = w_ff1.astype(BF16)
    w2_b = w_ff2.astype(BF16)
    cos, sina, sinb = _rope_tables(S)
    ones_t = jnp.ones((Lc, V7X_LANES), F32)
    zeros_t = jnp.zeros((Lc, V7X_LANES), F32)
    bias_lanes = jnp.pad(mlstm_gate_bias.reshape(depth, 1, N_GATES),
                         ((0, 0), (0, 0), (GATE_LANE0, V7X_LANES - GATE_LANE0 - N_GATES)))
    c_zero = jnp.zeros((B, 2, H, MLSTM_QK, 256), F32)
    m_zero = jnp.zeros((B, 2, H, 8, V7X_LANES), F32)

    tm = _row_tile(S, 512)
    FF = w_ff1.shape[2]
    tm_ffn, tf = _row_tile(S, 512), min(2048, FF)
    tq_mla = tk_mla = _row_tile(S, 2048)
    gfin = g_final[None, :]
    tq_swa = _row_tile(S, 512)
    chunk = _row_tile(S, 256)

    xs = x.reshape(B * S, D)
    xc = ctx.reshape(B * Lc, D)
    for l in range(depth):
        need_ctx = l < depth - 1
        vec = lambda j: mod6[l, :B, j][:, None, :]
        vecc = lambda j: mod6[l, B:B + 1, j][:, None, :]
        g1 = g_norm1[l][None, :]
        g2 = g_norm2[l][None, :]
        gq = mla_g_q[l][None, :]
        gkv = mla_g_kv[l][None, :]
        gh = mlstm_g_h[l].reshape(1, MVW)
        inproj = functools.partial(_inproj, g1=g1, w_ext=w_in_p, gq=gq, wuq=w_uq_p, gkv=gkv,
                                   wukv=w_ukv_p, l=l)
        ffn = functools.partial(_ffn, gfin=gfin, w1=w1_b, w2=w2_b, l=l, tf=tf)
        (qmc, kmc, vmc, qsc, ksc, vsc, mqc, mkc, mvc, moc, gsc) = inproj(
            xc, vecc(0), vecc(1), cos=ones_t, sina=zeros_t, sinb=zeros_t, tm=Lc)
        hfc, hbc, c_st, m_st = _mlstm(mqc, mkc, mvc, gsc, bias_lanes[l], c_zero, m_zero, B=B, L=Lc)

        (qm, km, vm, qs, ks, vs, mq, mk, mv, mo, gs) = inproj(
            xs, vec(0), vec(1), cos=cos, sina=sina, sinb=sinb, tm=tm)
        y_mla = _mla(qm, kmc, vmc, km, vm, B=B, tq=tq_mla, tk=tk_mla)
        y_swa = _swa(swa_sink[l], qs, ksc, vsc, ks, vs, B=B, tq=tq_swa)
        hf, hb, _, _ = _mlstm(mq, mk, mv, gs, bias_lanes[l], c_st, m_st, B=B, L=chunk)
        xs, h2 = _outproj(y_mla, y_swa, hf, hb, mo, gh, w_out_b, xs, vec(2), vec(3), vec(4), g2, l=l, tm=tm)
        xs = ffn(xs, h2, vec(5), tm=tm_ffn, tf=tf if need_ctx else tf // 2, final=not need_ctx)
        if need_ctx:
            yc_mla = _mla(qmc, kmc, vmc, B=B, tq=Lc, tk=Lc)
            yc_swa = _swa(swa_sink[l], qsc, ksc, vsc, B=B, tq=Lc)
            xc, h2c = _outproj(yc_mla, yc_swa, hfc, hbc, moc, gh, w_out_b, xc, vecc(2), vecc(3), vecc(4), g2,
                               l=l, tm=Lc)
            xc = ffn(xc, h2c, vecc(5), tm=B * Lc, tf=min(512, FF), final=False)
    return xs.reshape(B, S, D)
```

```python
import functools
import math

import jax
import jax.numpy as jnp
from jax import lax
from jax.experimental import pallas as pl
from jax.experimental.pallas import tpu as pltpu

F32 = jnp.float32
BF16 = jnp.bfloat16

GRID_W = 64
ROPE_BASE = 10000.0
NORM_EPS = 1e-6
MLA_HEADS = 4
MLA_Q_RANK = 512
MLA_KV_RANK = 256
MLA_NOPE = 128
MLA_ROPE = 64
MLA_V = 128
SWA_HEADS = 16
SWA_KV_HEADS = 2
SWA_HEAD_DIM = 64
WINDOW = 128
MLSTM_HEADS = 4
MLSTM_QK = 64
MLSTM_V = 128
N_GATES = 4 * MLSTM_HEADS

V7X_LANES = 128
V7X_VMEM_LIMIT_BYTES = 62 * 1024 * 1024

LOG2E = math.log2(math.e)
MLA_QSCALE = (MLA_NOPE + MLA_ROPE) ** -0.5 * LOG2E
SWA_QSCALE = SWA_HEAD_DIM ** -0.5 * LOG2E
MLSTM_QSCALE = MLSTM_QK ** -0.5
NEG_BIG = -1e30

C_ZQ = 0
C_ZKV = C_ZQ + MLA_Q_RANK
C_SQ = C_ZKV + MLA_KV_RANK
C_SK = C_SQ + SWA_HEADS * SWA_HEAD_DIM
C_SV = C_SK + SWA_KV_HEADS * SWA_HEAD_DIM
C_MQ = C_SV + SWA_KV_HEADS * SWA_HEAD_DIM
C_MK = C_MQ + MLSTM_HEADS * MLSTM_QK
C_MV = C_MK + MLSTM_HEADS * MLSTM_QK
C_MO = C_MV + MLSTM_HEADS * MLSTM_V
C_SMALL = C_MO + MLSTM_HEADS * MLSTM_V
C_END = C_SMALL + V7X_LANES
GATE_LANE0 = MLA_ROPE

MLA_QW = MLA_HEADS * 256
MLA_VW = MLA_HEADS * MLA_V
SWA_QW = SWA_HEADS * SWA_HEAD_DIM
SWA_KW = SWA_KV_HEADS * SWA_HEAD_DIM
MQW = MLSTM_HEADS * MLSTM_QK
MVW = MLSTM_HEADS * MLSTM_V


def _cparams(*sem):
    return pltpu.CompilerParams(dimension_semantics=sem, vmem_limit_bytes=V7X_VMEM_LIMIT_BYTES)


def _resident(shape):
    nd = len(shape)
    return pl.BlockSpec(shape, lambda *_: (0,) * nd, pipeline_mode=pl.Buffered(1))


def _layer_resident(stacked, l):
    nd = stacked.ndim - 1
    return pl.BlockSpec((None,) + stacked.shape[1:], lambda *_: (l,) + (0,) * nd,
                        pipeline_mode=pl.Buffered(1))


def _dot(a, b):
    return jnp.dot(a, b, preferred_element_type=F32)


def _dot_nt(a, b):
    return lax.dot_general(a, b, (((1,), (1,)), ((), ())), preferred_element_type=F32)


def _rms(x):
    return x * lax.rsqrt(jnp.mean(x * x, axis=-1, keepdims=True) + NORM_EPS)


def _mod_kernel(c_ref, w_ref, b_ref, o_ref):
    c = c_ref[...]
    a = c * jax.nn.sigmoid(c)
    w = w_ref[...]
    a_hi, w_hi = a.astype(BF16), w.astype(BF16)
    a_lo = (a - a_hi.astype(F32)).astype(BF16)
    w_lo = (w - w_hi.astype(F32)).astype(BF16)
    o_ref[...] = _dot(a_hi, w_hi) + _dot(a_lo, w_hi) + _dot(a_hi, w_lo) + b_ref[...]


def _mod_all(c_all, w_mod, b_mod):
    L, D, N = w_mod.shape
    R = c_all.shape[0]
    tn = next(t for t in (2048, 1024, N) if N % t == 0)
    return pl.pallas_call(
        _mod_kernel,
        grid=(L, N // tn),
        in_specs=[pl.BlockSpec((R, D), lambda l, j: (0, 0)),
                  pl.BlockSpec((None, D, tn), lambda l, j: (l, 0, j)),
                  pl.BlockSpec((None, 1, tn), lambda l, j: (l, 0, j))],
        out_specs=pl.BlockSpec((None, R, tn), lambda l, j: (l, 0, j)),
        out_shape=jax.ShapeDtypeStruct((L, R, N), F32),
        compiler_params=_cparams("parallel", "parallel"),
        name="mod",
    )(c_all, w_mod, b_mod.reshape(L, 1, N))


def _rope(x, cos, sina, sinb):
    return x * cos + pltpu.roll(x, V7X_LANES - 16, 1) * sina + pltpu.roll(x, 16, 1) * sinb


def _inproj_kernel(x_ref, sh_ref, sc_ref, g1_ref, w_ref, gq_ref, wuq_ref, gkv_ref, wukv_ref,
                   cos_ref, sina_ref, sinb_ref,
                   qm_ref, km_ref, vm_ref, qs_ref, ks_ref, vs_ref, mq_ref, mk_ref, mv_ref, mo_ref,
                   gs_ref):
    tm = x_ref.shape[0]
    h = (_rms(x_ref[...]) * g1_ref[...]) * (1.0 + sc_ref[...]) + sh_ref[...]
    hb = h.astype(BF16)
    cos, sina, sinb = cos_ref[...], sina_ref[...], sinb_ref[...]
    rope = lambda t: _rope(t, cos, sina, sinb)
    proj = lambda c0, c1: _dot(hb, w_ref[:, c0:c1])
    low = lax.broadcasted_iota(jnp.int32, (tm, V7X_LANES), 1) < MLA_ROPE

    small = proj(C_SMALL, C_END)
    gs_ref[...] = small
    k_rope = jnp.where(low, rope(small), 0.0).astype(BF16)

    zqn = (_rms(proj(C_ZQ, C_ZKV)) * gq_ref[...]).astype(BF16)
    for hh in range(MLA_HEADS):
        qa = _dot(zqn, wuq_ref[:, 256 * hh:256 * (hh + 1)]) * MLA_QSCALE
        qm_ref[:, 256 * hh:256 * hh + 128] = qa[:, :128].astype(BF16)
        qm_ref[:, 256 * hh + 128:256 * (hh + 1)] = jnp.where(low, rope(qa[:, 128:]), 0.0).astype(BF16)

    zkvn = (_rms(proj(C_ZKV, C_SQ)) * gkv_ref[...]).astype(BF16)
    kv = _dot(zkvn, wukv_ref[...])
    for hh in range(MLA_HEADS):
        km_ref[:, 256 * hh:256 * hh + 128] = kv[:, 128 * hh:128 * (hh + 1)].astype(BF16)
        km_ref[:, 256 * hh + 128:256 * (hh + 1)] = k_rope
    vm_ref[...] = kv[:, MLA_HEADS * MLA_NOPE:].astype(BF16)

    for c in range(SWA_QW // 512):
        sq = proj(C_SQ + 512 * c, C_SQ + 512 * (c + 1))
        for p in range(4):
            blk = rope(sq[:, 128 * p:128 * (p + 1)]) * SWA_QSCALE
            qs_ref[:, 512 * c + 128 * p:512 * c + 128 * (p + 1)] = blk.astype(BF16)
    for ref, val in ((ks_ref, rope(proj(C_SK, C_SV))), (vs_ref, proj(C_SV, C_MQ))):
        ref[:, :SWA_KW] = val.astype(BF16)
        ref[:, SWA_KW:] = pltpu.roll(val, SWA_HEAD_DIM, 1).astype(BF16)

    mq_ref[...] = (proj(C_MQ, C_MK) * MLSTM_QSCALE).astype(BF16)
    mk_ref[...] = proj(C_MK, C_MV).astype(BF16)
    mv_ref[...] = proj(C_MV, C_MO).astype(BF16)
    mo_ref[...] = proj(C_MO, C_SMALL).astype(BF16)


def _inproj(x, sh, sc, g1, w_ext, gq, wuq, gkv, wukv, cos, sina, sinb, *, l, tm):
    M, D = x.shape
    S = cos.shape[0]
    R = M // sh.shape[0]
    nt = S // tm
    row = lambda i: (i, 0)
    modv = pl.BlockSpec((None, 1, D), lambda i: (i // (R // tm), 0, 0))
    tab = pl.BlockSpec((tm, V7X_LANES), lambda i: (i % nt, 0))
    widths = (MLA_QW, MLA_QW, MLA_VW, SWA_QW, 2 * SWA_KW, 2 * SWA_KW, MQW, MQW, MVW, MVW)
    out_shape = [jax.ShapeDtypeStruct((M, w), BF16) for w in widths]
    out_shape.append(jax.ShapeDtypeStruct((M, V7X_LANES), F32))
    out_specs = [pl.BlockSpec((tm, w), row) for w in widths] + [pl.BlockSpec((tm, V7X_LANES), row)]
    return pl.pallas_call(
        _inproj_kernel,
        grid=(M // tm,),
        in_specs=[pl.BlockSpec((tm, D), row), modv, modv, _resident((1, D)),
                  _layer_resident(w_ext, l), _resident(gq.shape), _layer_resident(wuq, l),
                  _resident(gkv.shape), _layer_resident(wukv, l), tab, tab, tab],
        out_specs=out_specs,
        out_shape=out_shape,
        compiler_params=_cparams("parallel"),
        name="inproj",
    )(x, sh, sc, g1, w_ext, gq, wuq, gkv, wukv, cos, sina, sinb)


def _mla_kernel(*refs, tk, nk):
    if nk:
        q_ref, k_ref, v_ref, kc_ref, vc_ref, o_ref, m_ref, l_ref, acc_ref = refs
    else:
        q_ref, kc_ref, vc_ref, o_ref, m_ref, l_ref, acc_ref = refs
    q = q_ref[...]
    tq = q.shape[0]
    W = V7X_LANES

    def update(kblk, vblk, first):
        ncol = kblk.shape[0] // W
        s = _dot_nt(q, kblk)
        smax = s[:, 0:W]
        for c in range(1, ncol):
            smax = jnp.maximum(smax, s[:, W * c:W * (c + 1)])
        m_new = jnp.broadcast_to(jnp.max(smax, axis=1, keepdims=True), (tq, W))
        if not first:
            m_prev = m_ref[...]
            m_new = jnp.maximum(m_prev, m_new)
            alpha = jnp.exp2(m_prev - m_new)
        lsum = None
        ps = []
        for c in range(ncol):
            pc = jnp.exp2(s[:, W * c:W * (c + 1)] - m_new)
            lsum = pc if lsum is None else lsum + pc
            ps.append(pc.astype(BF16))
        pv = _dot(jnp.concatenate(ps, axis=1), vblk)
        if first:
            l_ref[...] = lsum
            acc_ref[...] = pv
        else:
            l_ref[...] = alpha * l_ref[...] + lsum
            acc_ref[...] = alpha * acc_ref[...] + pv
        m_ref[...] = m_new

    update(kc_ref[...], vc_ref[...], True)
    if nk:
        def body(j, carry):
            off = pl.multiple_of(j * tk, tk)
            update(k_ref[pl.ds(off, tk), :], v_ref[pl.ds(off, tk), :], False)
            return carry
        lax.fori_loop(0, nk, body, 0)
    l = jnp.sum(l_ref[...], axis=1, keepdims=True)
    o_ref[...] = (acc_ref[...] / l).astype(o_ref.dtype)


def _mla(q, kc, vc, k=None, v=None, *, B, tq, tk):
    M = q.shape[0]
    T = M // B
    Lc = kc.shape[0] // B
    nq = T // tq
    H = MLA_HEADS
    qspec = pl.BlockSpec((tq, 256), lambda b, h, i: (b * nq + i, h))
    cspecs = [pl.BlockSpec((Lc, 256), lambda b, h, i: (b, h)),
              pl.BlockSpec((Lc, MLA_V), lambda b, h, i: (b, h))]
    if k is None:
        nk, in_specs, args = 0, [qspec] + cspecs, (q, kc, vc)
    else:
        nk = T // tk
        in_specs = [qspec, pl.BlockSpec((T, 256), lambda b, h, i: (b, h)),
                    pl.BlockSpec((T, MLA_V), lambda b, h, i: (b, h))] + cspecs
        args = (q, k, v, kc, vc)
    return pl.pallas_call(
        functools.partial(_mla_kernel, tk=tk, nk=nk),
        grid=(B, H, nq),
        in_specs=in_specs,
        out_specs=pl.BlockSpec((tq, MLA_V), lambda b, h, i: (b * nq + i, h)),
        out_shape=jax.ShapeDtypeStruct((M, MLA_VW), BF16),
        scratch_shapes=[pltpu.VMEM((tq, V7X_LANES), F32), pltpu.VMEM((tq, V7X_LANES), F32),
                        pltpu.VMEM((tq, MLA_V), F32)],
        compiler_params=_cparams("parallel", "parallel", "arbitrary"),
        name="mla_latent" if nk else "mla_ctx",
    )(*args)


SWA_SUB = 128
SWA_SPAN = SWA_SUB + 2 * WINDOW


def _swa_kernel(*refs, local, tq, S):
    if local:
        sink_ref, q_ref, k_ref, v_ref, kc_ref, vc_ref, o_ref = refs
    else:
        sink_ref, q_ref, kc_ref, vc_ref, o_ref = refs
    W = V7X_LANES
    d = SWA_HEAD_DIM
    ppg = SWA_HEADS // SWA_KV_HEADS // 2
    Lc = kc_ref.shape[0]

    def variants(x2):
        low = lax.broadcasted_iota(jnp.int32, (x2.shape[0], W), 1) < d
        x, xs = x2[:, :W], x2[:, W:]
        zero = jnp.zeros_like(x)
        return {(0, 0): jnp.where(low, x, zero), (1, 1): jnp.where(low, zero, x),
                (0, 1): jnp.where(low, zero, xs), (1, 0): jnp.where(low, xs, zero)}

    def with_ones(vv):
        return {ge: jnp.concatenate([t, jnp.ones_like(t)], axis=1) for ge, t in vv.items()}

    kcv = variants(kc_ref[...])
    vcv = with_ones(variants(vc_ref[...]))
    for j in range(tq // SWA_SUB):
        r = slice(SWA_SUB * j, SWA_SUB * (j + 1))
        if local:
            qpos = pl.program_id(1) * tq + SWA_SUB * j
            r0 = pl.multiple_of(jnp.clip(qpos - WINDOW, 0, S - SWA_SPAN), W)
            kwin = variants(k_ref[pl.ds(r0, SWA_SPAN), :])
            vwin = with_ones(variants(v_ref[pl.ds(r0, SWA_SPAN), :]))
            kv = {ge: jnp.concatenate([kcv[ge], kwin[ge]], axis=0) for ge in kwin}
            vv = {ge: jnp.concatenate([vcv[ge], vwin[ge]], axis=0) for ge in vwin}
            rel = (lax.broadcasted_iota(jnp.int32, (SWA_SUB, SWA_SPAN), 1)
                   - lax.broadcasted_iota(jnp.int32, (SWA_SUB, SWA_SPAN), 0)) + (r0 - qpos)
            bias = jnp.where(jnp.abs(rel) <= WINDOW, 0.0, NEG_BIG)
            bias = jnp.concatenate([bias] * ppg, axis=0)
        for g in range(SWA_KV_HEADS):
            cols = [slice(W * (ppg * g + pp), W * (ppg * g + pp + 1)) for pp in range(ppg)]
            qs = jnp.concatenate([q_ref[r, c] for c in cols], axis=0)
            pair_out = None
            for e in range(2):
                sink = jnp.concatenate(
                    [jnp.full((SWA_SUB, W), sink_ref[2 * (ppg * g + pp) + e] * LOG2E, F32)
                     for pp in range(ppg)], axis=0)
                s = _dot_nt(qs, kv[g, e] if local else kcv[g, e])
                blocks = [s[:, W * c:W * (c + 1)] for c in range(Lc // W)]
                if local:
                    blocks += [s[:, Lc + W * c:Lc + W * (c + 1)] + bias[:, W * c:W * (c + 1)]
                               for c in range(SWA_SPAN // W)]
                mx = blocks[0]
                for blk in blocks[1:]:
                    mx = jnp.maximum(mx, blk)
                m = jnp.maximum(sink, jnp.broadcast_to(jnp.max(mx, axis=1, keepdims=True), mx.shape))
                ps = [jnp.exp2(blk - m).astype(BF16) for blk in blocks]
                out = _dot(jnp.concatenate(ps, axis=1), vv[g, e] if local else vcv[g, e])
                o = out[:, :W] / (out[:, W:] + jnp.exp2(sink - m))
                pair_out = o if pair_out is None else pair_out + o
            for pp, c in enumerate(cols):
                o_ref[r, c] = pair_out[SWA_SUB * pp:SWA_SUB * (pp + 1)].astype(o_ref.dtype)


def _swa(sink, q, kc, vc, k=None, v=None, *, B, tq):
    M = q.shape[0]
    T = M // B
    Lc = kc.shape[0] // B
    local = k is not None
    if not local:
        tq = T
    nq = T // tq
    qspec = pl.BlockSpec((tq, SWA_QW), lambda b, i: (b * nq + i, 0))
    cspec = pl.BlockSpec((Lc, 2 * SWA_KW), lambda b, i: (b, 0))
    in_specs = [pl.BlockSpec(memory_space=pltpu.SMEM), qspec]
    args = [sink, q]
    if local:
        in_specs += [pl.BlockSpec((T, 2 * SWA_KW), lambda b, i: (b, 0))] * 2
        args += [k, v]
    in_specs += [cspec, cspec]
    args += [kc, vc]
    return pl.pallas_call(
        functools.partial(_swa_kernel, local=local, tq=tq, S=T),
        grid=(B, nq),
        in_specs=in_specs,
        out_specs=qspec,
        out_shape=jax.ShapeDtypeStruct((M, SWA_QW), BF16),
        compiler_params=_cparams("parallel", "parallel"),
        name="swa_latent" if local else "swa_ctx",
    )(*args)


def _log_sigmoid(x):
    return jnp.minimum(x, 0.0) - jnp.log1p(jnp.exp(-jnp.abs(x)))


def _split3(x):
    hi = x.astype(BF16)
    r1 = x - hi.astype(F32)
    mid = r1.astype(BF16)
    return hi, mid, (r1 - mid.astype(F32)).astype(BF16)


def _mlstm_chunk(q_ref, k_ref, v_ref, g_ref, bias_ref, h_ref, c_scr, m_scr, *, rev):
    L = q_ref.shape[0]
    W = V7X_LANES
    H = MLSTM_HEADS
    dk, dv = MLSTM_QK, MLSTM_V
    pre = g_ref[...] + bias_ref[...]
    row = lax.broadcasted_iota(jnp.int32, (L, L), 0)
    col = lax.broadcasted_iota(jnp.int32, (L, L), 1)
    allowed = (col >= row) if rev else (col <= row)
    tri = jnp.where(allowed, 1.0, 0.0).astype(BF16)
    mask_bias = jnp.where(allowed, 0.0, -jnp.inf)
    acc3 = _dot(tri, jnp.concatenate(_split3(_log_sigmoid(pre)), axis=1))
    bcum = acc3[:, :W] + acc3[:, W:2 * W] + acc3[:, 2 * W:]
    lane0 = GATE_LANE0 + (2 * H if rev else 0)
    z = pre - pltpu.roll(bcum, W - H, 1)
    onehot = jnp.where(lax.broadcasted_iota(jnp.int32, (8, W), 1)
                       == lax.broadcasted_iota(jnp.int32, (8, W), 0) + lane0, 1.0, 0.0).astype(BF16)
    zr = _dot_nt(onehot, jnp.concatenate(_split3(z), axis=0))
    y_rows = zr[:, :L] + zr[:, L:2 * L] + zr[:, 2 * L:]
    rep = lambda t, n: jnp.concatenate([t] * n, axis=1)
    stack = lambda f: jnp.concatenate([f(hh) for hh in range(H)], axis=0)
    head = lambda t, hh: t[L * hh:L * (hh + 1)]
    ones_blk = jnp.ones((L, W), BF16)
    last = 0 if rev else L - 1

    b_rep = stack(lambda hh: jnp.broadcast_to(bcum[:, lane0 + H + hh:lane0 + H + hh + 1], (L, W)))
    i_rep = stack(lambda hh: jnp.broadcast_to(pre[:, lane0 + hh:lane0 + hh + 1], (L, W)))
    m_in = stack(lambda hh: jnp.broadcast_to(m_scr[hh, 0:1, :], (L, W)))
    g_tot = stack(lambda hh: jnp.broadcast_to(head(b_rep, hh)[last:last + 1, :], (L, W)))
    y = stack(lambda hh: y_rows[hh:hh + 1, :] + mask_bias)
    c = -jnp.maximum(m_in, jnp.broadcast_to(jnp.max(y, axis=1, keepdims=True), (H * L, W)))
    qs = [q_ref[:, dk * hh:dk * (hh + 1)] for hh in range(H)]
    ks = [k_ref[:, dk * hh:dk * (hh + 1)] for hh in range(H)]
    v_aug = [jnp.concatenate([v_ref[:, dv * hh:dv * (hh + 1)], ones_blk], axis=1) for hh in range(H)]
    s = (stack(lambda hh: _dot_nt(qs[hh], ks[hh])) * jnp.exp(y + rep(c, L // W))).astype(BF16)
    c_in = [c_scr[hh] for hh in range(H)]
    tot = (stack(lambda hh: _dot(head(s, hh), v_aug[hh]))
           + rep(jnp.exp(m_in + c), 2) * stack(lambda hh: _dot(qs[hh], c_in[hh].astype(BF16))))
    hout = tot[:, :dv] / jnp.maximum(jnp.abs(tot[:, dv:]), jnp.exp(c - b_rep))
    for hh in range(H):
        h_ref[:, dv * hh:dv * (hh + 1)] = head(hout, hh)

    a = g_tot - b_rep + i_rep
    a_max = stack(lambda hh: jnp.broadcast_to(jnp.max(head(a, hh), axis=0, keepdims=True), (L, W)))
    w = jnp.exp(a - a_max)
    m_new = jnp.maximum(g_tot + m_in, a_max)
    decay = jnp.exp(g_tot + m_in - m_new)
    grow = jnp.exp(a_max - m_new)
    for hh in range(H):
        kw = (ks[hh].astype(F32) * head(w, hh)[:, :dk]).astype(BF16)
        d_c = lax.dot_general(kw, v_aug[hh], (((0,), (0,)), ((), ())),
                              preferred_element_type=F32)
        c_scr[hh] = (rep(head(decay, hh)[:dk], 2) * c_in[hh] + rep(head(grow, hh)[:dk], 2) * d_c)
        m_scr[hh] = head(m_new, hh)[:m_scr.shape[1]]


def _mlstm_kernel(qf_ref, kf_ref, vf_ref, gf_ref, qb_ref, kb_ref, vb_ref, gb_ref, bias_ref, c0_ref, m0_ref,
                  hf_ref, hb_ref, cf_ref, mf_ref, c_scr, m_scr, *, nchunks):
    n = pl.program_id(1)

    @pl.when(n == 0)
    def _():
        c_scr[...] = c0_ref[...]
        m_scr[...] = m0_ref[...]

    _mlstm_chunk(qf_ref, kf_ref, vf_ref, gf_ref, bias_ref, hf_ref, c_scr.at[0], m_scr.at[0], rev=False)
    _mlstm_chunk(qb_ref, kb_ref, vb_ref, gb_ref, bias_ref, hb_ref, c_scr.at[1], m_scr.at[1], rev=True)

    @pl.when(n == nchunks - 1)
    def _():
        cf_ref[...] = c_scr[...]
        mf_ref[...] = m_scr[...]


def _mlstm(q, k, v, g, bias, c0, m0, *, B, L):
    M = q.shape[0]
    N = M // B // L
    H = MLSTM_HEADS
    fwd = lambda b, n: (b * N + n, 0)
    bwd = lambda b, n: (b * N + N - 1 - n, 0)
    data = lambda im: [pl.BlockSpec((L, MQW), im), pl.BlockSpec((L, MQW), im),
                       pl.BlockSpec((L, MVW), im), pl.BlockSpec((L, V7X_LANES), im)]
    cspec = pl.BlockSpec((None, 2, H, MLSTM_QK, 256), lambda b, n: (b, 0, 0, 0, 0))
    mspec = pl.BlockSpec((None, 2, H, 8, V7X_LANES), lambda b, n: (b, 0, 0, 0, 0))
    return pl.pallas_call(
        functools.partial(_mlstm_kernel, nchunks=N),
        grid=(B, N),
        in_specs=data(fwd) + data(bwd) + [_resident(bias.shape), cspec, mspec],
        out_specs=[pl.BlockSpec((L, MVW), fwd), pl.BlockSpec((L, MVW), bwd), cspec, mspec],
        out_shape=[jax.ShapeDtypeStruct((M, MVW), F32), jax.ShapeDtypeStruct((M, MVW), F32),
                   jax.ShapeDtypeStruct(c0.shape, F32), jax.ShapeDtypeStruct(m0.shape, F32)],
        scratch_shapes=[pltpu.VMEM((2, H, MLSTM_QK, 256), F32), pltpu.VMEM((2, H, 8, V7X_LANES), F32)],
        compiler_params=_cparams("parallel", "arbitrary"),
        name="mlstm",
    )(q, k, v, g, q, k, v, g, bias, c0, m0)


def _outproj_kernel(ya_ref, ys_ref, hf_ref, hb_ref, mo_ref, gh_ref, w_ref, x_ref, gt_ref,
                    sh_ref, sc_ref, g2_ref, o_ref, h2_ref, *, tn):
    dv = MLSTM_V
    parts = []
    for hh in range(MLSTM_HEADS):
        sl = slice(dv * hh, dv * (hh + 1))
        hn = _rms(hf_ref[:, sl] + hb_ref[:, sl]) * gh_ref[:, sl]
        parts.append((hn * jax.nn.sigmoid(mo_ref[:, sl].astype(F32))).astype(BF16))
    ym = jnp.concatenate(parts, axis=1)
    ya = ya_ref[...]
    ys = ys_ref[...]
    r1 = MLA_VW
    r2 = r1 + SWA_QW
    D = o_ref.shape[1]
    ssq = None
    for c in range(D // tn):
        cs = slice(tn * c, tn * (c + 1))
        acc = _dot(ya, w_ref[0:r1, cs]) + _dot(ys, w_ref[r1:r2, cs]) + _dot(ym, w_ref[r2:, cs])
        xn = x_ref[:, cs] + gt_ref[:, cs] * acc
        o_ref[:, cs] = xn
        part = jnp.sum(xn * xn, axis=-1, keepdims=True)
        ssq = part if ssq is None else ssq + part
    inv = lax.rsqrt(ssq * (1.0 / D) + NORM_EPS)
    for c in range(D // tn):
        cs = slice(tn * c, tn * (c + 1))
        h2 = (o_ref[:, cs] * inv * g2_ref[:, cs]) * (1.0 + sc_ref[:, cs]) + sh_ref[:, cs]
        h2_ref[:, cs] = h2.astype(BF16)


def _outproj(ya, ys, hf, hb, mo, gh, w_out, x, gt, sh, sc, g2, *, l, tm):
    M, D = x.shape
    R = M // gt.shape[0]
    row = lambda i: (i, 0)
    modv = pl.BlockSpec((None, 1, D), lambda i: (i // (R // tm), 0, 0))
    return pl.pallas_call(
        functools.partial(_outproj_kernel, tn=min(512, D)),
        grid=(M // tm,),
        in_specs=[pl.BlockSpec((tm, MLA_VW), row), pl.BlockSpec((tm, SWA_QW), row),
                  pl.BlockSpec((tm, MVW), row), pl.BlockSpec((tm, MVW), row),
                  pl.BlockSpec((tm, MVW), row), _resident(gh.shape), _layer_resident(w_out, l),
                  pl.BlockSpec((tm, D), row), modv, modv, modv, _resident((1, D))],
        out_specs=[pl.BlockSpec((tm, D), row), pl.BlockSpec((tm, D), row)],
        out_shape=[jax.ShapeDtypeStruct((M, D), F32), jax.ShapeDtypeStruct((M, D), BF16)],
        compiler_params=_cparams("parallel"),
        name="outproj",
    )(ya, ys, hf, hb, mo, gh, w_out, x, gt, sh, sc, g2)


def _ffn_kernel(x_ref, h_ref, gt_ref, gfin_ref, w1_ref, w2_ref, o_ref, *, nf, final):
    f = pl.program_id(1)
    u = jnp.maximum(_dot(h_ref[...], w1_ref[...]), 0.0)
    a = (u * u).astype(BF16)

    @pl.when(f == 0)
    def _():
        o_ref[...] = _dot(a, w2_ref[...])

    @pl.when(f != 0)
    def _():
        o_ref[...] += _dot(a, w2_ref[...])

    @pl.when(f == nf - 1)
    def _():
        D = o_ref.shape[1]
        tn = min(512, D)
        ssq = None
        for c in range(D // tn):
            cs = slice(tn * c, tn * (c + 1))
            y = x_ref[:, cs] + gt_ref[:, cs] * o_ref[:, cs]
            o_ref[:, cs] = y
            if final:
                part = jnp.sum(y * y, axis=-1, keepdims=True)
                ssq = part if ssq is None else ssq + part
        if final:
            inv = lax.rsqrt(ssq * (1.0 / D) + NORM_EPS)
            for c in range(D // tn):
                cs = slice(tn * c, tn * (c + 1))
                o_ref[:, cs] = o_ref[:, cs] * inv * gfin_ref[:, cs]


def _ffn(x, h2, gt, gfin, w1, w2, *, l, tm, tf, final):
    M, D = x.shape
    FF = w1.shape[2]
    R = M // gt.shape[0]
    nf = FF // tf
    rows = lambda i, f: (i, 0)
    hbuf = pl.Buffered(1) if final else None
    return pl.pallas_call(
        functools.partial(_ffn_kernel, nf=nf, final=final),
        grid=(M // tm, nf),
        in_specs=[pl.BlockSpec((tm, D), rows), pl.BlockSpec((tm, D), rows, pipeline_mode=hbuf),
                  pl.BlockSpec((None, 1, D), lambda i, f: (i // (R // tm), 0, 0)), _resident((1, D)),
                  pl.BlockSpec((None, D, tf), lambda i, f: (l, 0, f)),
                  pl.BlockSpec((None, tf, D), lambda i, f: (l, f, 0))],
        out_specs=pl.BlockSpec((tm, D), rows),
        out_shape=jax.ShapeDtypeStruct((M, D), F32),
        compiler_params=_cparams("parallel", "arbitrary"),
        name="ffn",
    )(x, h2, gt, gfin, w1, w2)


def _rope_tables(S):
    half = SWA_HEAD_DIM // 2
    pos = jnp.arange(S)
    inv = ROPE_BASE ** (-jnp.arange(0, half, 2, dtype=F32) / half)
    ar = (pos // GRID_W)[:, None].astype(F32) * inv
    ac = (pos % GRID_W)[:, None].astype(F32) * inv
    ang = jnp.concatenate([ar, ar, ac, ac] * 2, axis=-1)
    first = (jnp.arange(V7X_LANES) % 32) < 16
    sin = jnp.sin(ang)
    return jnp.cos(ang), jnp.where(first, -sin, 0.0), jnp.where(first, 0.0, sin)


def _permute_w_in(w_in):
    o = [0]
    for w in (MLA_Q_RANK, MLA_KV_RANK, MLA_ROPE, SWA_QW, SWA_KW, SWA_KW, MQW, MQW, MVW, N_GATES, MVW):
        o.append(o[-1] + w)
    w_in = w_in.astype(BF16)
    seg = lambda i: w_in[..., o[i]:o[i + 1]]
    pad = jnp.zeros(w_in.shape[:-1] + (V7X_LANES - MLA_ROPE - N_GATES,), w_in.dtype)
    cols = [seg(0), seg(1), seg(3), seg(4), seg(5), seg(6), seg(7), seg(8), seg(10), seg(2), seg(9), pad]
    return jnp.concatenate(cols, axis=-1)


def _permute_w_uq(w_uq):
    L, Rk, _ = w_uq.shape
    w = w_uq.reshape(L, Rk, MLA_HEADS, MLA_NOPE + MLA_ROPE)
    w = jnp.pad(w, ((0, 0), (0, 0), (0, 0), (0, 256 - MLA_NOPE - MLA_ROPE)))
    return w.reshape(L, Rk, MLA_QW).astype(BF16)


def _permute_w_ukv(w_ukv):
    L, Rk, _ = w_ukv.shape
    w = w_ukv.reshape(L, Rk, MLA_HEADS, MLA_NOPE + MLA_V)
    k = w[..., :MLA_NOPE].reshape(L, Rk, MLA_HEADS * MLA_NOPE)
    v = w[..., MLA_NOPE:].reshape(L, Rk, MLA_VW)
    return jnp.concatenate([k, v], axis=-1).astype(BF16)


def _row_tile(rows, want):
    return want if rows % want == 0 else rows


def kernel(x, c, ctx, c_ctx, w_mod, b_mod, g_norm1, g_norm2, w_in, mla_g_q, mla_w_uq, mla_g_kv,
           mla_w_ukv, swa_sink, mlstm_gate_bias, mlstm_g_h, w_out, w_ff1, w_ff2, g_final):
    B, S, D = x.shape
    Lc = ctx.shape[1]
    depth = w_in.shape[0]
    H = MLSTM_HEADS

    rows = -(-(B + 1) // 8) * 8
    c_all = jnp.concatenate([c, c_ctx[None, :], jnp.zeros((rows - B - 1, D), F32)], axis=0)
    mod = _mod_all(c_all, w_mod, b_mod)
    mod6 = mod.reshape(depth, rows, 6, D)

    w_in_p = _permute_w_in(w_in)
    w_uq_p = _permute_w_uq(mla_w_uq)
    w_ukv_p = _permute_w_ukv(mla_w_ukv)
    w_out_b = w_out.astype(BF16)
    w1_b = w_ff1.astype(BF16)
    w2_b = w_ff2.astype(BF16)
    cos, sina, sinb = _rope_tables(S)
    ones_t = jnp.ones((Lc, V7X_LANES), F32)
    zeros_t = jnp.zeros((Lc, V7X_LANES), F32)
    bias_lanes = jnp.pad(mlstm_gate_bias.reshape(depth, 1, N_GATES),
                         ((0, 0), (0, 0), (GATE_LANE0, V7X_LANES - GATE_LANE0 - N_GATES)))
    c_zero = jnp.zeros((B, 2, H, MLSTM_QK, 256), F32)
    m_zero = jnp.zeros((B, 2, H, 8, V7X_LANES), F32)

    tm = _row_tile(S, 512)
    FF = w_ff1.shape[2]
    tm_ffn, tf = _row_tile(S, 512), min(2048, FF)
    tq_mla = tk_mla = _row_tile(S, 2048)
    gfin = g_final[None, :]
    tq_swa = _row_tile(S, 512)
    chunk = _row_tile(S, 256)

    xs = x.reshape(B * S, D)
    xc = ctx.reshape(B * Lc, D)
    for l in range(depth):
        need_ctx = l < depth - 1
        vec = lambda j: mod6[l, :B, j][:, None, :]
        vecc = lambda j: mod6[l, B:B + 1, j][:, None, :]
        g1 = g_norm1[l][None, :]
        g2 = g_norm2[l][None, :]
        gq = mla_g_q[l][None, :]
        gkv = mla_g_kv[l][None, :]
        gh = mlstm_g_h[l].reshape(1, MVW)
        inproj = functools.partial(_inproj, g1=g1, w_ext=w_in_p, gq=gq, wuq=w_uq_p, gkv=gkv,
                                   wukv=w_ukv_p, l=l)
        ffn = functools.partial(_ffn, gfin=gfin, w1=w1_b, w2=w2_b, l=l, tf=tf)
        (qmc, kmc, vmc, qsc, ksc, vsc, mqc, mkc, mvc, moc, gsc) = inproj(
            xc, vecc(0), vecc(1), cos=ones_t, sina=zeros_t, sinb=zeros_t, tm=Lc)
        hfc, hbc, c_st, m_st = _mlstm(mqc, mkc, mvc, gsc, bias_lanes[l], c_zero, m_zero, B=B, L=Lc)

        (qm, km, vm, qs, ks, vs, mq, mk, mv, mo, gs) = inproj(
            xs, vec(0), vec(1), cos=cos, sina=sina, sinb=sinb, tm=tm)
        y_mla = _mla(qm, kmc, vmc, km, vm, B=B, tq=tq_mla, tk=tk_mla)
        y_swa = _swa(swa_sink[l], qs, ksc, vsc, ks, vs, B=B, tq=tq_swa)
        hf, hb, _, _ = _mlstm(mq, mk, mv, gs, bias_lanes[l], c_st, m_st, B=B, L=chunk)
        xs, h2 = _outproj(y_mla, y_swa, hf, hb, mo, gh, w_out_b, xs, vec(2), vec(3), vec(4), g2, l=l, tm=tm)
        xs = ffn(xs, h2, vec(5), tm=tm_ffn, final=not need_ctx)
        if need_ctx:
            yc_mla = _mla(qmc, kmc, vmc, B=B, tq=Lc, tk=Lc)
            yc_swa = _swa(swa_sink[l], qsc, ksc, vsc, B=B, tq=Lc)
            xc, h2c = _outproj(yc_mla, yc_swa, hfc, hbc, moc, gh, w_out_b, xc, vecc(2), vecc(3), vecc(4), g2,
                               l=l, tm=Lc)
            xc = ffn(xc, h2c, vecc(5), tm=B * Lc, tf=min(512, FF), final=False)
    return xs.reshape(B, S, D)
```

```python
import functools
import math

import jax
import jax.numpy as jnp
from jax import lax
from jax.experimental import pallas as pl
from jax.experimental.pallas import tpu as pltpu

F32 = jnp.float32
BF16 = jnp.bfloat16

GRID_W = 64
ROPE_BASE = 10000.0
NORM_EPS = 1e-6
MLA_HEADS = 4
MLA_Q_RANK = 512
MLA_KV_RANK = 256
MLA_NOPE = 128
MLA_ROPE = 64
MLA_V = 128
SWA_HEADS = 16
SWA_KV_HEADS = 2
SWA_HEAD_DIM = 64
WINDOW = 128
MLSTM_HEADS = 4
MLSTM_QK = 64
MLSTM_V = 128
N_GATES = 4 * MLSTM_HEADS

V7X_LANES = 128
V7X_VMEM_LIMIT_BYTES = 63 * 1024 * 1024

LOG2E = math.log2(math.e)
MLA_QSCALE = (MLA_NOPE + MLA_ROPE) ** -0.5 * LOG2E
SWA_QSCALE = SWA_HEAD_DIM ** -0.5 * LOG2E
MLSTM_QSCALE = MLSTM_QK ** -0.5
NEG_BIG = -1e30

C_ZQ = 0
C_ZKV = C_ZQ + MLA_Q_RANK
C_SQ = C_ZKV + MLA_KV_RANK
C_SK = C_SQ + SWA_HEADS * SWA_HEAD_DIM
C_SV = C_SK + SWA_KV_HEADS * SWA_HEAD_DIM
C_MQ = C_SV + SWA_KV_HEADS * SWA_HEAD_DIM
C_MK = C_MQ + MLSTM_HEADS * MLSTM_QK
C_MV = C_MK + MLSTM_HEADS * MLSTM_QK
C_MO = C_MV + MLSTM_HEADS * MLSTM_V
C_SMALL = C_MO + MLSTM_HEADS * MLSTM_V
C_END = C_SMALL + V7X_LANES
GATE_LANE0 = MLA_ROPE

MLA_QW = MLA_HEADS * 256
MLA_VW = MLA_HEADS * MLA_V
SWA_QW = SWA_HEADS * SWA_HEAD_DIM
SWA_KW = SWA_KV_HEADS * SWA_HEAD_DIM
MQW = MLSTM_HEADS * MLSTM_QK
MVW = MLSTM_HEADS * MLSTM_V


def _cparams(*sem):
    return pltpu.CompilerParams(dimension_semantics=sem, vmem_limit_bytes=V7X_VMEM_LIMIT_BYTES)


def _resident(shape):
    nd = len(shape)
    return pl.BlockSpec(shape, lambda *_: (0,) * nd, pipeline_mode=pl.Buffered(1))


def _layer_resident(stacked, l):
    nd = stacked.ndim - 1
    return pl.BlockSpec((None,) + stacked.shape[1:], lambda *_: (l,) + (0,) * nd,
                        pipeline_mode=pl.Buffered(1))


def _dot(a, b):
    return jnp.dot(a, b, preferred_element_type=F32)


def _dot_nt(a, b):
    return lax.dot_general(a, b, (((1,), (1,)), ((), ())), preferred_element_type=F32)


def _rms(x):
    return x * lax.rsqrt(jnp.mean(x * x, axis=-1, keepdims=True) + NORM_EPS)


def _mod_kernel(c_ref, w_ref, b_ref, o_ref):
    c = c_ref[...]
    a = c * jax.nn.sigmoid(c)
    w = w_ref[...]
    a_hi, w_hi = a.astype(BF16), w.astype(BF16)
    a_lo = (a - a_hi.astype(F32)).astype(BF16)
    w_lo = (w - w_hi.astype(F32)).astype(BF16)
    o_ref[...] = _dot(a_hi, w_hi) + _dot(a_lo, w_hi) + _dot(a_hi, w_lo) + b_ref[...]


def _mod_all(c_all, w_mod, b_mod):
    L, D, N = w_mod.shape
    R = c_all.shape[0]
    tn = next(t for t in (2048, 1024, N) if N % t == 0)
    return pl.pallas_call(
        _mod_kernel,
        grid=(L, N // tn),
        in_specs=[pl.BlockSpec((R, D), lambda l, j: (0, 0)),
                  pl.BlockSpec((None, D, tn), lambda l, j: (l, 0, j)),
                  pl.BlockSpec((None, 1, tn), lambda l, j: (l, 0, j))],
        out_specs=pl.BlockSpec((None, R, tn), lambda l, j: (l, 0, j)),
        out_shape=jax.ShapeDtypeStruct((L, R, N), F32),
        compiler_params=_cparams("parallel", "parallel"),
        name="mod",
    )(c_all, w_mod, b_mod.reshape(L, 1, N))


def _rope(x, cos, sina, sinb):
    return x * cos + pltpu.roll(x, V7X_LANES - 16, 1) * sina + pltpu.roll(x, 16, 1) * sinb


def _inproj_kernel(x_ref, sh_ref, sc_ref, g1_ref, w_ref, gq_ref, wuq_ref, gkv_ref, wukv_ref,
                   cos_ref, sina_ref, sinb_ref,
                   qm_ref, km_ref, vm_ref, qs_ref, ks_ref, vs_ref, mq_ref, mk_ref, mv_ref, mo_ref,
                   gs_ref):
    tm = x_ref.shape[0]
    h = (_rms(x_ref[...]) * g1_ref[...]) * (1.0 + sc_ref[...]) + sh_ref[...]
    hb = h.astype(BF16)
    cos, sina, sinb = cos_ref[...], sina_ref[...], sinb_ref[...]
    rope = lambda t: _rope(t, cos, sina, sinb)
    proj = lambda c0, c1: _dot(hb, w_ref[:, c0:c1])
    low = lax.broadcasted_iota(jnp.int32, (tm, V7X_LANES), 1) < MLA_ROPE

    small = proj(C_SMALL, C_END)
    gs_ref[...] = small
    k_rope = jnp.where(low, rope(small), 0.0).astype(BF16)

    zqn = (_rms(proj(C_ZQ, C_ZKV)) * gq_ref[...]).astype(BF16)
    for hh in range(MLA_HEADS):
        qa = _dot(zqn, wuq_ref[:, 256 * hh:256 * (hh + 1)]) * MLA_QSCALE
        qm_ref[:, 256 * hh:256 * hh + 128] = qa[:, :128].astype(BF16)
        qm_ref[:, 256 * hh + 128:256 * (hh + 1)] = jnp.where(low, rope(qa[:, 128:]), 0.0).astype(BF16)

    zkvn = (_rms(proj(C_ZKV, C_SQ)) * gkv_ref[...]).astype(BF16)
    kv = _dot(zkvn, wukv_ref[...])
    for hh in range(MLA_HEADS):
        km_ref[:, 256 * hh:256 * hh + 128] = kv[:, 128 * hh:128 * (hh + 1)].astype(BF16)
        km_ref[:, 256 * hh + 128:256 * (hh + 1)] = k_rope
    vm_ref[...] = kv[:, MLA_HEADS * MLA_NOPE:].astype(BF16)

    for c in range(SWA_QW // 512):
        sq = proj(C_SQ + 512 * c, C_SQ + 512 * (c + 1))
        for p in range(4):
            blk = rope(sq[:, 128 * p:128 * (p + 1)]) * SWA_QSCALE
            qs_ref[:, 512 * c + 128 * p:512 * c + 128 * (p + 1)] = blk.astype(BF16)
    for ref, val in ((ks_ref, rope(proj(C_SK, C_SV))), (vs_ref, proj(C_SV, C_MQ))):
        ref[:, :SWA_KW] = val.astype(BF16)
        ref[:, SWA_KW:] = pltpu.roll(val, SWA_HEAD_DIM, 1).astype(BF16)

    mq_ref[...] = (proj(C_MQ, C_MK) * MLSTM_QSCALE).astype(BF16)
    mk_ref[...] = proj(C_MK, C_MV).astype(BF16)
    mv_ref[...] = proj(C_MV, C_MO).astype(BF16)
    mo_ref[...] = proj(C_MO, C_SMALL).astype(BF16)


def _inproj(x, sh, sc, g1, w_ext, gq, wuq, gkv, wukv, cos, sina, sinb, *, l, tm):
    M, D = x.shape
    S = cos.shape[0]
    R = M // sh.shape[0]
    nt = S // tm
    row = lambda i: (i, 0)
    modv = pl.BlockSpec((None, 1, D), lambda i: (i // (R // tm), 0, 0))
    tab = pl.BlockSpec((tm, V7X_LANES), lambda i: (i % nt, 0))
    widths = (MLA_QW, MLA_QW, MLA_VW, SWA_QW, 2 * SWA_KW, 2 * SWA_KW, MQW, MQW, MVW, MVW)
    out_shape = [jax.ShapeDtypeStruct((M, w), BF16) for w in widths]
    out_shape.append(jax.ShapeDtypeStruct((M, V7X_LANES), F32))
    out_specs = [pl.BlockSpec((tm, w), row) for w in widths] + [pl.BlockSpec((tm, V7X_LANES), row)]
    return pl.pallas_call(
        _inproj_kernel,
        grid=(M // tm,),
        in_specs=[pl.BlockSpec((tm, D), row), modv, modv, _resident((1, D)),
                  _layer_resident(w_ext, l), _resident(gq.shape), _layer_resident(wuq, l),
                  _resident(gkv.shape), _layer_resident(wukv, l), tab, tab, tab],
        out_specs=out_specs,
        out_shape=out_shape,
        compiler_params=_cparams("parallel"),
        name="inproj",
    )(x, sh, sc, g1, w_ext, gq, wuq, gkv, wukv, cos, sina, sinb)


def _mla_kernel(*refs, tk, nk):
    if nk:
        q_ref, k_ref, v_ref, kc_ref, vc_ref, o_ref, m_ref, l_ref, acc_ref = refs
    else:
        q_ref, kc_ref, vc_ref, o_ref, m_ref, l_ref, acc_ref = refs
    q = q_ref[...]
    tq = q.shape[0]
    W = V7X_LANES

    def update(kblk, vblk, first):
        ncol = kblk.shape[0] // W
        s = _dot_nt(q, kblk)
        smax = s[:, 0:W]
        for c in range(1, ncol):
            smax = jnp.maximum(smax, s[:, W * c:W * (c + 1)])
        m_new = jnp.broadcast_to(jnp.max(smax, axis=1, keepdims=True), (tq, W))
        if not first:
            m_prev = m_ref[...]
            m_new = jnp.maximum(m_prev, m_new)
            alpha = jnp.exp2(m_prev - m_new)
        lsum = None
        ps = []
        for c in range(ncol):
            pc = jnp.exp2(s[:, W * c:W * (c + 1)] - m_new)
            lsum = pc if lsum is None else lsum + pc
            ps.append(pc.astype(BF16))
        pv = _dot(jnp.concatenate(ps, axis=1), vblk)
        if first:
            l_ref[...] = lsum
            acc_ref[...] = pv
        else:
            l_ref[...] = alpha * l_ref[...] + lsum
            acc_ref[...] = alpha * acc_ref[...] + pv
        m_ref[...] = m_new

    update(kc_ref[...], vc_ref[...], True)
    if nk:
        def body(j, carry):
            off = pl.multiple_of(j * tk, tk)
            update(k_ref[pl.ds(off, tk), :], v_ref[pl.ds(off, tk), :], False)
            return carry
        lax.fori_loop(0, nk, body, 0)
    l = jnp.sum(l_ref[...], axis=1, keepdims=True)
    o_ref[...] = (acc_ref[...] / l).astype(o_ref.dtype)


def _mla(q, kc, vc, k=None, v=None, *, B, tq, tk):
    M = q.shape[0]
    T = M // B
    Lc = kc.shape[0] // B
    nq = T // tq
    H = MLA_HEADS
    qspec = pl.BlockSpec((tq, 256), lambda b, h, i: (b * nq + i, h))
    cspecs = [pl.BlockSpec((Lc, 256), lambda b, h, i: (b, h)),
              pl.BlockSpec((Lc, MLA_V), lambda b, h, i: (b, h))]
    if k is None:
        nk, in_specs, args = 0, [qspec] + cspecs, (q, kc, vc)
    else:
        nk = T // tk
        in_specs = [qspec, pl.BlockSpec((T, 256), lambda b, h, i: (b, h)),
                    pl.BlockSpec((T, MLA_V), lambda b, h, i: (b, h))] + cspecs
        args = (q, k, v, kc, vc)
    return pl.pallas_call(
        functools.partial(_mla_kernel, tk=tk, nk=nk),
        grid=(B, H, nq),
        in_specs=in_specs,
        out_specs=pl.BlockSpec((tq, MLA_V), lambda b, h, i: (b * nq + i, h)),
        out_shape=jax.ShapeDtypeStruct((M, MLA_VW), BF16),
        scratch_shapes=[pltpu.VMEM((tq, V7X_LANES), F32), pltpu.VMEM((tq, V7X_LANES), F32),
                        pltpu.VMEM((tq, MLA_V), F32)],
        compiler_params=_cparams("parallel", "parallel", "arbitrary"),
        name="mla_latent" if nk else "mla_ctx",
    )(*args)


SWA_SUB = 128
SWA_SPAN = SWA_SUB + 2 * WINDOW


def _swa_kernel(*refs, local, tq, S):
    if local:
        sink_ref, q_ref, k_ref, v_ref, kc_ref, vc_ref, o_ref = refs
    else:
        sink_ref, q_ref, kc_ref, vc_ref, o_ref = refs
    W = V7X_LANES
    d = SWA_HEAD_DIM
    ppg = SWA_HEADS // SWA_KV_HEADS // 2
    Lc = kc_ref.shape[0]

    def variants(x2):
        low = lax.broadcasted_iota(jnp.int32, (x2.shape[0], W), 1) < d
        x, xs = x2[:, :W], x2[:, W:]
        zero = jnp.zeros_like(x)
        return {(0, 0): jnp.where(low, x, zero), (1, 1): jnp.where(low, zero, x),
                (0, 1): jnp.where(low, zero, xs), (1, 0): jnp.where(low, xs, zero)}

    def with_ones(vv):
        return {ge: jnp.concatenate([t, jnp.ones_like(t)], axis=1) for ge, t in vv.items()}

    kcv = variants(kc_ref[...])
    vcv = with_ones(variants(vc_ref[...]))
    for j in range(tq // SWA_SUB):
        r = slice(SWA_SUB * j, SWA_SUB * (j + 1))
        if local:
            qpos = pl.program_id(1) * tq + SWA_SUB * j
            r0 = pl.multiple_of(jnp.clip(qpos - WINDOW, 0, S - SWA_SPAN), W)
            kwin = variants(k_ref[pl.ds(r0, SWA_SPAN), :])
            vwin = with_ones(variants(v_ref[pl.ds(r0, SWA_SPAN), :]))
            kv = {ge: jnp.concatenate([kcv[ge], kwin[ge]], axis=0) for ge in kwin}
            vv = {ge: jnp.concatenate([vcv[ge], vwin[ge]], axis=0) for ge in vwin}
            rel = (lax.broadcasted_iota(jnp.int32, (SWA_SUB, SWA_SPAN), 1)
                   - lax.broadcasted_iota(jnp.int32, (SWA_SUB, SWA_SPAN), 0)) + (r0 - qpos)
            bias = jnp.where(jnp.abs(rel) <= WINDOW, 0.0, NEG_BIG)
            bias = jnp.concatenate([bias] * ppg, axis=0)
        for g in range(SWA_KV_HEADS):
            cols = [slice(W * (ppg * g + pp), W * (ppg * g + pp + 1)) for pp in range(ppg)]
            qs = jnp.concatenate([q_ref[r, c] for c in cols], axis=0)
            pair_out = None
            for e in range(2):
                sink = jnp.concatenate(
                    [jnp.full((SWA_SUB, W), sink_ref[2 * (ppg * g + pp) + e] * LOG2E, F32)
                     for pp in range(ppg)], axis=0)
                s = _dot_nt(qs, kv[g, e] if local else kcv[g, e])
                blocks = [s[:, W * c:W * (c + 1)] for c in range(Lc // W)]
                if local:
                    blocks += [s[:, Lc + W * c:Lc + W * (c + 1)] + bias[:, W * c:W * (c + 1)]
                               for c in range(SWA_SPAN // W)]
                mx = blocks[0]
                for blk in blocks[1:]:
                    mx = jnp.maximum(mx, blk)
                m = jnp.maximum(sink, jnp.broadcast_to(jnp.max(mx, axis=1, keepdims=True), mx.shape))
                ps = [jnp.exp2(blk - m).astype(BF16) for blk in blocks]
                out = _dot(jnp.concatenate(ps, axis=1), vv[g, e] if local else vcv[g, e])
                o = out[:, :W] / (out[:, W:] + jnp.exp2(sink - m))
                pair_out = o if pair_out is None else pair_out + o
            for pp, c in enumerate(cols):
                o_ref[r, c] = pair_out[SWA_SUB * pp:SWA_SUB * (pp + 1)].astype(o_ref.dtype)


def _swa(sink, q, kc, vc, k=None, v=None, *, B, tq):
    M = q.shape[0]
    T = M // B
    Lc = kc.shape[0] // B
    local = k is not None
    if not local:
        tq = T
    nq = T // tq
    qspec = pl.BlockSpec((tq, SWA_QW), lambda b, i: (b * nq + i, 0))
    cspec = pl.BlockSpec((Lc, 2 * SWA_KW), lambda b, i: (b, 0))
    in_specs = [pl.BlockSpec(memory_space=pltpu.SMEM), qspec]
    args = [sink, q]
    if local:
        in_specs += [pl.BlockSpec((T, 2 * SWA_KW), lambda b, i: (b, 0))] * 2
        args += [k, v]
    in_specs += [cspec, cspec]
    args += [kc, vc]
    return pl.pallas_call(
        functools.partial(_swa_kernel, local=local, tq=tq, S=T),
        grid=(B, nq),
        in_specs=in_specs,
        out_specs=qspec,
        out_shape=jax.ShapeDtypeStruct((M, SWA_QW), BF16),
        compiler_params=_cparams("parallel", "parallel"),
        name="swa_latent" if local else "swa_ctx",
    )(*args)


def _log_sigmoid(x):
    return jnp.minimum(x, 0.0) - jnp.log1p(jnp.exp(-jnp.abs(x)))


def _split3(x):
    hi = x.astype(BF16)
    r1 = x - hi.astype(F32)
    mid = r1.astype(BF16)
    return hi, mid, (r1 - mid.astype(F32)).astype(BF16)


def _mlstm_chunk(q_ref, k_ref, v_ref, g_ref, bias_ref, h_ref, c_scr, m_scr, *, rev):
    L = q_ref.shape[0]
    W = V7X_LANES
    H = MLSTM_HEADS
    dk, dv = MLSTM_QK, MLSTM_V
    pre = g_ref[...] + bias_ref[...]
    row = lax.broadcasted_iota(jnp.int32, (L, L), 0)
    col = lax.broadcasted_iota(jnp.int32, (L, L), 1)
    allowed = (col >= row) if rev else (col <= row)
    tri = jnp.where(allowed, 1.0, 0.0).astype(BF16)
    mask_bias = jnp.where(allowed, 0.0, -jnp.inf)
    acc3 = _dot(tri, jnp.concatenate(_split3(_log_sigmoid(pre)), axis=1))
    bcum = acc3[:, :W] + acc3[:, W:2 * W] + acc3[:, 2 * W:]
    lane0 = GATE_LANE0 + (2 * H if rev else 0)
    z = pre - pltpu.roll(bcum, W - H, 1)
    onehot = jnp.where(lax.broadcasted_iota(jnp.int32, (8, W), 1)
                       == lax.broadcasted_iota(jnp.int32, (8, W), 0) + lane0, 1.0, 0.0).astype(BF16)
    zr = _dot_nt(onehot, jnp.concatenate(_split3(z), axis=0))
    y_rows = zr[:, :L] + zr[:, L:2 * L] + zr[:, 2 * L:]
    rep = lambda t, n: jnp.concatenate([t] * n, axis=1)
    stack = lambda f: jnp.concatenate([f(hh) for hh in range(H)], axis=0)
    head = lambda t, hh: t[L * hh:L * (hh + 1)]
    ones_blk = jnp.ones((L, W), BF16)
    last = 0 if rev else L - 1

    b_rep = stack(lambda hh: jnp.broadcast_to(bcum[:, lane0 + H + hh:lane0 + H + hh + 1], (L, W)))
    i_rep = stack(lambda hh: jnp.broadcast_to(pre[:, lane0 + hh:lane0 + hh + 1], (L, W)))
    m_in = stack(lambda hh: jnp.broadcast_to(m_scr[hh, 0:1, :], (L, W)))
    g_tot = stack(lambda hh: jnp.broadcast_to(head(b_rep, hh)[last:last + 1, :], (L, W)))
    y = stack(lambda hh: y_rows[hh:hh + 1, :] + mask_bias)
    c = -jnp.maximum(m_in, jnp.broadcast_to(jnp.max(y, axis=1, keepdims=True), (H * L, W)))
    qs = [q_ref[:, dk * hh:dk * (hh + 1)] for hh in range(H)]
    ks = [k_ref[:, dk * hh:dk * (hh + 1)] for hh in range(H)]
    v_aug = [jnp.concatenate([v_ref[:, dv * hh:dv * (hh + 1)], ones_blk], axis=1) for hh in range(H)]
    s = (stack(lambda hh: _dot_nt(qs[hh], ks[hh])) * jnp.exp(y + rep(c, L // W))).astype(BF16)
    c_in = [c_scr[hh] for hh in range(H)]
    tot = (stack(lambda hh: _dot(head(s, hh), v_aug[hh]))
           + rep(jnp.exp(m_in + c), 2) * stack(lambda hh: _dot(qs[hh], c_in[hh].astype(BF16))))
    hout = tot[:, :dv] / jnp.maximum(jnp.abs(tot[:, dv:]), jnp.exp(c - b_rep))
    for hh in range(H):
        h_ref[:, dv * hh:dv * (hh + 1)] = head(hout, hh)

    a = g_tot - b_rep + i_rep
    a_max = stack(lambda hh: jnp.broadcast_to(jnp.max(head(a, hh), axis=0, keepdims=True), (L, W)))
    w = jnp.exp(a - a_max)
    m_new = jnp.maximum(g_tot + m_in, a_max)
    decay = jnp.exp(g_tot + m_in - m_new)
    grow = jnp.exp(a_max - m_new)
    for hh in range(H):
        kw = (ks[hh].astype(F32) * head(w, hh)[:, :dk]).astype(BF16)
        d_c = lax.dot_general(kw, v_aug[hh], (((0,), (0,)), ((), ())),
                              preferred_element_type=F32)
        c_scr[hh] = (rep(head(decay, hh)[:dk], 2) * c_in[hh] + rep(head(grow, hh)[:dk], 2) * d_c)
        m_scr[hh] = head(m_new, hh)[:m_scr.shape[1]]


def _mlstm_kernel(qf_ref, kf_ref, vf_ref, gf_ref, qb_ref, kb_ref, vb_ref, gb_ref, bias_ref, c0_ref, m0_ref,
                  hf_ref, hb_ref, cf_ref, mf_ref, c_scr, m_scr, *, nchunks):
    n = pl.program_id(1)

    @pl.when(n == 0)
    def _():
        c_scr[...] = c0_ref[...]
        m_scr[...] = m0_ref[...]

    _mlstm_chunk(qf_ref, kf_ref, vf_ref, gf_ref, bias_ref, hf_ref, c_scr.at[0], m_scr.at[0], rev=False)
    _mlstm_chunk(qb_ref, kb_ref, vb_ref, gb_ref, bias_ref, hb_ref, c_scr.at[1], m_scr.at[1], rev=True)

    @pl.when(n == nchunks - 1)
    def _():
        cf_ref[...] = c_scr[...]
        mf_ref[...] = m_scr[...]


def _mlstm(q, k, v, g, bias, c0, m0, *, B, L):
    M = q.shape[0]
    N = M // B // L
    H = MLSTM_HEADS
    fwd = lambda b, n: (b * N + n, 0)
    bwd = lambda b, n: (b * N + N - 1 - n, 0)
    data = lambda im: [pl.BlockSpec((L, MQW), im), pl.BlockSpec((L, MQW), im),
                       pl.BlockSpec((L, MVW), im), pl.BlockSpec((L, V7X_LANES), im)]
    cspec = pl.BlockSpec((None, 2, H, MLSTM_QK, 256), lambda b, n: (b, 0, 0, 0, 0))
    mspec = pl.BlockSpec((None, 2, H, 8, V7X_LANES), lambda b, n: (b, 0, 0, 0, 0))
    return pl.pallas_call(
        functools.partial(_mlstm_kernel, nchunks=N),
        grid=(B, N),
        in_specs=data(fwd) + data(bwd) + [_resident(bias.shape), cspec, mspec],
        out_specs=[pl.BlockSpec((L, MVW), fwd), pl.BlockSpec((L, MVW), bwd), cspec, mspec],
        out_shape=[jax.ShapeDtypeStruct((M, MVW), F32), jax.ShapeDtypeStruct((M, MVW), F32),
                   jax.ShapeDtypeStruct(c0.shape, F32), jax.ShapeDtypeStruct(m0.shape, F32)],
        scratch_shapes=[pltpu.VMEM((2, H, MLSTM_QK, 256), F32), pltpu.VMEM((2, H, 8, V7X_LANES), F32)],
        compiler_params=_cparams("parallel", "arbitrary"),
        name="mlstm",
    )(q, k, v, g, q, k, v, g, bias, c0, m0)


def _outproj_kernel(ya_ref, ys_ref, hf_ref, hb_ref, mo_ref, gh_ref, w_ref, x_ref, gt_ref,
                    sh_ref, sc_ref, g2_ref, o_ref, h2_ref, *, tn):
    dv = MLSTM_V
    parts = []
    for hh in range(MLSTM_HEADS):
        sl = slice(dv * hh, dv * (hh + 1))
        hn = _rms(hf_ref[:, sl] + hb_ref[:, sl]) * gh_ref[:, sl]
        parts.append((hn * jax.nn.sigmoid(mo_ref[:, sl].astype(F32))).astype(BF16))
    ym = jnp.concatenate(parts, axis=1)
    ya = ya_ref[...]
    ys = ys_ref[...]
    r1 = MLA_VW
    r2 = r1 + SWA_QW
    D = o_ref.shape[1]
    ssq = None
    for c in range(D // tn):
        cs = slice(tn * c, tn * (c + 1))
        acc = _dot(ya, w_ref[0:r1, cs]) + _dot(ys, w_ref[r1:r2, cs]) + _dot(ym, w_ref[r2:, cs])
        xn = x_ref[:, cs] + gt_ref[:, cs] * acc
        o_ref[:, cs] = xn
        part = jnp.sum(xn * xn, axis=-1, keepdims=True)
        ssq = part if ssq is None else ssq + part
    inv = lax.rsqrt(ssq * (1.0 / D) + NORM_EPS)
    for c in range(D // tn):
        cs = slice(tn * c, tn * (c + 1))
        h2 = (o_ref[:, cs] * inv * g2_ref[:, cs]) * (1.0 + sc_ref[:, cs]) + sh_ref[:, cs]
        h2_ref[:, cs] = h2.astype(BF16)


def _outproj(ya, ys, hf, hb, mo, gh, w_out, x, gt, sh, sc, g2, *, l, tm):
    M, D = x.shape
    R = M // gt.shape[0]
    row = lambda i: (i, 0)
    modv = pl.BlockSpec((None, 1, D), lambda i: (i // (R // tm), 0, 0))
    return pl.pallas_call(
        functools.partial(_outproj_kernel, tn=min(512, D)),
        grid=(M // tm,),
        in_specs=[pl.BlockSpec((tm, MLA_VW), row), pl.BlockSpec((tm, SWA_QW), row),
                  pl.BlockSpec((tm, MVW), row), pl.BlockSpec((tm, MVW), row),
                  pl.BlockSpec((tm, MVW), row), _resident(gh.shape), _layer_resident(w_out, l),
                  pl.BlockSpec((tm, D), row), modv, modv, modv, _resident((1, D))],
        out_specs=[pl.BlockSpec((tm, D), row), pl.BlockSpec((tm, D), row)],
        out_shape=[jax.ShapeDtypeStruct((M, D), F32), jax.ShapeDtypeStruct((M, D), BF16)],
        compiler_params=_cparams("parallel"),
        name="outproj",
    )(ya, ys, hf, hb, mo, gh, w_out, x, gt, sh, sc, g2)


def _ffn_kernel(x_ref, h_ref, gt_ref, gfin_ref, w1_ref, w2_ref, o_ref, *, nf, final):
    f = pl.program_id(1)
    u = jnp.maximum(_dot(h_ref[...], w1_ref[...]), 0.0)
    a = (u * u).astype(BF16)

    @pl.when(f == 0)
    def _():
        o_ref[...] = _dot(a, w2_ref[...])

    @pl.when(f != 0)
    def _():
        o_ref[...] += _dot(a, w2_ref[...])

    @pl.when(f == nf - 1)
    def _():
        D = o_ref.shape[1]
        tn = min(512, D)
        ssq = None
        for c in range(D // tn):
            cs = slice(tn * c, tn * (c + 1))
            y = x_ref[:, cs] + gt_ref[:, cs] * o_ref[:, cs]
            o_ref[:, cs] = y
            if final:
                part = jnp.sum(y * y, axis=-1, keepdims=True)
                ssq = part if ssq is None else ssq + part
        if final:
            inv = lax.rsqrt(ssq * (1.0 / D) + NORM_EPS)
            for c in range(D // tn):
                cs = slice(tn * c, tn * (c + 1))
                o_ref[:, cs] = o_ref[:, cs] * inv * gfin_ref[:, cs]


def _ffn(x, h2, gt, gfin, w1, w2, *, l, tm, tf, final):
    M, D = x.shape
    FF = w1.shape[2]
    R = M // gt.shape[0]
    nf = FF // tf
    rows = lambda i, f: (i, 0)
    return pl.pallas_call(
        functools.partial(_ffn_kernel, nf=nf, final=final),
        grid=(M // tm, nf),
        in_specs=[pl.BlockSpec((tm, D), rows), pl.BlockSpec((tm, D), rows),
                  pl.BlockSpec((None, 1, D), lambda i, f: (i // (R // tm), 0, 0)), _resident((1, D)),
                  pl.BlockSpec((None, D, tf), lambda i, f: (l, 0, f)),
                  pl.BlockSpec((None, tf, D), lambda i, f: (l, f, 0))],
        out_specs=pl.BlockSpec((tm, D), rows),
        out_shape=jax.ShapeDtypeStruct((M, D), F32),
        compiler_params=_cparams("parallel", "arbitrary"),
        name="ffn",
    )(x, h2, gt, gfin, w1, w2)


def _rope_tables(S):
    half = SWA_HEAD_DIM // 2
    pos = jnp.arange(S)
    inv = ROPE_BASE ** (-jnp.arange(0, half, 2, dtype=F32) / half)
    ar = (pos // GRID_W)[:, None].astype(F32) * inv
    ac = (pos % GRID_W)[:, None].astype(F32) * inv
    ang = jnp.concatenate([ar, ar, ac, ac] * 2, axis=-1)
    first = (jnp.arange(V7X_LANES) % 32) < 16
    sin = jnp.sin(ang)
    return jnp.cos(ang), jnp.where(first, -sin, 0.0), jnp.where(first, 0.0, sin)


def _w_in_segments():
    widths = (MLA_Q_RANK, MLA_KV_RANK, MLA_ROPE, SWA_QW, SWA_KW, SWA_KW, MQW, MQW, MVW, N_GATES, MVW)
    src = [0]
    for w in widths:
        src.append(src[-1] + w)
    segs, dst = [], 0
    for i in (0, 1, 3, 4, 5, 6, 7, 8, 10, 2, 9):
        segs.append((src[i], widths[i], dst))
        dst += widths[i]
    return segs, dst


def _permute_w_in_kernel(w_ref, o_ref):
    segs, end = _w_in_segments()
    for s0, w, d0 in segs:
        o_ref[:, d0:d0 + w] = w_ref[:, s0:s0 + w].astype(BF16)
    o_ref[:, end:] = jnp.zeros((o_ref.shape[0], o_ref.shape[1] - end), BF16)


def _permute_w_in(w_in):
    L, D, N = w_in.shape
    tr = _row_tile(D, 256)
    return pl.pallas_call(
        _permute_w_in_kernel,
        grid=(L, D // tr),
        in_specs=[pl.BlockSpec((None, tr, N), lambda l, i: (l, i, 0))],
        out_specs=pl.BlockSpec((None, tr, C_END), lambda l, i: (l, i, 0)),
        out_shape=jax.ShapeDtypeStruct((L, D, C_END), BF16),
        compiler_params=_cparams("parallel", "parallel"),
        name="permute_w_in",
    )(w_in)


def _permute_w_uq(w_uq):
    L, Rk, _ = w_uq.shape
    w = w_uq.reshape(L, Rk, MLA_HEADS, MLA_NOPE + MLA_ROPE)
    w = jnp.pad(w, ((0, 0), (0, 0), (0, 0), (0, 256 - MLA_NOPE - MLA_ROPE)))
    return w.reshape(L, Rk, MLA_QW).astype(BF16)


def _permute_w_ukv(w_ukv):
    L, Rk, _ = w_ukv.shape
    w = w_ukv.reshape(L, Rk, MLA_HEADS, MLA_NOPE + MLA_V)
    k = w[..., :MLA_NOPE].reshape(L, Rk, MLA_HEADS * MLA_NOPE)
    v = w[..., MLA_NOPE:].reshape(L, Rk, MLA_VW)
    return jnp.concatenate([k, v], axis=-1).astype(BF16)


def _row_tile(rows, want):
    return want if rows % want == 0 else rows


def kernel(x, c, ctx, c_ctx, w_mod, b_mod, g_norm1, g_norm2, w_in, mla_g_q, mla_w_uq, mla_g_kv,
           mla_w_ukv, swa_sink, mlstm_gate_bias, mlstm_g_h, w_out, w_ff1, w_ff2, g_final):
    B, S, D = x.shape
    Lc = ctx.shape[1]
    depth = w_in.shape[0]
    H = MLSTM_HEADS

    rows = -(-(B + 1) // 8) * 8
    c_all = jnp.concatenate([c, c_ctx[None, :], jnp.zeros((rows - B - 1, D), F32)], axis=0)
    mod = _mod_all(c_all, w_mod, b_mod)
    mod6 = mod.reshape(depth, rows, 6, D)

    w_in_p = _permute_w_in(w_in)
    w_uq_p = _permute_w_uq(mla_w_uq)
    w_ukv_p = _permute_w_ukv(mla_w_ukv)
    w_out_b = w_out.astype(BF16)
    w1_b = w_ff1.astype(BF16)
    w2_b = w_ff2.astype(BF16)
    cos, sina, sinb = _rope_tables(S)
    ones_t = jnp.ones((Lc, V7X_LANES), F32)
    zeros_t = jnp.zeros((Lc, V7X_LANES), F32)
    bias_lanes = jnp.pad(mlstm_gate_bias.reshape(depth, 1, N_GATES),
                         ((0, 0), (0, 0), (GATE_LANE0, V7X_LANES - GATE_LANE0 - N_GATES)))
    c_zero = jnp.zeros((B, 2, H, MLSTM_QK, 256), F32)
    m_zero = jnp.zeros((B, 2, H, 8, V7X_LANES), F32)

    tm = _row_tile(S, 512)
    FF = w_ff1.shape[2]
    tm_ffn, tf = _row_tile(S, 512), min(2048, FF)
    tq_mla = tk_mla = _row_tile(S, 2048)
    gfin = g_final[None, :]
    tq_swa = _row_tile(S, 512)
    chunk = _row_tile(S, 256)

    xs = x.reshape(B * S, D)
    xc = ctx.reshape(B * Lc, D)
    for l in range(depth):
        need_ctx = l < depth - 1
        vec = lambda j: mod6[l, :B, j][:, None, :]
        vecc = lambda j: mod6[l, B:B + 1, j][:, None, :]
        g1 = g_norm1[l][None, :]
        g2 = g_norm2[l][None, :]
        gq = mla_g_q[l][None, :]
        gkv = mla_g_kv[l][None, :]
        gh = mlstm_g_h[l].reshape(1, MVW)
        inproj = functools.partial(_inproj, g1=g1, w_ext=w_in_p, gq=gq, wuq=w_uq_p, gkv=gkv,
                                   wukv=w_ukv_p, l=l)
        ffn = functools.partial(_ffn, gfin=gfin, w1=w1_b, w2=w2_b, l=l, tf=tf)
        (qmc, kmc, vmc, qsc, ksc, vsc, mqc, mkc, mvc, moc, gsc) = inproj(
            xc, vecc(0), vecc(1), cos=ones_t, sina=zeros_t, sinb=zeros_t, tm=Lc)
        hfc, hbc, c_st, m_st = _mlstm(mqc, mkc, mvc, gsc, bias_lanes[l], c_zero, m_zero, B=B, L=Lc)

        (qm, km, vm, qs, ks, vs, mq, mk, mv, mo, gs) = inproj(
            xs, vec(0), vec(1), cos=cos, sina=sina, sinb=sinb, tm=tm)
        y_mla = _mla(qm, kmc, vmc, km, vm, B=B, tq=tq_mla, tk=tk_mla)
        y_swa = _swa(swa_sink[l], qs, ksc, vsc, ks, vs, B=B, tq=tq_swa)
        hf, hb, _, _ = _mlstm(mq, mk, mv, gs, bias_lanes[l], c_st, m_st, B=B, L=chunk)
        xs, h2 = _outproj(y_mla, y_swa, hf, hb, mo, gh, w_out_b, xs, vec(2), vec(3), vec(4), g2, l=l, tm=tm)
        xs = ffn(xs, h2, vec(5), tm=tm_ffn, final=not need_ctx)
        if need_ctx:
            yc_mla = _mla(qmc, kmc, vmc, B=B, tq=Lc, tk=Lc)
            yc_swa = _swa(swa_sink[l], qsc, ksc, vsc, B=B, tq=Lc)
            xc, h2c = _outproj(yc_mla, yc_swa, hfc, hbc, moc, gh, w_out_b, xc, vecc(2), vecc(3), vecc(4), g2,
                               l=l, tm=Lc)
            xc = ffn(xc, h2c, vecc(5), tm=B * Lc, tf=min(512, FF), final=False)
    return xs.reshape(B, S, D)
```

```python
import functools
import math

import jax
import jax.numpy as jnp
from jax import lax
from jax.experimental import pallas as pl
from jax.experimental.pallas import tpu as pltpu

F32 = jnp.float32
BF16 = jnp.bfloat16

GRID_W = 64
ROPE_BASE = 10000.0
NORM_EPS = 1e-6
MLA_HEADS = 4
MLA_Q_RANK = 512
MLA_KV_RANK = 256
MLA_NOPE = 128
MLA_ROPE = 64
MLA_V = 128
SWA_HEADS = 16
SWA_KV_HEADS = 2
SWA_HEAD_DIM = 64
WINDOW = 128
MLSTM_HEADS = 4
MLSTM_QK = 64
MLSTM_V = 128
N_GATES = 4 * MLSTM_HEADS

V7X_LANES = 128
V7X_VMEM_LIMIT_BYTES = 63 * 1024 * 1024

LOG2E = math.log2(math.e)
MLA_QSCALE = (MLA_NOPE + MLA_ROPE) ** -0.5 * LOG2E
SWA_QSCALE = SWA_HEAD_DIM ** -0.5 * LOG2E
MLSTM_QSCALE = MLSTM_QK ** -0.5
NEG_BIG = -1e30

C_ZQ = 0
C_ZKV = C_ZQ + MLA_Q_RANK
C_SQ = C_ZKV + MLA_KV_RANK
C_SK = C_SQ + SWA_HEADS * SWA_HEAD_DIM
C_SV = C_SK + SWA_KV_HEADS * SWA_HEAD_DIM
C_MQ = C_SV + SWA_KV_HEADS * SWA_HEAD_DIM
C_MK = C_MQ + MLSTM_HEADS * MLSTM_QK
C_MV = C_MK + MLSTM_HEADS * MLSTM_QK
C_MO = C_MV + MLSTM_HEADS * MLSTM_V
C_SMALL = C_MO + MLSTM_HEADS * MLSTM_V
C_END = C_SMALL + V7X_LANES
GATE_LANE0 = MLA_ROPE

MLA_QW = MLA_HEADS * 256
MLA_VW = MLA_HEADS * MLA_V
SWA_QW = SWA_HEADS * SWA_HEAD_DIM
SWA_KW = SWA_KV_HEADS * SWA_HEAD_DIM
MQW = MLSTM_HEADS * MLSTM_QK
MVW = MLSTM_HEADS * MLSTM_V


def _cparams(*sem):
    return pltpu.CompilerParams(dimension_semantics=sem, vmem_limit_bytes=V7X_VMEM_LIMIT_BYTES)


def _resident(shape):
    nd = len(shape)
    return pl.BlockSpec(shape, lambda *_: (0,) * nd, pipeline_mode=pl.Buffered(1))


def _layer_resident(stacked, l):
    nd = stacked.ndim - 1
    return pl.BlockSpec((None,) + stacked.shape[1:], lambda *_: (l,) + (0,) * nd,
                        pipeline_mode=pl.Buffered(1))


def _dot(a, b):
    return jnp.dot(a, b, preferred_element_type=F32)


def _dot_nt(a, b):
    return lax.dot_general(a, b, (((1,), (1,)), ((), ())), preferred_element_type=F32)


def _rms(x):
    return x * lax.rsqrt(jnp.mean(x * x, axis=-1, keepdims=True) + NORM_EPS)


def _mod_kernel(c_ref, w_ref, b_ref, o_ref):
    c = c_ref[...]
    a = c * jax.nn.sigmoid(c)
    w = w_ref[...]
    a_hi, w_hi = a.astype(BF16), w.astype(BF16)
    a_lo = (a - a_hi.astype(F32)).astype(BF16)
    w_lo = (w - w_hi.astype(F32)).astype(BF16)
    o_ref[...] = _dot(a_hi, w_hi) + _dot(a_lo, w_hi) + _dot(a_hi, w_lo) + b_ref[...]


def _mod_all(c_all, w_mod, b_mod):
    L, D, N = w_mod.shape
    R = c_all.shape[0]
    tn = next(t for t in (2048, 1024, N) if N % t == 0)
    return pl.pallas_call(
        _mod_kernel,
        grid=(L, N // tn),
        in_specs=[pl.BlockSpec((R, D), lambda l, j: (0, 0)),
                  pl.BlockSpec((None, D, tn), lambda l, j: (l, 0, j)),
                  pl.BlockSpec((None, 1, tn), lambda l, j: (l, 0, j))],
        out_specs=pl.BlockSpec((None, R, tn), lambda l, j: (l, 0, j)),
        out_shape=jax.ShapeDtypeStruct((L, R, N), F32),
        compiler_params=_cparams("parallel", "parallel"),
        name="mod",
    )(c_all, w_mod, b_mod.reshape(L, 1, N))


def _rope(x, cos, sina, sinb):
    return x * cos + pltpu.roll(x, V7X_LANES - 16, 1) * sina + pltpu.roll(x, 16, 1) * sinb


def _inproj_kernel(x_ref, sh_ref, sc_ref, g1_ref, w_ref, gq_ref, wuq_ref, gkv_ref, wukv_ref,
                   cos_ref, sina_ref, sinb_ref,
                   qm_ref, km_ref, vm_ref, qs_ref, ks_ref, vs_ref, mq_ref, mk_ref, mv_ref, mo_ref,
                   gs_ref):
    tm = x_ref.shape[0]
    h = (_rms(x_ref[...]) * g1_ref[...]) * (1.0 + sc_ref[...]) + sh_ref[...]
    hb = h.astype(BF16)
    cos, sina, sinb = cos_ref[...], sina_ref[...], sinb_ref[...]
    rope = lambda t: _rope(t, cos, sina, sinb)
    proj = lambda c0, c1: _dot(hb, w_ref[:, c0:c1])
    low = lax.broadcasted_iota(jnp.int32, (tm, V7X_LANES), 1) < MLA_ROPE

    small = proj(C_SMALL, C_END)
    gs_ref[...] = small
    k_rope = jnp.where(low, rope(small), 0.0).astype(BF16)

    zqn = (_rms(proj(C_ZQ, C_ZKV)) * gq_ref[...]).astype(BF16)
    qall = _dot(zqn, wuq_ref[...]) * MLA_QSCALE
    for hh in range(MLA_HEADS):
        qa = qall[:, 256 * hh:256 * (hh + 1)]
        qm_ref[:, 256 * hh:256 * hh + 128] = qa[:, :128].astype(BF16)
        qm_ref[:, 256 * hh + 128:256 * (hh + 1)] = jnp.where(low, rope(qa[:, 128:]), 0.0).astype(BF16)

    zkvn = (_rms(proj(C_ZKV, C_SQ)) * gkv_ref[...]).astype(BF16)
    kv = _dot(zkvn, wukv_ref[...])
    for hh in range(MLA_HEADS):
        km_ref[:, 256 * hh:256 * hh + 128] = kv[:, 128 * hh:128 * (hh + 1)].astype(BF16)
        km_ref[:, 256 * hh + 128:256 * (hh + 1)] = k_rope
    vm_ref[...] = kv[:, MLA_HEADS * MLA_NOPE:].astype(BF16)

    for c in range(SWA_QW // 512):
        sq = proj(C_SQ + 512 * c, C_SQ + 512 * (c + 1))
        for p in range(4):
            blk = rope(sq[:, 128 * p:128 * (p + 1)]) * SWA_QSCALE
            qs_ref[:, 512 * c + 128 * p:512 * c + 128 * (p + 1)] = blk.astype(BF16)
    for ref, val in ((ks_ref, rope(proj(C_SK, C_SV))), (vs_ref, proj(C_SV, C_MQ))):
        ref[:, :SWA_KW] = val.astype(BF16)
        ref[:, SWA_KW:] = pltpu.roll(val, SWA_HEAD_DIM, 1).astype(BF16)

    mq_ref[...] = (proj(C_MQ, C_MK) * MLSTM_QSCALE).astype(BF16)
    mk_ref[...] = proj(C_MK, C_MV).astype(BF16)
    mv_ref[...] = proj(C_MV, C_MO).astype(BF16)
    mo_ref[...] = proj(C_MO, C_SMALL).astype(BF16)


def _inproj(x, sh, sc, g1, w_ext, gq, wuq, gkv, wukv, cos, sina, sinb, *, l, tm):
    M, D = x.shape
    S = cos.shape[0]
    R = M // sh.shape[0]
    nt = S // tm
    row = lambda i: (i, 0)
    modv = pl.BlockSpec((None, 1, D), lambda i: (i // (R // tm), 0, 0))
    tab = pl.BlockSpec((tm, V7X_LANES), lambda i: (i % nt, 0))
    widths = (MLA_QW, MLA_QW, MLA_VW, SWA_QW, 2 * SWA_KW, 2 * SWA_KW, MQW, MQW, MVW, MVW)
    out_shape = [jax.ShapeDtypeStruct((M, w), BF16) for w in widths]
    out_shape.append(jax.ShapeDtypeStruct((M, V7X_LANES), F32))
    out_specs = [pl.BlockSpec((tm, w), row) for w in widths] + [pl.BlockSpec((tm, V7X_LANES), row)]
    return pl.pallas_call(
        _inproj_kernel,
        grid=(M // tm,),
        in_specs=[pl.BlockSpec((tm, D), row), modv, modv, _resident((1, D)),
                  _layer_resident(w_ext, l), _resident(gq.shape), _layer_resident(wuq, l),
                  _resident(gkv.shape), _layer_resident(wukv, l), tab, tab, tab],
        out_specs=out_specs,
        out_shape=out_shape,
        compiler_params=_cparams("parallel"),
        name="inproj",
    )(x, sh, sc, g1, w_ext, gq, wuq, gkv, wukv, cos, sina, sinb)


def _mla_kernel(*refs, tk, nk):
    if nk:
        q_ref, k_ref, v_ref, kc_ref, vc_ref, o_ref, m_ref, l_ref, acc_ref = refs
    else:
        q_ref, kc_ref, vc_ref, o_ref, m_ref, l_ref, acc_ref = refs
    q = q_ref[...]
    tq = q.shape[0]
    W = V7X_LANES

    def update(kblk, vblk, first):
        ncol = kblk.shape[0] // W
        s = _dot_nt(q, kblk)
        smax = s[:, 0:W]
        for c in range(1, ncol):
            smax = jnp.maximum(smax, s[:, W * c:W * (c + 1)])
        m_new = jnp.broadcast_to(jnp.max(smax, axis=1, keepdims=True), (tq, W))
        if not first:
            m_prev = m_ref[...]
            m_new = jnp.maximum(m_prev, m_new)
            alpha = jnp.exp2(m_prev - m_new)
        lsum = None
        ps = []
        for c in range(ncol):
            pc = jnp.exp2(s[:, W * c:W * (c + 1)] - m_new)
            lsum = pc if lsum is None else lsum + pc
            ps.append(pc.astype(BF16))
        pv = _dot(jnp.concatenate(ps, axis=1), vblk)
        if first:
            l_ref[...] = lsum
            acc_ref[...] = pv
        else:
            l_ref[...] = alpha * l_ref[...] + lsum
            acc_ref[...] = alpha * acc_ref[...] + pv
        m_ref[...] = m_new

    update(kc_ref[...], vc_ref[...], True)
    if nk:
        def body(j, carry):
            off = pl.multiple_of(j * tk, tk)
            update(k_ref[pl.ds(off, tk), :], v_ref[pl.ds(off, tk), :], False)
            return carry
        lax.fori_loop(0, nk, body, 0)
    l = jnp.sum(l_ref[...], axis=1, keepdims=True)
    o_ref[...] = (acc_ref[...] / l).astype(o_ref.dtype)


def _mla(q, kc, vc, k=None, v=None, *, B, tq, tk):
    M = q.shape[0]
    T = M // B
    Lc = kc.shape[0] // B
    nq = T // tq
    H = MLA_HEADS
    qspec = pl.BlockSpec((tq, 256), lambda b, h, i: (b * nq + i, h))
    cspecs = [pl.BlockSpec((Lc, 256), lambda b, h, i: (b, h)),
              pl.BlockSpec((Lc, MLA_V), lambda b, h, i: (b, h))]
    if k is None:
        nk, in_specs, args = 0, [qspec] + cspecs, (q, kc, vc)
    else:
        nk = T // tk
        in_specs = [qspec, pl.BlockSpec((T, 256), lambda b, h, i: (b, h)),
                    pl.BlockSpec((T, MLA_V), lambda b, h, i: (b, h))] + cspecs
        args = (q, k, v, kc, vc)
    return pl.pallas_call(
        functools.partial(_mla_kernel, tk=tk, nk=nk),
        grid=(B, H, nq),
        in_specs=in_specs,
        out_specs=pl.BlockSpec((tq, MLA_V), lambda b, h, i: (b * nq + i, h)),
        out_shape=jax.ShapeDtypeStruct((M, MLA_VW), BF16),
        scratch_shapes=[pltpu.VMEM((tq, V7X_LANES), F32), pltpu.VMEM((tq, V7X_LANES), F32),
                        pltpu.VMEM((tq, MLA_V), F32)],
        compiler_params=_cparams("parallel", "parallel", "arbitrary"),
        name="mla_latent" if nk else "mla_ctx",
    )(*args)


SWA_SUB = 128
SWA_SPAN = SWA_SUB + 2 * WINDOW


def _swa_kernel(*refs, local, tq, S):
    if local:
        sink_ref, q_ref, k_ref, v_ref, kc_ref, vc_ref, o_ref = refs
    else:
        sink_ref, q_ref, kc_ref, vc_ref, o_ref = refs
    W = V7X_LANES
    d = SWA_HEAD_DIM
    ppg = SWA_HEADS // SWA_KV_HEADS // 2
    Lc = kc_ref.shape[0]

    def variants(x2):
        low = lax.broadcasted_iota(jnp.int32, (x2.shape[0], W), 1) < d
        x, xs = x2[:, :W], x2[:, W:]
        zero = jnp.zeros_like(x)
        return {(0, 0): jnp.where(low, x, zero), (1, 1): jnp.where(low, zero, x),
                (0, 1): jnp.where(low, zero, xs), (1, 0): jnp.where(low, xs, zero)}

    def with_ones(vv):
        def ones_half(t, e):
            mine = (lax.broadcasted_iota(jnp.int32, t.shape, 1) < d) == (e == 0)
            return jnp.where(mine, 1.0, 0.0).astype(t.dtype)
        return {(g, e): jnp.concatenate([t, ones_half(t, e)], axis=1) for (g, e), t in vv.items()}

    kcv = variants(kc_ref[...])
    vcv = with_ones(variants(vc_ref[...]))
    for j in range(tq // SWA_SUB):
        r = slice(SWA_SUB * j, SWA_SUB * (j + 1))
        if local:
            qpos = pl.program_id(1) * tq + SWA_SUB * j
            r0 = pl.multiple_of(jnp.clip(qpos - WINDOW, 0, S - SWA_SPAN), W)
            kwin = variants(k_ref[pl.ds(r0, SWA_SPAN), :])
            vwin = with_ones(variants(v_ref[pl.ds(r0, SWA_SPAN), :]))
            kv = {ge: jnp.concatenate([kcv[ge], kwin[ge]], axis=0) for ge in kwin}
            vv = {ge: jnp.concatenate([vcv[ge], vwin[ge]], axis=0) for ge in vwin}
            rel = (lax.broadcasted_iota(jnp.int32, (SWA_SUB, SWA_SPAN), 1)
                   - lax.broadcasted_iota(jnp.int32, (SWA_SUB, SWA_SPAN), 0)) + (r0 - qpos)
            bias = jnp.where(jnp.abs(rel) <= WINDOW, 0.0, NEG_BIG)
            bias = jnp.concatenate([bias] * ppg, axis=0)
        for g in range(SWA_KV_HEADS):
            cols = [slice(W * (ppg * g + pp), W * (ppg * g + pp + 1)) for pp in range(ppg)]
            qs = jnp.concatenate([q_ref[r, c] for c in cols], axis=0)
            ksrc, vsrc = (kv, vv) if local else (kcv, vcv)
            nkeys = ksrc[g, 0].shape[0]
            s = _dot_nt(qs, jnp.concatenate([ksrc[g, 0], ksrc[g, 1]], axis=0))
            ps, sink_terms = [], []
            for e in range(2):
                sink = jnp.concatenate(
                    [jnp.full((SWA_SUB, W), sink_ref[2 * (ppg * g + pp) + e] * LOG2E, F32)
                     for pp in range(ppg)], axis=0)
                c0 = nkeys * e
                blocks = [s[:, c0 + W * c:c0 + W * (c + 1)] for c in range(Lc // W)]
                if local:
                    blocks += [s[:, c0 + Lc + W * c:c0 + Lc + W * (c + 1)] + bias[:, W * c:W * (c + 1)]
                               for c in range(SWA_SPAN // W)]
                mx = blocks[0]
                for blk in blocks[1:]:
                    mx = jnp.maximum(mx, blk)
                m = jnp.maximum(sink, jnp.broadcast_to(jnp.max(mx, axis=1, keepdims=True), mx.shape))
                ps += [jnp.exp2(blk - m).astype(BF16) for blk in blocks]
                sink_terms.append(jnp.exp2(sink - m))
            out = _dot(jnp.concatenate(ps, axis=1),
                       jnp.concatenate([vsrc[g, 0], vsrc[g, 1]], axis=0))
            first_half = lax.broadcasted_iota(jnp.int32, sink_terms[0].shape, 1) < d
            pair_out = out[:, :W] / (out[:, W:] + jnp.where(first_half, sink_terms[0], sink_terms[1]))
            for pp, c in enumerate(cols):
                o_ref[r, c] = pair_out[SWA_SUB * pp:SWA_SUB * (pp + 1)].astype(o_ref.dtype)


def _swa(sink, q, kc, vc, k=None, v=None, *, B, tq):
    M = q.shape[0]
    T = M // B
    Lc = kc.shape[0] // B
    local = k is not None
    if not local:
        tq = T
    nq = T // tq
    qspec = pl.BlockSpec((tq, SWA_QW), lambda b, i: (b * nq + i, 0))
    cspec = pl.BlockSpec((Lc, 2 * SWA_KW), lambda b, i: (b, 0))
    in_specs = [pl.BlockSpec(memory_space=pltpu.SMEM), qspec]
    args = [sink, q]
    if local:
        in_specs += [pl.BlockSpec((T, 2 * SWA_KW), lambda b, i: (b, 0))] * 2
        args += [k, v]
    in_specs += [cspec, cspec]
    args += [kc, vc]
    return pl.pallas_call(
        functools.partial(_swa_kernel, local=local, tq=tq, S=T),
        grid=(B, nq),
        in_specs=in_specs,
        out_specs=qspec,
        out_shape=jax.ShapeDtypeStruct((M, SWA_QW), BF16),
        compiler_params=_cparams("parallel", "parallel"),
        name="swa_latent" if local else "swa_ctx",
    )(*args)


def _log_sigmoid(x):
    return jnp.minimum(x, 0.0) - jnp.log1p(jnp.exp(-jnp.abs(x)))


def _split3(x):
    hi = x.astype(BF16)
    r1 = x - hi.astype(F32)
    mid = r1.astype(BF16)
    return hi, mid, (r1 - mid.astype(F32)).astype(BF16)


def _mlstm_chunk(q_ref, k_ref, v_ref, g_ref, bias_ref, h_ref, c_scr, m_scr, *, rev):
    L = q_ref.shape[0]
    W = V7X_LANES
    H = MLSTM_HEADS
    dk, dv = MLSTM_QK, MLSTM_V
    pre = g_ref[...] + bias_ref[...]
    row = lax.broadcasted_iota(jnp.int32, (L, L), 0)
    col = lax.broadcasted_iota(jnp.int32, (L, L), 1)
    allowed = (col >= row) if rev else (col <= row)
    tri = jnp.where(allowed, 1.0, 0.0).astype(BF16)
    mask_bias = jnp.where(allowed, 0.0, -jnp.inf)
    acc3 = _dot(tri, jnp.concatenate(_split3(_log_sigmoid(pre)), axis=1))
    bcum = acc3[:, :W] + acc3[:, W:2 * W] + acc3[:, 2 * W:]
    lane0 = GATE_LANE0 + (2 * H if rev else 0)
    z = pre - pltpu.roll(bcum, W - H, 1)
    onehot = jnp.where(lax.broadcasted_iota(jnp.int32, (8, W), 1)
                       == lax.broadcasted_iota(jnp.int32, (8, W), 0) + lane0, 1.0, 0.0).astype(BF16)
    zr = _dot_nt(onehot, jnp.concatenate(_split3(z), axis=0))
    y_rows = zr[:, :L] + zr[:, L:2 * L] + zr[:, 2 * L:]
    rep = lambda t, n: jnp.concatenate([t] * n, axis=1)
    stack = lambda f: jnp.concatenate([f(hh) for hh in range(H)], axis=0)
    head = lambda t, hh: t[L * hh:L * (hh + 1)]
    ones_blk = jnp.ones((L, W), BF16)
    last = 0 if rev else L - 1

    b_rep = stack(lambda hh: jnp.broadcast_to(bcum[:, lane0 + H + hh:lane0 + H + hh + 1], (L, W)))
    i_rep = stack(lambda hh: jnp.broadcast_to(pre[:, lane0 + hh:lane0 + hh + 1], (L, W)))
    m_in = stack(lambda hh: jnp.broadcast_to(m_scr[hh, 0:1, :], (L, W)))
    g_tot = stack(lambda hh: jnp.broadcast_to(head(b_rep, hh)[last:last + 1, :], (L, W)))
    y = stack(lambda hh: y_rows[hh:hh + 1, :] + mask_bias)
    c = -jnp.maximum(m_in, jnp.broadcast_to(jnp.max(y, axis=1, keepdims=True), (H * L, W)))
    qs = [q_ref[:, dk * hh:dk * (hh + 1)] for hh in range(H)]
    ks = [k_ref[:, dk * hh:dk * (hh + 1)] for hh in range(H)]
    v_aug = [jnp.concatenate([v_ref[:, dv * hh:dv * (hh + 1)], ones_blk], axis=1) for hh in range(H)]
    s = (stack(lambda hh: _dot_nt(qs[hh], ks[hh])) * jnp.exp(y + rep(c, L // W))).astype(BF16)
    c_in = [c_scr[hh] for hh in range(H)]
    tot = (stack(lambda hh: _dot(head(s, hh), v_aug[hh]))
           + rep(jnp.exp(m_in + c), 2) * stack(lambda hh: _dot(qs[hh], c_in[hh].astype(BF16))))
    hout = tot[:, :dv] / jnp.maximum(jnp.abs(tot[:, dv:]), jnp.exp(c - b_rep))
    for hh in range(H):
        h_ref[:, dv * hh:dv * (hh + 1)] = head(hout, hh)

    a = g_tot - b_rep + i_rep
    a_max = stack(lambda hh: jnp.broadcast_to(jnp.max(head(a, hh), axis=0, keepdims=True), (L, W)))
    w = jnp.exp(a - a_max)
    m_new = jnp.maximum(g_tot + m_in, a_max)
    decay = jnp.exp(g_tot + m_in - m_new)
    grow = jnp.exp(a_max - m_new)
    for hh in range(H):
        kw = (ks[hh].astype(F32) * head(w, hh)[:, :dk]).astype(BF16)
        d_c = lax.dot_general(kw, v_aug[hh], (((0,), (0,)), ((), ())),
                              preferred_element_type=F32)
        c_scr[hh] = (rep(head(decay, hh)[:dk], 2) * c_in[hh] + rep(head(grow, hh)[:dk], 2) * d_c)
        m_scr[hh] = head(m_new, hh)[:m_scr.shape[1]]


def _mlstm_kernel(qf_ref, kf_ref, vf_ref, gf_ref, qb_ref, kb_ref, vb_ref, gb_ref, bias_ref, c0_ref, m0_ref,
                  hf_ref, hb_ref, cf_ref, mf_ref, c_scr, m_scr, *, nchunks):
    n = pl.program_id(1)

    @pl.when(n == 0)
    def _():
        c_scr[...] = c0_ref[...]
        m_scr[...] = m0_ref[...]

    _mlstm_chunk(qf_ref, kf_ref, vf_ref, gf_ref, bias_ref, hf_ref, c_scr.at[0], m_scr.at[0], rev=False)
    _mlstm_chunk(qb_ref, kb_ref, vb_ref, gb_ref, bias_ref, hb_ref, c_scr.at[1], m_scr.at[1], rev=True)

    @pl.when(n == nchunks - 1)
    def _():
        cf_ref[...] = c_scr[...]
        mf_ref[...] = m_scr[...]


def _mlstm(q, k, v, g, bias, c0, m0, *, B, L):
    M = q.shape[0]
    N = M // B // L
    H = MLSTM_HEADS
    fwd = lambda b, n: (b * N + n, 0)
    bwd = lambda b, n: (b * N + N - 1 - n, 0)
    data = lambda im: [pl.BlockSpec((L, MQW), im), pl.BlockSpec((L, MQW), im),
                       pl.BlockSpec((L, MVW), im), pl.BlockSpec((L, V7X_LANES), im)]
    cspec = pl.BlockSpec((None, 2, H, MLSTM_QK, 256), lambda b, n: (b, 0, 0, 0, 0))
    mspec = pl.BlockSpec((None, 2, H, 8, V7X_LANES), lambda b, n: (b, 0, 0, 0, 0))
    return pl.pallas_call(
        functools.partial(_mlstm_kernel, nchunks=N),
        grid=(B, N),
        in_specs=data(fwd) + data(bwd) + [_resident(bias.shape), cspec, mspec],
        out_specs=[pl.BlockSpec((L, MVW), fwd), pl.BlockSpec((L, MVW), bwd), cspec, mspec],
        out_shape=[jax.ShapeDtypeStruct((M, MVW), F32), jax.ShapeDtypeStruct((M, MVW), F32),
                   jax.ShapeDtypeStruct(c0.shape, F32), jax.ShapeDtypeStruct(m0.shape, F32)],
        scratch_shapes=[pltpu.VMEM((2, H, MLSTM_QK, 256), F32), pltpu.VMEM((2, H, 8, V7X_LANES), F32)],
        compiler_params=_cparams("parallel", "arbitrary"),
        name="mlstm",
    )(q, k, v, g, q, k, v, g, bias, c0, m0)


def _outproj_kernel(ya_ref, ys_ref, hf_ref, hb_ref, mo_ref, gh_ref, w_ref, x_ref, gt_ref,
                    sh_ref, sc_ref, g2_ref, o_ref, h2_ref, *, tn):
    dv = MLSTM_V
    parts = []
    for hh in range(MLSTM_HEADS):
        sl = slice(dv * hh, dv * (hh + 1))
        hn = _rms(hf_ref[:, sl] + hb_ref[:, sl]) * gh_ref[:, sl]
        parts.append((hn * jax.nn.sigmoid(mo_ref[:, sl].astype(F32))).astype(BF16))
    y = jnp.concatenate([ya_ref[...], ys_ref[...]] + parts, axis=1)
    D = o_ref.shape[1]
    ssq = None
    for c in range(D // tn):
        cs = slice(tn * c, tn * (c + 1))
        xn = x_ref[:, cs] + gt_ref[:, cs] * _dot(y, w_ref[:, cs])
        o_ref[:, cs] = xn
        part = jnp.sum(xn * xn, axis=-1, keepdims=True)
        ssq = part if ssq is None else ssq + part
    inv = lax.rsqrt(ssq * (1.0 / D) + NORM_EPS)
    for c in range(D // tn):
        cs = slice(tn * c, tn * (c + 1))
        h2 = (o_ref[:, cs] * inv * g2_ref[:, cs]) * (1.0 + sc_ref[:, cs]) + sh_ref[:, cs]
        h2_ref[:, cs] = h2.astype(BF16)


def _outproj(ya, ys, hf, hb, mo, gh, w_out, x, gt, sh, sc, g2, *, l, tm):
    M, D = x.shape
    R = M // gt.shape[0]
    row = lambda i: (i, 0)
    modv = pl.BlockSpec((None, 1, D), lambda i: (i // (R // tm), 0, 0))
    return pl.pallas_call(
        functools.partial(_outproj_kernel, tn=min(512, D)),
        grid=(M // tm,),
        in_specs=[pl.BlockSpec((tm, MLA_VW), row), pl.BlockSpec((tm, SWA_QW), row),
                  pl.BlockSpec((tm, MVW), row), pl.BlockSpec((tm, MVW), row),
                  pl.BlockSpec((tm, MVW), row), _resident(gh.shape), _layer_resident(w_out, l),
                  pl.BlockSpec((tm, D), row), modv, modv, modv, _resident((1, D))],
        out_specs=[pl.BlockSpec((tm, D), row), pl.BlockSpec((tm, D), row)],
        out_shape=[jax.ShapeDtypeStruct((M, D), F32), jax.ShapeDtypeStruct((M, D), BF16)],
        compiler_params=_cparams("parallel"),
        name="outproj",
    )(ya, ys, hf, hb, mo, gh, w_out, x, gt, sh, sc, g2)


def _ffn_kernel(x_ref, h_ref, gt_ref, gfin_ref, w1_ref, w2_ref, o_ref, *, nf, final):
    f = pl.program_id(1)
    u = jnp.maximum(_dot(h_ref[...], w1_ref[...]), 0.0)
    a = (u * u).astype(BF16)

    @pl.when(f == 0)
    def _():
        o_ref[...] = _dot(a, w2_ref[...])

    @pl.when(f != 0)
    def _():
        o_ref[...] += _dot(a, w2_ref[...])

    @pl.when(f == nf - 1)
    def _():
        D = o_ref.shape[1]
        tn = min(512, D)
        ssq = None
        for c in range(D // tn):
            cs = slice(tn * c, tn * (c + 1))
            y = x_ref[:, cs] + gt_ref[:, cs] * o_ref[:, cs]
            o_ref[:, cs] = y
            if final:
                part = jnp.sum(y * y, axis=-1, keepdims=True)
                ssq = part if ssq is None else ssq + part
        if final:
            inv = lax.rsqrt(ssq * (1.0 / D) + NORM_EPS)
            for c in range(D // tn):
                cs = slice(tn * c, tn * (c + 1))
                o_ref[:, cs] = o_ref[:, cs] * inv * gfin_ref[:, cs]


def _ffn(x, h2, gt, gfin, w1, w2, *, l, tm, tf, final):
    M, D = x.shape
    FF = w1.shape[2]
    R = M // gt.shape[0]
    nf = FF // tf
    rows = lambda i, f: (i, 0)
    return pl.pallas_call(
        functools.partial(_ffn_kernel, nf=nf, final=final),
        grid=(M // tm, nf),
        in_specs=[pl.BlockSpec((tm, D), rows), pl.BlockSpec((tm, D), rows),
                  pl.BlockSpec((None, 1, D), lambda i, f: (i // (R // tm), 0, 0)), _resident((1, D)),
                  pl.BlockSpec((None, D, tf), lambda i, f: (l, 0, f)),
                  pl.BlockSpec((None, tf, D), lambda i, f: (l, f, 0))],
        out_specs=pl.BlockSpec((tm, D), rows),
        out_shape=jax.ShapeDtypeStruct((M, D), F32),
        compiler_params=_cparams("parallel", "arbitrary"),
        name="ffn",
    )(x, h2, gt, gfin, w1, w2)


def _rope_tables(S):
    half = SWA_HEAD_DIM // 2
    pos = jnp.arange(S)
    inv = ROPE_BASE ** (-jnp.arange(0, half, 2, dtype=F32) / half)
    ar = (pos // GRID_W)[:, None].astype(F32) * inv
    ac = (pos % GRID_W)[:, None].astype(F32) * inv
    ang = jnp.concatenate([ar, ar, ac, ac] * 2, axis=-1)
    first = (jnp.arange(V7X_LANES) % 32) < 16
    sin = jnp.sin(ang)
    return jnp.cos(ang), jnp.where(first, -sin, 0.0), jnp.where(first, 0.0, sin)


def _w_in_segments():
    widths = (MLA_Q_RANK, MLA_KV_RANK, MLA_ROPE, SWA_QW, SWA_KW, SWA_KW, MQW, MQW, MVW, N_GATES, MVW)
    src = [0]
    for w in widths:
        src.append(src[-1] + w)
    segs, dst = [], 0
    for i in (0, 1, 3, 4, 5, 6, 7, 8, 10, 2, 9):
        segs.append((src[i], widths[i], dst))
        dst += widths[i]
    return segs, dst


def _permute_w_in_kernel(w_ref, o_ref):
    segs, end = _w_in_segments()
    for s0, w, d0 in segs:
        o_ref[:, d0:d0 + w] = w_ref[:, s0:s0 + w].astype(BF16)
    o_ref[:, end:] = jnp.zeros((o_ref.shape[0], o_ref.shape[1] - end), BF16)


def _permute_w_in(w_in):
    L, D, N = w_in.shape
    tr = _row_tile(D, 256)
    return pl.pallas_call(
        _permute_w_in_kernel,
        grid=(L, D // tr),
        in_specs=[pl.BlockSpec((None, tr, N), lambda l, i: (l, i, 0))],
        out_specs=pl.BlockSpec((None, tr, C_END), lambda l, i: (l, i, 0)),
        out_shape=jax.ShapeDtypeStruct((L, D, C_END), BF16),
        compiler_params=_cparams("parallel", "parallel"),
        name="permute_w_in",
    )(w_in)


def _permute_w_uq(w_uq):
    L, Rk, _ = w_uq.shape
    w = w_uq.reshape(L, Rk, MLA_HEADS, MLA_NOPE + MLA_ROPE)
    w = jnp.pad(w, ((0, 0), (0, 0), (0, 0), (0, 256 - MLA_NOPE - MLA_ROPE)))
    return w.reshape(L, Rk, MLA_QW).astype(BF16)


def _permute_w_ukv(w_ukv):
    L, Rk, _ = w_ukv.shape
    w = w_ukv.reshape(L, Rk, MLA_HEADS, MLA_NOPE + MLA_V)
    k = w[..., :MLA_NOPE].reshape(L, Rk, MLA_HEADS * MLA_NOPE)
    v = w[..., MLA_NOPE:].reshape(L, Rk, MLA_VW)
    return jnp.concatenate([k, v], axis=-1).astype(BF16)


def _row_tile(rows, want):
    return want if rows % want == 0 else rows


def kernel(x, c, ctx, c_ctx, w_mod, b_mod, g_norm1, g_norm2, w_in, mla_g_q, mla_w_uq, mla_g_kv,
           mla_w_ukv, swa_sink, mlstm_gate_bias, mlstm_g_h, w_out, w_ff1, w_ff2, g_final):
    B, S, D = x.shape
    Lc = ctx.shape[1]
    depth = w_in.shape[0]
    H = MLSTM_HEADS

    rows = -(-(B + 1) // 8) * 8
    c_all = jnp.concatenate([c, c_ctx[None, :], jnp.zeros((rows - B - 1, D), F32)], axis=0)
    mod = _mod_all(c_all, w_mod, b_mod)
    mod6 = mod.reshape(depth, rows, 6, D)

    w_in_p = _permute_w_in(w_in)
    w_uq_p = _permute_w_uq(mla_w_uq)
    w_ukv_p = _permute_w_ukv(mla_w_ukv)
    w_out_b = w_out.astype(BF16)
    w1_b = w_ff1.astype(BF16)
    w2_b = w_ff2.astype(BF16)
    cos, sina, sinb = _rope_tables(S)
    ones_t = jnp.ones((Lc, V7X_LANES), F32)
    zeros_t = jnp.zeros((Lc, V7X_LANES), F32)
    bias_lanes = jnp.pad(mlstm_gate_bias.reshape(depth, 1, N_GATES),
                         ((0, 0), (0, 0), (GATE_LANE0, V7X_LANES - GATE_LANE0 - N_GATES)))
    c_zero = jnp.zeros((B, 2, H, MLSTM_QK, 256), F32)
    m_zero = jnp.zeros((B, 2, H, 8, V7X_LANES), F32)

    tm = _row_tile(S, 512)
    FF = w_ff1.shape[2]
    tm_ffn, tf = _row_tile(S, 512), min(2048, FF)
    tq_mla = tk_mla = _row_tile(S, 2048)
    gfin = g_final[None, :]
    tq_swa = _row_tile(S, 512)
    chunk = _row_tile(S, 256)

    xs = x.reshape(B * S, D)
    xc = ctx.reshape(B * Lc, D)
    for l in range(depth):
        need_ctx = l < depth - 1
        vec = lambda j: mod6[l, :B, j][:, None, :]
        vecc = lambda j: mod6[l, B:B + 1, j][:, None, :]
        g1 = g_norm1[l][None, :]
        g2 = g_norm2[l][None, :]
        gq = mla_g_q[l][None, :]
        gkv = mla_g_kv[l][None, :]
        gh = mlstm_g_h[l].reshape(1, MVW)
        inproj = functools.partial(_inproj, g1=g1, w_ext=w_in_p, gq=gq, wuq=w_uq_p, gkv=gkv,
                                   wukv=w_ukv_p, l=l)
        ffn = functools.partial(_ffn, gfin=gfin, w1=w1_b, w2=w2_b, l=l, tf=tf)
        (qmc, kmc, vmc, qsc, ksc, vsc, mqc, mkc, mvc, moc, gsc) = inproj(
            xc, vecc(0), vecc(1), cos=ones_t, sina=zeros_t, sinb=zeros_t, tm=Lc)
        hfc, hbc, c_st, m_st = _mlstm(mqc, mkc, mvc, gsc, bias_lanes[l], c_zero, m_zero, B=B, L=Lc)

        (qm, km, vm, qs, ks, vs, mq, mk, mv, mo, gs) = inproj(
            xs, vec(0), vec(1), cos=cos, sina=sina, sinb=sinb, tm=tm)
        y_mla = _mla(qm, kmc, vmc, km, vm, B=B, tq=tq_mla, tk=tk_mla)
        y_swa = _swa(swa_sink[l], qs, ksc, vsc, ks, vs, B=B, tq=tq_swa)
        hf, hb, _, _ = _mlstm(mq, mk, mv, gs, bias_lanes[l], c_st, m_st, B=B, L=chunk)
        xs, h2 = _outproj(y_mla, y_swa, hf, hb, mo, gh, w_out_b, xs, vec(2), vec(3), vec(4), g2, l=l, tm=tm)
        xs = ffn(xs, h2, vec(5), tm=tm_ffn, final=not need_ctx)
        if need_ctx:
            yc_mla = _mla(qmc, kmc, vmc, B=B, tq=Lc, tk=Lc)
            yc_swa = _swa(swa_sink[l], qsc, ksc, vsc, B=B, tq=Lc)
            xc, h2c = _outproj(yc_mla, yc_swa, hfc, hbc, moc, gh, w_out_b, xc, vecc(2), vecc(3), vecc(4), g2,
                               l=l, tm=Lc)
            xc = ffn(xc, h2c, vecc(5), tm=B * Lc, tf=min(512, FF), final=False)
    return xs.reshape(B, S, D)
```

```python
import functools
import math

import jax
import jax.numpy as jnp
from jax import lax
from jax.experimental import pallas as pl
from jax.experimental.pallas import tpu as pltpu

F32 = jnp.float32
BF16 = jnp.bfloat16

GRID_W = 64
ROPE_BASE = 10000.0
NORM_EPS = 1e-6
MLA_HEADS = 4
MLA_Q_RANK = 512
MLA_KV_RANK = 256
MLA_NOPE = 128
MLA_ROPE = 64
MLA_V = 128
SWA_HEADS = 16
SWA_KV_HEADS = 2
SWA_HEAD_DIM = 64
WINDOW = 128
MLSTM_HEADS = 4
MLSTM_QK = 64
MLSTM_V = 128
N_GATES = 4 * MLSTM_HEADS

V7X_LANES = 128
V7X_VMEM_LIMIT_BYTES = 63 * 1024 * 1024

LOG2E = math.log2(math.e)
MLA_QSCALE = (MLA_NOPE + MLA_ROPE) ** -0.5 * LOG2E
SWA_QSCALE = SWA_HEAD_DIM ** -0.5 * LOG2E
MLSTM_QSCALE = MLSTM_QK ** -0.5
NEG_BIG = -1e30

C_ZQ = 0
C_ZKV = C_ZQ + MLA_Q_RANK
C_SQ = C_ZKV + MLA_KV_RANK
C_SK = C_SQ + SWA_HEADS * SWA_HEAD_DIM
C_SV = C_SK + SWA_KV_HEADS * SWA_HEAD_DIM
C_MQ = C_SV + SWA_KV_HEADS * SWA_HEAD_DIM
C_MK = C_MQ + MLSTM_HEADS * MLSTM_QK
C_MV = C_MK + MLSTM_HEADS * MLSTM_QK
C_MO = C_MV + MLSTM_HEADS * MLSTM_V
C_SMALL = C_MO + MLSTM_HEADS * MLSTM_V
C_END = C_SMALL + V7X_LANES
GATE_LANE0 = MLA_ROPE

MLA_QW = MLA_HEADS * 256
MLA_VW = MLA_HEADS * MLA_V
SWA_QW = SWA_HEADS * SWA_HEAD_DIM
SWA_KW = SWA_KV_HEADS * SWA_HEAD_DIM
MQW = MLSTM_HEADS * MLSTM_QK
MVW = MLSTM_HEADS * MLSTM_V


def _cparams(*sem):
    return pltpu.CompilerParams(dimension_semantics=sem, vmem_limit_bytes=V7X_VMEM_LIMIT_BYTES)


def _resident(shape):
    nd = len(shape)
    return pl.BlockSpec(shape, lambda *_: (0,) * nd, pipeline_mode=pl.Buffered(1))


def _layer_resident(stacked, l):
    nd = stacked.ndim - 1
    return pl.BlockSpec((None,) + stacked.shape[1:], lambda *_: (l,) + (0,) * nd,
                        pipeline_mode=pl.Buffered(1))


def _dot(a, b):
    return jnp.dot(a, b, preferred_element_type=F32)


def _dot_nt(a, b):
    return lax.dot_general(a, b, (((1,), (1,)), ((), ())), preferred_element_type=F32)


def _rms(x):
    return x * lax.rsqrt(jnp.mean(x * x, axis=-1, keepdims=True) + NORM_EPS)


def _mod_kernel(c_ref, w_ref, b_ref, o_ref):
    c = c_ref[...]
    a = c * jax.nn.sigmoid(c)
    w = w_ref[...]
    a_hi, w_hi = a.astype(BF16), w.astype(BF16)
    a_lo = (a - a_hi.astype(F32)).astype(BF16)
    w_lo = (w - w_hi.astype(F32)).astype(BF16)
    o_ref[...] = _dot(a_hi, w_hi) + _dot(a_lo, w_hi) + _dot(a_hi, w_lo) + b_ref[...]


def _mod_all(c_all, w_mod, b_mod):
    L, D, N = w_mod.shape
    R = c_all.shape[0]
    tn = next(t for t in (2048, 1024, N) if N % t == 0)
    return pl.pallas_call(
        _mod_kernel,
        grid=(L, N // tn),
        in_specs=[pl.BlockSpec((R, D), lambda l, j: (0, 0)),
                  pl.BlockSpec((None, D, tn), lambda l, j: (l, 0, j)),
                  pl.BlockSpec((None, 1, tn), lambda l, j: (l, 0, j))],
        out_specs=pl.BlockSpec((None, R, tn), lambda l, j: (l, 0, j)),
        out_shape=jax.ShapeDtypeStruct((L, R, N), F32),
        compiler_params=_cparams("parallel", "parallel"),
        name="mod",
    )(c_all, w_mod, b_mod.reshape(L, 1, N))


def _rope(x, cos, sina, sinb):
    return x * cos + pltpu.roll(x, V7X_LANES - 16, 1) * sina + pltpu.roll(x, 16, 1) * sinb


def _inproj_kernel(x_ref, sh_ref, sc_ref, g1_ref, w_ref, gq_ref, wuq_ref, gkv_ref, wukv_ref,
                   cos_ref, sina_ref, sinb_ref,
                   qm_ref, km_ref, vm_ref, qs_ref, ks_ref, vs_ref, mq_ref, mk_ref, mv_ref, mo_ref,
                   gs_ref):
    tm = x_ref.shape[0]
    h = (_rms(x_ref[...]) * g1_ref[...]) * (1.0 + sc_ref[...]) + sh_ref[...]
    hb = h.astype(BF16)
    cos, sina, sinb = cos_ref[...], sina_ref[...], sinb_ref[...]
    rope = lambda t: _rope(t, cos, sina, sinb)
    proj = lambda c0, c1: _dot(hb, w_ref[:, c0:c1])
    low = lax.broadcasted_iota(jnp.int32, (tm, V7X_LANES), 1) < MLA_ROPE

    small = proj(C_SMALL, C_END)
    gs_ref[...] = small
    k_rope = jnp.where(low, rope(small), 0.0).astype(BF16)

    zqn = (_rms(proj(C_ZQ, C_ZKV)) * gq_ref[...]).astype(BF16)
    qall = _dot(zqn, wuq_ref[...]) * MLA_QSCALE
    for hh in range(MLA_HEADS):
        qa = qall[:, 256 * hh:256 * (hh + 1)]
        qm_ref[:, 256 * hh:256 * hh + 128] = qa[:, :128].astype(BF16)
        qm_ref[:, 256 * hh + 128:256 * (hh + 1)] = jnp.where(low, rope(qa[:, 128:]), 0.0).astype(BF16)

    zkvn = (_rms(proj(C_ZKV, C_SQ)) * gkv_ref[...]).astype(BF16)
    kv = _dot(zkvn, wukv_ref[...])
    for hh in range(MLA_HEADS):
        km_ref[:, 256 * hh:256 * hh + 128] = kv[:, 128 * hh:128 * (hh + 1)].astype(BF16)
        km_ref[:, 256 * hh + 128:256 * (hh + 1)] = k_rope
    vm_ref[...] = kv[:, MLA_HEADS * MLA_NOPE:].astype(BF16)

    for c in range(SWA_QW // 512):
        sq = proj(C_SQ + 512 * c, C_SQ + 512 * (c + 1))
        for p in range(4):
            blk = rope(sq[:, 128 * p:128 * (p + 1)]) * SWA_QSCALE
            qs_ref[:, 512 * c + 128 * p:512 * c + 128 * (p + 1)] = blk.astype(BF16)
    for ref, val in ((ks_ref, rope(proj(C_SK, C_SV))), (vs_ref, proj(C_SV, C_MQ))):
        ref[:, :SWA_KW] = val.astype(BF16)
        ref[:, SWA_KW:] = pltpu.roll(val, SWA_HEAD_DIM, 1).astype(BF16)

    mq_ref[...] = (proj(C_MQ, C_MK) * MLSTM_QSCALE).astype(BF16)
    mk_ref[...] = proj(C_MK, C_MV).astype(BF16)
    mv_ref[...] = proj(C_MV, C_MO).astype(BF16)
    mo_ref[...] = proj(C_MO, C_SMALL).astype(BF16)


def _inproj(x, sh, sc, g1, w_ext, gq, wuq, gkv, wukv, cos, sina, sinb, *, l, tm):
    M, D = x.shape
    S = cos.shape[0]
    R = M // sh.shape[0]
    nt = S // tm
    row = lambda i: (i, 0)
    modv = pl.BlockSpec((None, 1, D), lambda i: (i // (R // tm), 0, 0))
    tab = pl.BlockSpec((tm, V7X_LANES), lambda i: (i % nt, 0))
    widths = (MLA_QW, MLA_QW, MLA_VW, SWA_QW, 2 * SWA_KW, 2 * SWA_KW, MQW, MQW, MVW, MVW)
    out_shape = [jax.ShapeDtypeStruct((M, w), BF16) for w in widths]
    out_shape.append(jax.ShapeDtypeStruct((M, V7X_LANES), F32))
    out_specs = [pl.BlockSpec((tm, w), row) for w in widths] + [pl.BlockSpec((tm, V7X_LANES), row)]
    return pl.pallas_call(
        _inproj_kernel,
        grid=(M // tm,),
        in_specs=[pl.BlockSpec((tm, D), row), modv, modv, _resident((1, D)),
                  _layer_resident(w_ext, l), _resident(gq.shape), _layer_resident(wuq, l),
                  _resident(gkv.shape), _layer_resident(wukv, l), tab, tab, tab],
        out_specs=out_specs,
        out_shape=out_shape,
        compiler_params=_cparams("parallel"),
        name="inproj",
    )(x, sh, sc, g1, w_ext, gq, wuq, gkv, wukv, cos, sina, sinb)


def _mla_kernel(*refs, tk, nk):
    if nk:
        q_ref, k_ref, v_ref, kc_ref, vc_ref, o_ref, m_ref, l_ref, acc_ref = refs
    else:
        q_ref, kc_ref, vc_ref, o_ref, m_ref, l_ref, acc_ref = refs
    q = q_ref[...]
    tq = q.shape[0]
    W = V7X_LANES

    def update(kblk, vblk, first):
        ncol = kblk.shape[0] // W
        s = _dot_nt(q, kblk)
        smax = s[:, 0:W]
        for c in range(1, ncol):
            smax = jnp.maximum(smax, s[:, W * c:W * (c + 1)])
        m_new = jnp.broadcast_to(jnp.max(smax, axis=1, keepdims=True), (tq, W))
        if not first:
            m_prev = m_ref[...]
            m_new = jnp.maximum(m_prev, m_new)
            alpha = jnp.exp2(m_prev - m_new)
        lsum = None
        ps = []
        for c in range(ncol):
            pc = jnp.exp2(s[:, W * c:W * (c + 1)] - m_new)
            lsum = pc if lsum is None else lsum + pc
            ps.append(pc.astype(BF16))
        pv = _dot(jnp.concatenate(ps, axis=1), vblk)
        if first:
            l_ref[...] = lsum
            acc_ref[...] = pv
        else:
            l_ref[...] = alpha * l_ref[...] + lsum
            acc_ref[...] = alpha * acc_ref[...] + pv
        m_ref[...] = m_new

    update(kc_ref[...], vc_ref[...], True)
    if nk:
        def body(j, carry):
            off = pl.multiple_of(j * tk, tk)
            update(k_ref[pl.ds(off, tk), :], v_ref[pl.ds(off, tk), :], False)
            return carry
        lax.fori_loop(0, nk, body, 0)
    l = jnp.sum(l_ref[...], axis=1, keepdims=True)
    o_ref[...] = (acc_ref[...] / l).astype(o_ref.dtype)


def _mla(q, kc, vc, k=None, v=None, *, B, tq, tk):
    M = q.shape[0]
    T = M // B
    Lc = kc.shape[0] // B
    nq = T // tq
    H = MLA_HEADS
    qspec = pl.BlockSpec((tq, 256), lambda b, h, i: (b * nq + i, h))
    cspecs = [pl.BlockSpec((Lc, 256), lambda b, h, i: (b, h)),
              pl.BlockSpec((Lc, MLA_V), lambda b, h, i: (b, h))]
    if k is None:
        nk, in_specs, args = 0, [qspec] + cspecs, (q, kc, vc)
    else:
        nk = T // tk
        in_specs = [qspec, pl.BlockSpec((T, 256), lambda b, h, i: (b, h)),
                    pl.BlockSpec((T, MLA_V), lambda b, h, i: (b, h))] + cspecs
        args = (q, k, v, kc, vc)
    return pl.pallas_call(
        functools.partial(_mla_kernel, tk=tk, nk=nk),
        grid=(B, H, nq),
        in_specs=in_specs,
        out_specs=pl.BlockSpec((tq, MLA_V), lambda b, h, i: (b * nq + i, h)),
        out_shape=jax.ShapeDtypeStruct((M, MLA_VW), BF16),
        scratch_shapes=[pltpu.VMEM((tq, V7X_LANES), F32), pltpu.VMEM((tq, V7X_LANES), F32),
                        pltpu.VMEM((tq, MLA_V), F32)],
        compiler_params=_cparams("parallel", "parallel", "arbitrary"),
        name="mla_latent" if nk else "mla_ctx",
    )(*args)


SWA_SUB = 128
SWA_SPAN = SWA_SUB + 2 * WINDOW


def _swa_kernel(*refs, local, tq, S):
    if local:
        sink_ref, q_ref, k_ref, v_ref, kc_ref, vc_ref, o_ref = refs
    else:
        sink_ref, q_ref, kc_ref, vc_ref, o_ref = refs
    W = V7X_LANES
    d = SWA_HEAD_DIM
    ppg = SWA_HEADS // SWA_KV_HEADS // 2
    Lc = kc_ref.shape[0]

    def variants(x2):
        low = lax.broadcasted_iota(jnp.int32, (x2.shape[0], W), 1) < d
        x, xs = x2[:, :W], x2[:, W:]
        zero = jnp.zeros_like(x)
        return {(0, 0): jnp.where(low, x, zero), (1, 1): jnp.where(low, zero, x),
                (0, 1): jnp.where(low, zero, xs), (1, 0): jnp.where(low, xs, zero)}

    def with_ones(vv):
        def ones_half(t, e):
            mine = (lax.broadcasted_iota(jnp.int32, t.shape, 1) < d) == (e == 0)
            return jnp.where(mine, 1.0, 0.0).astype(t.dtype)
        return {(g, e): jnp.concatenate([t, ones_half(t, e)], axis=1) for (g, e), t in vv.items()}

    kcv = variants(kc_ref[...])
    vcv = with_ones(variants(vc_ref[...]))
    for j in range(tq // SWA_SUB):
        r = slice(SWA_SUB * j, SWA_SUB * (j + 1))
        if local:
            qpos = pl.program_id(1) * tq + SWA_SUB * j
            r0 = pl.multiple_of(jnp.clip(qpos - WINDOW, 0, S - SWA_SPAN), W)
            kwin = variants(k_ref[pl.ds(r0, SWA_SPAN), :])
            vwin = with_ones(variants(v_ref[pl.ds(r0, SWA_SPAN), :]))
            kv = {ge: jnp.concatenate([kcv[ge], kwin[ge]], axis=0) for ge in kwin}
            vv = {ge: jnp.concatenate([vcv[ge], vwin[ge]], axis=0) for ge in vwin}
            rel = (lax.broadcasted_iota(jnp.int32, (SWA_SUB, SWA_SPAN), 1)
                   - lax.broadcasted_iota(jnp.int32, (SWA_SUB, SWA_SPAN), 0)) + (r0 - qpos)
            bias = jnp.where(jnp.abs(rel) <= WINDOW, 0.0, NEG_BIG)
            bias = jnp.concatenate([bias] * ppg, axis=0)
        for g in range(SWA_KV_HEADS):
            cols = [slice(W * (ppg * g + pp), W * (ppg * g + pp + 1)) for pp in range(ppg)]
            qs = jnp.concatenate([q_ref[r, c] for c in cols], axis=0)
            ksrc, vsrc = (kv, vv) if local else (kcv, vcv)
            nkeys = ksrc[g, 0].shape[0]
            s = _dot_nt(qs, jnp.concatenate([ksrc[g, 0], ksrc[g, 1]], axis=0))
            ps, sink_terms = [], []
            for e in range(2):
                sink = jnp.concatenate(
                    [jnp.full((SWA_SUB, W), sink_ref[2 * (ppg * g + pp) + e] * LOG2E, F32)
                     for pp in range(ppg)], axis=0)
                c0 = nkeys * e
                blocks = [s[:, c0 + W * c:c0 + W * (c + 1)] for c in range(Lc // W)]
                if local:
                    blocks += [s[:, c0 + Lc + W * c:c0 + Lc + W * (c + 1)] + bias[:, W * c:W * (c + 1)]
                               for c in range(SWA_SPAN // W)]
                mx = blocks[0]
                for blk in blocks[1:]:
                    mx = jnp.maximum(mx, blk)
                m = jnp.maximum(sink, jnp.broadcast_to(jnp.max(mx, axis=1, keepdims=True), mx.shape))
                ps += [jnp.exp2(blk - m).astype(BF16) for blk in blocks]
                sink_terms.append(jnp.exp2(sink - m))
            out = _dot(jnp.concatenate(ps, axis=1),
                       jnp.concatenate([vsrc[g, 0], vsrc[g, 1]], axis=0))
            first_half = lax.broadcasted_iota(jnp.int32, sink_terms[0].shape, 1) < d
            pair_out = out[:, :W] / (out[:, W:] + jnp.where(first_half, sink_terms[0], sink_terms[1]))
            for pp, c in enumerate(cols):
                o_ref[r, c] = pair_out[SWA_SUB * pp:SWA_SUB * (pp + 1)].astype(o_ref.dtype)


def _swa(sink, q, kc, vc, k=None, v=None, *, B, tq):
    M = q.shape[0]
    T = M // B
    Lc = kc.shape[0] // B
    local = k is not None
    if not local:
        tq = T
    nq = T // tq
    qspec = pl.BlockSpec((tq, SWA_QW), lambda b, i: (b * nq + i, 0))
    cspec = pl.BlockSpec((Lc, 2 * SWA_KW), lambda b, i: (b, 0))
    in_specs = [pl.BlockSpec(memory_space=pltpu.SMEM), qspec]
    args = [sink, q]
    if local:
        in_specs += [pl.BlockSpec((T, 2 * SWA_KW), lambda b, i: (b, 0))] * 2
        args += [k, v]
    in_specs += [cspec, cspec]
    args += [kc, vc]
    return pl.pallas_call(
        functools.partial(_swa_kernel, local=local, tq=tq, S=T),
        grid=(B, nq),
        in_specs=in_specs,
        out_specs=qspec,
        out_shape=jax.ShapeDtypeStruct((M, SWA_QW), BF16),
        compiler_params=_cparams("parallel", "parallel"),
        name="swa_latent" if local else "swa_ctx",
    )(*args)


def _log_sigmoid(x):
    return jnp.minimum(x, 0.0) - jnp.log1p(jnp.exp(-jnp.abs(x)))


def _split3(x):
    hi = x.astype(BF16)
    r1 = x - hi.astype(F32)
    mid = r1.astype(BF16)
    return hi, mid, (r1 - mid.astype(F32)).astype(BF16)


def _mlstm_chunk(q_ref, k_ref, v_ref, g_ref, bias_ref, h_ref, c_scr, m_scr, *, rev):
    L = q_ref.shape[0]
    W = V7X_LANES
    H = MLSTM_HEADS
    dk, dv = MLSTM_QK, MLSTM_V
    pre = g_ref[...] + bias_ref[...]
    row = lax.broadcasted_iota(jnp.int32, (L, L), 0)
    col = lax.broadcasted_iota(jnp.int32, (L, L), 1)
    allowed = (col >= row) if rev else (col <= row)
    tri = jnp.where(allowed, 1.0, 0.0).astype(BF16)
    mask_bias = jnp.where(allowed, 0.0, -jnp.inf)
    acc3 = _dot(tri, jnp.concatenate(_split3(_log_sigmoid(pre)), axis=1))
    bcum = acc3[:, :W] + acc3[:, W:2 * W] + acc3[:, 2 * W:]
    lane0 = GATE_LANE0 + (2 * H if rev else 0)
    z = pre - pltpu.roll(bcum, W - H, 1)
    onehot = jnp.where(lax.broadcasted_iota(jnp.int32, (8, W), 1)
                       == lax.broadcasted_iota(jnp.int32, (8, W), 0) + lane0, 1.0, 0.0).astype(BF16)
    zr = _dot_nt(onehot, jnp.concatenate(_split3(z), axis=0))
    y_rows = zr[:, :L] + zr[:, L:2 * L] + zr[:, 2 * L:]
    rep = lambda t, n: jnp.concatenate([t] * n, axis=1)
    stack = lambda f: jnp.concatenate([f(hh) for hh in range(H)], axis=0)
    head = lambda t, hh: t[L * hh:L * (hh + 1)]
    ones_blk = jnp.ones((L, W), BF16)
    last = 0 if rev else L - 1

    b_rep = stack(lambda hh: jnp.broadcast_to(bcum[:, lane0 + H + hh:lane0 + H + hh + 1], (L, W)))
    i_rep = stack(lambda hh: jnp.broadcast_to(pre[:, lane0 + hh:lane0 + hh + 1], (L, W)))
    m_in = stack(lambda hh: jnp.broadcast_to(m_scr[hh, 0:1, :], (L, W)))
    g_tot = stack(lambda hh: jnp.broadcast_to(head(b_rep, hh)[last:last + 1, :], (L, W)))
    y = stack(lambda hh: y_rows[hh:hh + 1, :] + mask_bias)
    c = -jnp.maximum(m_in, jnp.broadcast_to(jnp.max(y, axis=1, keepdims=True), (H * L, W)))
    lanes = lax.broadcasted_iota(jnp.int32, (L, H * dk), 1)
    in_head = lambda hh: (lanes >= dk * hh) & (lanes < dk * (hh + 1))
    q_full, k_full = q_ref[...], k_ref[...]
    qm = stack(lambda hh: jnp.where(in_head(hh), q_full, jnp.zeros_like(q_full)))
    v_aug = [jnp.concatenate([v_ref[:, dv * hh:dv * (hh + 1)], ones_blk], axis=1) for hh in range(H)]
    s = (_dot_nt(qm, k_full) * jnp.exp(y + rep(c, L // W))).astype(BF16)
    c_in = c_scr[...]
    tot = (stack(lambda hh: _dot(head(s, hh), v_aug[hh]))
           + rep(jnp.exp(m_in + c), 2) * _dot(qm, c_in.astype(BF16)))
    hout = tot[:, :dv] / jnp.maximum(jnp.abs(tot[:, dv:]), jnp.exp(c - b_rep))
    for hh in range(H):
        h_ref[:, dv * hh:dv * (hh + 1)] = head(hout, hh)

    a = g_tot - b_rep + i_rep
    a_max = stack(lambda hh: jnp.broadcast_to(jnp.max(head(a, hh), axis=0, keepdims=True), (L, W)))
    w = jnp.exp(a - a_max)
    m_new = jnp.maximum(g_tot + m_in, a_max)
    decay = jnp.exp(g_tot + m_in - m_new)
    grow = jnp.exp(a_max - m_new)
    k32 = k_full.astype(F32)
    kw = stack(lambda hh: jnp.where(in_head(hh), k32 * rep(head(w, hh), H * dk // W), 0.0)).astype(BF16)
    d_c = lax.dot_general(kw, jnp.concatenate(v_aug, axis=0), (((0,), (0,)), ((), ())),
                          preferred_element_type=F32)
    per_head_rows = lambda t: rep(jnp.concatenate([head(t, hh)[:dk] for hh in range(H)], axis=0), 2)
    c_scr[...] = per_head_rows(decay) * c_in + per_head_rows(grow) * d_c
    for hh in range(H):
        m_scr[hh] = head(m_new, hh)[:m_scr.shape[1]]


def _mlstm_kernel(qf_ref, kf_ref, vf_ref, gf_ref, qb_ref, kb_ref, vb_ref, gb_ref, bias_ref, c0_ref, m0_ref,
                  hf_ref, hb_ref, cf_ref, mf_ref, c_scr, m_scr, *, nchunks):
    n = pl.program_id(1)

    @pl.when(n == 0)
    def _():
        c_scr[...] = c0_ref[...]
        m_scr[...] = m0_ref[...]

    _mlstm_chunk(qf_ref, kf_ref, vf_ref, gf_ref, bias_ref, hf_ref, c_scr.at[0], m_scr.at[0], rev=False)
    _mlstm_chunk(qb_ref, kb_ref, vb_ref, gb_ref, bias_ref, hb_ref, c_scr.at[1], m_scr.at[1], rev=True)

    @pl.when(n == nchunks - 1)
    def _():
        cf_ref[...] = c_scr[...]
        mf_ref[...] = m_scr[...]


def _mlstm(q, k, v, g, bias, c0, m0, *, B, L):
    M = q.shape[0]
    N = M // B // L
    H = MLSTM_HEADS
    fwd = lambda b, n: (b * N + n, 0)
    bwd = lambda b, n: (b * N + N - 1 - n, 0)
    data = lambda im: [pl.BlockSpec((L, MQW), im), pl.BlockSpec((L, MQW), im),
                       pl.BlockSpec((L, MVW), im), pl.BlockSpec((L, V7X_LANES), im)]
    cspec = pl.BlockSpec((None, 2, H * MLSTM_QK, 256), lambda b, n: (b, 0, 0, 0))
    mspec = pl.BlockSpec((None, 2, H, 8, V7X_LANES), lambda b, n: (b, 0, 0, 0, 0))
    return pl.pallas_call(
        functools.partial(_mlstm_kernel, nchunks=N),
        grid=(B, N),
        in_specs=data(fwd) + data(bwd) + [_resident(bias.shape), cspec, mspec],
        out_specs=[pl.BlockSpec((L, MVW), fwd), pl.BlockSpec((L, MVW), bwd), cspec, mspec],
        out_shape=[jax.ShapeDtypeStruct((M, MVW), F32), jax.ShapeDtypeStruct((M, MVW), F32),
                   jax.ShapeDtypeStruct(c0.shape, F32), jax.ShapeDtypeStruct(m0.shape, F32)],
        scratch_shapes=[pltpu.VMEM((2, H * MLSTM_QK, 256), F32), pltpu.VMEM((2, H, 8, V7X_LANES), F32)],
        compiler_params=_cparams("parallel", "arbitrary"),
        name="mlstm",
    )(q, k, v, g, q, k, v, g, bias, c0, m0)


def _outproj_kernel(ya_ref, ys_ref, hf_ref, hb_ref, mo_ref, gh_ref, w_ref, x_ref, gt_ref,
                    sh_ref, sc_ref, g2_ref, o_ref, h2_ref, xg_scr, *, tn):
    dv = MLSTM_V
    parts = []
    for hh in range(MLSTM_HEADS):
        sl = slice(dv * hh, dv * (hh + 1))
        hn = _rms(hf_ref[:, sl] + hb_ref[:, sl]) * gh_ref[:, sl]
        parts.append((hn * jax.nn.sigmoid(mo_ref[:, sl].astype(F32))).astype(BF16))
    y = jnp.concatenate([ya_ref[...], ys_ref[...]] + parts, axis=1)
    D = o_ref.shape[1]
    ssq = None
    for c in range(D // tn):
        cs = slice(tn * c, tn * (c + 1))
        xn = x_ref[:, cs] + gt_ref[:, cs] * _dot(y, w_ref[:, cs])
        o_ref[:, cs] = xn
        xg_scr[:, cs] = xn * (g2_ref[:, cs] * (1.0 + sc_ref[:, cs]))
        part = jnp.sum(xn * xn, axis=-1, keepdims=True)
        ssq = part if ssq is None else ssq + part
    inv = lax.rsqrt(ssq * (1.0 / D) + NORM_EPS)
    for c in range(D // tn):
        cs = slice(tn * c, tn * (c + 1))
        h2_ref[:, cs] = (xg_scr[:, cs] * inv + sh_ref[:, cs]).astype(BF16)


def _outproj(ya, ys, hf, hb, mo, gh, w_out, x, gt, sh, sc, g2, *, l, tm):
    M, D = x.shape
    R = M // gt.shape[0]
    row = lambda i: (i, 0)
    modv = pl.BlockSpec((None, 1, D), lambda i: (i // (R // tm), 0, 0))
    return pl.pallas_call(
        functools.partial(_outproj_kernel, tn=min(512, D)),
        grid=(M // tm,),
        in_specs=[pl.BlockSpec((tm, MLA_VW), row), pl.BlockSpec((tm, SWA_QW), row),
                  pl.BlockSpec((tm, MVW), row), pl.BlockSpec((tm, MVW), row),
                  pl.BlockSpec((tm, MVW), row), _resident(gh.shape), _layer_resident(w_out, l),
                  pl.BlockSpec((tm, D), row), modv, modv, modv, _resident((1, D))],
        out_specs=[pl.BlockSpec((tm, D), row), pl.BlockSpec((tm, D), row)],
        out_shape=[jax.ShapeDtypeStruct((M, D), F32), jax.ShapeDtypeStruct((M, D), BF16)],
        scratch_shapes=[pltpu.VMEM((tm, D), F32)],
        compiler_params=_cparams("parallel"),
        name="outproj",
    )(ya, ys, hf, hb, mo, gh, w_out, x, gt, sh, sc, g2)


def _ffn_kernel(x_ref, h_ref, gt_ref, gfin_ref, w1_ref, w2_ref, o_ref, *, nf, final):
    f = pl.program_id(1)
    u = jnp.maximum(_dot(h_ref[...], w1_ref[...]), 0.0)
    a = (u * u).astype(BF16)

    @pl.when(f == 0)
    def _():
        o_ref[...] = _dot(a, w2_ref[...])

    @pl.when(f != 0)
    def _():
        o_ref[...] += _dot(a, w2_ref[...])

    @pl.when(f == nf - 1)
    def _():
        D = o_ref.shape[1]
        tn = min(512, D)
        ssq = None
        for c in range(D // tn):
            cs = slice(tn * c, tn * (c + 1))
            y = x_ref[:, cs] + gt_ref[:, cs] * o_ref[:, cs]
            o_ref[:, cs] = y
            if final:
                part = jnp.sum(y * y, axis=-1, keepdims=True)
                ssq = part if ssq is None else ssq + part
        if final:
            inv = lax.rsqrt(ssq * (1.0 / D) + NORM_EPS)
            for c in range(D // tn):
                cs = slice(tn * c, tn * (c + 1))
                o_ref[:, cs] = o_ref[:, cs] * inv * gfin_ref[:, cs]


def _ffn(x, h2, gt, gfin, w1, w2, *, l, tm, tf, final):
    M, D = x.shape
    FF = w1.shape[2]
    R = M // gt.shape[0]
    nf = FF // tf
    rows = lambda i, f: (i, 0)
    return pl.pallas_call(
        functools.partial(_ffn_kernel, nf=nf, final=final),
        grid=(M // tm, nf),
        in_specs=[pl.BlockSpec((tm, D), rows), pl.BlockSpec((tm, D), rows),
                  pl.BlockSpec((None, 1, D), lambda i, f: (i // (R // tm), 0, 0)), _resident((1, D)),
                  pl.BlockSpec((None, D, tf), lambda i, f: (l, 0, f)),
                  pl.BlockSpec((None, tf, D), lambda i, f: (l, f, 0))],
        out_specs=pl.BlockSpec((tm, D), rows),
        out_shape=jax.ShapeDtypeStruct((M, D), F32),
        compiler_params=_cparams("parallel", "arbitrary"),
        name="ffn",
    )(x, h2, gt, gfin, w1, w2)


def _rope_tables(S):
    half = SWA_HEAD_DIM // 2
    pos = jnp.arange(S)
    inv = ROPE_BASE ** (-jnp.arange(0, half, 2, dtype=F32) / half)
    ar = (pos // GRID_W)[:, None].astype(F32) * inv
    ac = (pos % GRID_W)[:, None].astype(F32) * inv
    ang = jnp.concatenate([ar, ar, ac, ac] * 2, axis=-1)
    first = (jnp.arange(V7X_LANES) % 32) < 16
    sin = jnp.sin(ang)
    return jnp.cos(ang), jnp.where(first, -sin, 0.0), jnp.where(first, 0.0, sin)


def _w_in_segments():
    widths = (MLA_Q_RANK, MLA_KV_RANK, MLA_ROPE, SWA_QW, SWA_KW, SWA_KW, MQW, MQW, MVW, N_GATES, MVW)
    src = [0]
    for w in widths:
        src.append(src[-1] + w)
    segs, dst = [], 0
    for i in (0, 1, 3, 4, 5, 6, 7, 8, 10, 2, 9):
        segs.append((src[i], widths[i], dst))
        dst += widths[i]
    return segs, dst


def _permute_w_in_kernel(w_ref, o_ref):
    segs, end = _w_in_segments()
    for s0, w, d0 in segs:
        o_ref[:, d0:d0 + w] = w_ref[:, s0:s0 + w].astype(BF16)
    o_ref[:, end:] = jnp.zeros((o_ref.shape[0], o_ref.shape[1] - end), BF16)


def _permute_w_in(w_in):
    L, D, N = w_in.shape
    tr = _row_tile(D, 256)
    return pl.pallas_call(
        _permute_w_in_kernel,
        grid=(L, D // tr),
        in_specs=[pl.BlockSpec((None, tr, N), lambda l, i: (l, i, 0))],
        out_specs=pl.BlockSpec((None, tr, C_END), lambda l, i: (l, i, 0)),
        out_shape=jax.ShapeDtypeStruct((L, D, C_END), BF16),
        compiler_params=_cparams("parallel", "parallel"),
        name="permute_w_in",
    )(w_in)


def _permute_w_uq(w_uq):
    L, Rk, _ = w_uq.shape
    w = w_uq.reshape(L, Rk, MLA_HEADS, MLA_NOPE + MLA_ROPE)
    w = jnp.pad(w, ((0, 0), (0, 0), (0, 0), (0, 256 - MLA_NOPE - MLA_ROPE)))
    return w.reshape(L, Rk, MLA_QW).astype(BF16)


def _permute_w_ukv(w_ukv):
    L, Rk, _ = w_ukv.shape
    w = w_ukv.reshape(L, Rk, MLA_HEADS, MLA_NOPE + MLA_V)
    k = w[..., :MLA_NOPE].reshape(L, Rk, MLA_HEADS * MLA_NOPE)
    v = w[..., MLA_NOPE:].reshape(L, Rk, MLA_VW)
    return jnp.concatenate([k, v], axis=-1).astype(BF16)


def _row_tile(rows, want):
    return want if rows % want == 0 else rows


def kernel(x, c, ctx, c_ctx, w_mod, b_mod, g_norm1, g_norm2, w_in, mla_g_q, mla_w_uq, mla_g_kv,
           mla_w_ukv, swa_sink, mlstm_gate_bias, mlstm_g_h, w_out, w_ff1, w_ff2, g_final):
    B, S, D = x.shape
    Lc = ctx.shape[1]
    depth = w_in.shape[0]
    H = MLSTM_HEADS

    rows = -(-(B + 1) // 8) * 8
    c_all = jnp.concatenate([c, c_ctx[None, :], jnp.zeros((rows - B - 1, D), F32)], axis=0)
    mod = _mod_all(c_all, w_mod, b_mod)
    mod6 = mod.reshape(depth, rows, 6, D)

    w_in_p = _permute_w_in(w_in)
    w_uq_p = _permute_w_uq(mla_w_uq)
    w_ukv_p = _permute_w_ukv(mla_w_ukv)
    w_out_b = w_out.astype(BF16)
    w1_b = w_ff1.astype(BF16)
    w2_b = w_ff2.astype(BF16)
    cos, sina, sinb = _rope_tables(S)
    ones_t = jnp.ones((Lc, V7X_LANES), F32)
    zeros_t = jnp.zeros((Lc, V7X_LANES), F32)
    bias_lanes = jnp.pad(mlstm_gate_bias.reshape(depth, 1, N_GATES),
                         ((0, 0), (0, 0), (GATE_LANE0, V7X_LANES - GATE_LANE0 - N_GATES)))
    c_zero = jnp.zeros((B, 2, H * MLSTM_QK, 256), F32)
    m_zero = jnp.zeros((B, 2, H, 8, V7X_LANES), F32)

    tm = _row_tile(S, 512)
    FF = w_ff1.shape[2]
    tm_ffn, tf = _row_tile(S, 512), min(2048, FF)
    tq_mla = tk_mla = _row_tile(S, 2048)
    gfin = g_final[None, :]
    tq_swa = _row_tile(S, 512)
    chunk = _row_tile(S, 256)

    xs = x.reshape(B * S, D)
    xc = ctx.reshape(B * Lc, D)
    for l in range(depth):
        need_ctx = l < depth - 1
        vec = lambda j: mod6[l, :B, j][:, None, :]
        vecc = lambda j: mod6[l, B:B + 1, j][:, None, :]
        g1 = g_norm1[l][None, :]
        g2 = g_norm2[l][None, :]
        gq = mla_g_q[l][None, :]
        gkv = mla_g_kv[l][None, :]
        gh = mlstm_g_h[l].reshape(1, MVW)
        inproj = functools.partial(_inproj, g1=g1, w_ext=w_in_p, gq=gq, wuq=w_uq_p, gkv=gkv,
                                   wukv=w_ukv_p, l=l)
        ffn = functools.partial(_ffn, gfin=gfin, w1=w1_b, w2=w2_b, l=l, tf=tf)
        (qmc, kmc, vmc, qsc, ksc, vsc, mqc, mkc, mvc, moc, gsc) = inproj(
            xc, vecc(0), vecc(1), cos=ones_t, sina=zeros_t, sinb=zeros_t, tm=Lc)
        hfc, hbc, c_st, m_st = _mlstm(mqc, mkc, mvc, gsc, bias_lanes[l], c_zero, m_zero, B=B, L=Lc)

        (qm, km, vm, qs, ks, vs, mq, mk, mv, mo, gs) = inproj(
            xs, vec(0), vec(1), cos=cos, sina=sina, sinb=sinb, tm=tm)
        y_mla = _mla(qm, kmc, vmc, km, vm, B=B, tq=tq_mla, tk=tk_mla)
        y_swa = _swa(swa_sink[l], qs, ksc, vsc, ks, vs, B=B, tq=tq_swa)
        hf, hb, _, _ = _mlstm(mq, mk, mv, gs, bias_lanes[l], c_st, m_st, B=B, L=chunk)
        xs, h2 = _outproj(y_mla, y_swa, hf, hb, mo, gh, w_out_b, xs, vec(2), vec(3), vec(4), g2, l=l, tm=tm)
        xs = ffn(xs, h2, vec(5), tm=tm_ffn, final=not need_ctx)
        if need_ctx:
            yc_mla = _mla(qmc, kmc, vmc, B=B, tq=Lc, tk=Lc)
            yc_swa = _swa(swa_sink[l], qsc, ksc, vsc, B=B, tq=Lc)
            xc, h2c = _outproj(yc_mla, yc_swa, hfc, hbc, moc, gh, w_out_b, xc, vecc(2), vecc(3), vecc(4), g2,
                               l=l, tm=Lc)
            xc = ffn(xc, h2c, vecc(5), tm=B * Lc, tf=min(512, FF), final=False)
    return xs.reshape(B, S, D)
```

```python
import functools
import math

import jax
import jax.numpy as jnp
from jax import lax
from jax.experimental import pallas as pl
from jax.experimental.pallas import tpu as pltpu

F32 = jnp.float32
BF16 = jnp.bfloat16

GRID_W = 64
ROPE_BASE = 10000.0
NORM_EPS = 1e-6
MLA_HEADS = 4
MLA_Q_RANK = 512
MLA_KV_RANK = 256
MLA_NOPE = 128
MLA_ROPE = 64
MLA_V = 128
SWA_HEADS = 16
SWA_KV_HEADS = 2
SWA_HEAD_DIM = 64
WINDOW = 128
MLSTM_HEADS = 4
MLSTM_QK = 64
MLSTM_V = 128
N_GATES = 4 * MLSTM_HEADS

V7X_LANES = 128
V7X_VMEM_LIMIT_BYTES = 63 * 1024 * 1024

LOG2E = math.log2(math.e)
MLA_QSCALE = (MLA_NOPE + MLA_ROPE) ** -0.5 * LOG2E
SWA_QSCALE = SWA_HEAD_DIM ** -0.5 * LOG2E
MLSTM_QSCALE = MLSTM_QK ** -0.5
NEG_BIG = -1e30

C_ZQ = 0
C_ZKV = C_ZQ + MLA_Q_RANK
C_SQ = C_ZKV + MLA_KV_RANK
C_SK = C_SQ + SWA_HEADS * SWA_HEAD_DIM
C_SV = C_SK + SWA_KV_HEADS * SWA_HEAD_DIM
C_MQ = C_SV + SWA_KV_HEADS * SWA_HEAD_DIM
C_MK = C_MQ + MLSTM_HEADS * MLSTM_QK
C_MV = C_MK + MLSTM_HEADS * MLSTM_QK
C_MO = C_MV + MLSTM_HEADS * MLSTM_V
C_SMALL = C_MO + MLSTM_HEADS * MLSTM_V
C_END = C_SMALL + V7X_LANES
GATE_LANE0 = MLA_ROPE

MLA_QW = MLA_HEADS * 256
MLA_VW = MLA_HEADS * MLA_V
SWA_QW = SWA_HEADS * SWA_HEAD_DIM
SWA_KW = SWA_KV_HEADS * SWA_HEAD_DIM
MQW = MLSTM_HEADS * MLSTM_QK
MVW = MLSTM_HEADS * MLSTM_V


def _cparams(*sem):
    return pltpu.CompilerParams(dimension_semantics=sem, vmem_limit_bytes=V7X_VMEM_LIMIT_BYTES)


def _resident(shape):
    nd = len(shape)
    return pl.BlockSpec(shape, lambda *_: (0,) * nd, pipeline_mode=pl.Buffered(1))


def _layer_resident(stacked, l):
    nd = stacked.ndim - 1
    return pl.BlockSpec((None,) + stacked.shape[1:], lambda *_: (l,) + (0,) * nd,
                        pipeline_mode=pl.Buffered(1))


def _dot(a, b):
    return jnp.dot(a, b, preferred_element_type=F32)


def _dot_nt(a, b):
    return lax.dot_general(a, b, (((1,), (1,)), ((), ())), preferred_element_type=F32)


def _rms(x):
    return x * lax.rsqrt(jnp.mean(x * x, axis=-1, keepdims=True) + NORM_EPS)


def _mod_kernel(c_ref, w_ref, b_ref, o_ref):
    c = c_ref[...]
    a = c * jax.nn.sigmoid(c)
    w = w_ref[...]
    a_hi, w_hi = a.astype(BF16), w.astype(BF16)
    a_lo = (a - a_hi.astype(F32)).astype(BF16)
    w_lo = (w - w_hi.astype(F32)).astype(BF16)
    o_ref[...] = _dot(a_hi, w_hi) + _dot(a_lo, w_hi) + _dot(a_hi, w_lo) + b_ref[...]


def _mod_all(c_all, w_mod, b_mod):
    L, D, N = w_mod.shape
    R = c_all.shape[0]
    tn = next(t for t in (2048, 1024, N) if N % t == 0)
    return pl.pallas_call(
        _mod_kernel,
        grid=(L, N // tn),
        in_specs=[pl.BlockSpec((R, D), lambda l, j: (0, 0)),
                  pl.BlockSpec((None, D, tn), lambda l, j: (l, 0, j)),
                  pl.BlockSpec((None, 1, tn), lambda l, j: (l, 0, j))],
        out_specs=pl.BlockSpec((None, R, tn), lambda l, j: (l, 0, j)),
        out_shape=jax.ShapeDtypeStruct((L, R, N), F32),
        compiler_params=_cparams("parallel", "parallel"),
        name="mod",
    )(c_all, w_mod, b_mod.reshape(L, 1, N))


def _rope(x, cos, sina, sinb):
    return x * cos + pltpu.roll(x, V7X_LANES - 16, 1) * sina + pltpu.roll(x, 16, 1) * sinb


def _inproj_kernel(x_ref, sh_ref, sc_ref, g1_ref, w_ref, gq_ref, wuq_ref, gkv_ref, wukv_ref,
                   cos_ref, sina_ref, sinb_ref,
                   qm_ref, km_ref, vm_ref, qs_ref, ks_ref, vs_ref, mq_ref, mk_ref, mv_ref, mo_ref,
                   gs_ref):
    tm = x_ref.shape[0]
    h = (_rms(x_ref[...]) * g1_ref[...]) * (1.0 + sc_ref[...]) + sh_ref[...]
    hb = h.astype(BF16)
    cos, sina, sinb = cos_ref[...], sina_ref[...], sinb_ref[...]
    rope = lambda t: _rope(t, cos, sina, sinb)
    proj = lambda c0, c1: _dot(hb, w_ref[:, c0:c1])
    low = lax.broadcasted_iota(jnp.int32, (tm, V7X_LANES), 1) < MLA_ROPE

    small = proj(C_SMALL, C_END)
    gs_ref[...] = small
    k_rope = jnp.where(low, rope(small), 0.0).astype(BF16)

    zqn = (_rms(proj(C_ZQ, C_ZKV)) * gq_ref[...]).astype(BF16)
    qall = _dot(zqn, wuq_ref[...]) * MLA_QSCALE
    for hh in range(MLA_HEADS):
        qa = qall[:, 256 * hh:256 * (hh + 1)]
        qm_ref[:, 256 * hh:256 * hh + 128] = qa[:, :128].astype(BF16)
        qm_ref[:, 256 * hh + 128:256 * (hh + 1)] = jnp.where(low, rope(qa[:, 128:]), 0.0).astype(BF16)

    zkvn = (_rms(proj(C_ZKV, C_SQ)) * gkv_ref[...]).astype(BF16)
    kv = _dot(zkvn, wukv_ref[...])
    for hh in range(MLA_HEADS):
        km_ref[:, 256 * hh:256 * hh + 128] = kv[:, 128 * hh:128 * (hh + 1)].astype(BF16)
        km_ref[:, 256 * hh + 128:256 * (hh + 1)] = k_rope
    vm_ref[...] = kv[:, MLA_HEADS * MLA_NOPE:].astype(BF16)

    for c in range(SWA_QW // 512):
        sq = proj(C_SQ + 512 * c, C_SQ + 512 * (c + 1))
        for p in range(4):
            blk = rope(sq[:, 128 * p:128 * (p + 1)]) * SWA_QSCALE
            qs_ref[:, 512 * c + 128 * p:512 * c + 128 * (p + 1)] = blk.astype(BF16)
    for ref, val in ((ks_ref, rope(proj(C_SK, C_SV))), (vs_ref, proj(C_SV, C_MQ))):
        ref[:, :SWA_KW] = val.astype(BF16)
        ref[:, SWA_KW:] = pltpu.roll(val, SWA_HEAD_DIM, 1).astype(BF16)

    mq_ref[...] = (proj(C_MQ, C_MK) * MLSTM_QSCALE).astype(BF16)
    mk_ref[...] = proj(C_MK, C_MV).astype(BF16)
    mv_ref[...] = proj(C_MV, C_MO).astype(BF16)
    mo_ref[...] = proj(C_MO, C_SMALL).astype(BF16)


def _inproj(x, sh, sc, g1, w_ext, gq, wuq, gkv, wukv, cos, sina, sinb, *, l, tm):
    M, D = x.shape
    S = cos.shape[0]
    R = M // sh.shape[0]
    nt = S // tm
    row = lambda i: (i, 0)
    modv = pl.BlockSpec((None, 1, D), lambda i: (i // (R // tm), 0, 0))
    tab = pl.BlockSpec((tm, V7X_LANES), lambda i: (i % nt, 0))
    widths = (MLA_QW, MLA_QW, MLA_VW, SWA_QW, 2 * SWA_KW, 2 * SWA_KW, MQW, MQW, MVW, MVW)
    out_shape = [jax.ShapeDtypeStruct((M, w), BF16) for w in widths]
    out_shape.append(jax.ShapeDtypeStruct((M, V7X_LANES), F32))
    out_specs = [pl.BlockSpec((tm, w), row) for w in widths] + [pl.BlockSpec((tm, V7X_LANES), row)]
    return pl.pallas_call(
        _inproj_kernel,
        grid=(M // tm,),
        in_specs=[pl.BlockSpec((tm, D), row), modv, modv, _resident((1, D)),
                  _layer_resident(w_ext, l), _resident(gq.shape), _layer_resident(wuq, l),
                  _resident(gkv.shape), _layer_resident(wukv, l), tab, tab, tab],
        out_specs=out_specs,
        out_shape=out_shape,
        compiler_params=_cparams("parallel"),
        name="inproj",
    )(x, sh, sc, g1, w_ext, gq, wuq, gkv, wukv, cos, sina, sinb)


def _mla_kernel(*refs, tk, nk):
    if nk:
        q_ref, k_ref, v_ref, kc_ref, vc_ref, o_ref, m_ref, l_ref, acc_ref = refs
    else:
        q_ref, kc_ref, vc_ref, o_ref, m_ref, l_ref, acc_ref = refs
    q = q_ref[...]
    tq = q.shape[0]
    W = V7X_LANES

    def update(kblk, vblk, first):
        ncol = kblk.shape[0] // W
        s = _dot_nt(q, kblk)
        smax = s[:, 0:W]
        for c in range(1, ncol):
            smax = jnp.maximum(smax, s[:, W * c:W * (c + 1)])
        m_new = jnp.broadcast_to(jnp.max(smax, axis=1, keepdims=True), (tq, W))
        if not first:
            m_prev = m_ref[...]
            m_new = jnp.maximum(m_prev, m_new)
            alpha = jnp.exp2(m_prev - m_new)
        lsum = None
        ps = []
        for c in range(ncol):
            pc = jnp.exp2(s[:, W * c:W * (c + 1)] - m_new)
            lsum = pc if lsum is None else lsum + pc
            ps.append(pc.astype(BF16))
        pv = _dot(jnp.concatenate(ps, axis=1), vblk)
        if first:
            l_ref[...] = lsum
            acc_ref[...] = pv
        else:
            l_ref[...] = alpha * l_ref[...] + lsum
            acc_ref[...] = alpha * acc_ref[...] + pv
        m_ref[...] = m_new

    update(kc_ref[...], vc_ref[...], True)
    if nk:
        def body(j, carry):
            off = pl.multiple_of(j * tk, tk)
            update(k_ref[pl.ds(off, tk), :], v_ref[pl.ds(off, tk), :], False)
            return carry
        lax.fori_loop(0, nk, body, 0)
    l = jnp.sum(l_ref[...], axis=1, keepdims=True)
    o_ref[...] = (acc_ref[...] / l).astype(o_ref.dtype)


def _mla(q, kc, vc, k=None, v=None, *, B, tq, tk):
    M = q.shape[0]
    T = M // B
    Lc = kc.shape[0] // B
    nq = T // tq
    H = MLA_HEADS
    qspec = pl.BlockSpec((tq, 256), lambda b, h, i: (b * nq + i, h))
    cspecs = [pl.BlockSpec((Lc, 256), lambda b, h, i: (b, h)),
              pl.BlockSpec((Lc, MLA_V), lambda b, h, i: (b, h))]
    if k is None:
        nk, in_specs, args = 0, [qspec] + cspecs, (q, kc, vc)
    else:
        nk = T // tk
        in_specs = [qspec, pl.BlockSpec((T, 256), lambda b, h, i: (b, h)),
                    pl.BlockSpec((T, MLA_V), lambda b, h, i: (b, h))] + cspecs
        args = (q, k, v, kc, vc)
    return pl.pallas_call(
        functools.partial(_mla_kernel, tk=tk, nk=nk),
        grid=(B, H, nq),
        in_specs=in_specs,
        out_specs=pl.BlockSpec((tq, MLA_V), lambda b, h, i: (b * nq + i, h)),
        out_shape=jax.ShapeDtypeStruct((M, MLA_VW), BF16),
        scratch_shapes=[pltpu.VMEM((tq, V7X_LANES), F32), pltpu.VMEM((tq, V7X_LANES), F32),
                        pltpu.VMEM((tq, MLA_V), F32)],
        compiler_params=_cparams("parallel", "parallel", "arbitrary"),
        name="mla_latent" if nk else "mla_ctx",
    )(*args)


SWA_SUB = 128
SWA_SPAN = SWA_SUB + 2 * WINDOW


def _swa_kernel(*refs, local, tq, S):
    if local:
        sink_ref, q_ref, k_ref, v_ref, kc_ref, vc_ref, o_ref = refs
    else:
        sink_ref, q_ref, kc_ref, vc_ref, o_ref = refs
    W = V7X_LANES
    d = SWA_HEAD_DIM
    ppg = SWA_HEADS // SWA_KV_HEADS // 2
    Lc = kc_ref.shape[0]

    def variants(x2):
        low = lax.broadcasted_iota(jnp.int32, (x2.shape[0], W), 1) < d
        x, xs = x2[:, :W], x2[:, W:]
        zero = jnp.zeros_like(x)
        return {(0, 0): jnp.where(low, x, zero), (1, 1): jnp.where(low, zero, x),
                (0, 1): jnp.where(low, zero, xs), (1, 0): jnp.where(low, xs, zero)}

    def with_ones(vv):
        def ones_half(t, e):
            mine = (lax.broadcasted_iota(jnp.int32, t.shape, 1) < d) == (e == 0)
            return jnp.where(mine, 1.0, 0.0).astype(t.dtype)
        return {(g, e): jnp.concatenate([t, ones_half(t, e)], axis=1) for (g, e), t in vv.items()}

    kcv = variants(kc_ref[...])
    vcv = with_ones(variants(vc_ref[...]))
    for j in range(tq // SWA_SUB):
        r = slice(SWA_SUB * j, SWA_SUB * (j + 1))
        if local:
            qpos = pl.program_id(1) * tq + SWA_SUB * j
            r0 = pl.multiple_of(jnp.clip(qpos - WINDOW, 0, S - SWA_SPAN), W)
            kwin = variants(k_ref[pl.ds(r0, SWA_SPAN), :])
            vwin = with_ones(variants(v_ref[pl.ds(r0, SWA_SPAN), :]))
            kv = {ge: jnp.concatenate([kcv[ge], kwin[ge]], axis=0) for ge in kwin}
            vv = {ge: jnp.concatenate([vcv[ge], vwin[ge]], axis=0) for ge in vwin}
            rel = (lax.broadcasted_iota(jnp.int32, (SWA_SUB, SWA_SPAN), 1)
                   - lax.broadcasted_iota(jnp.int32, (SWA_SUB, SWA_SPAN), 0)) + (r0 - qpos)
            bias = jnp.where(jnp.abs(rel) <= WINDOW, 0.0, NEG_BIG)
            bias = jnp.concatenate([bias] * ppg, axis=0)
        for g in range(SWA_KV_HEADS):
            cols = [slice(W * (ppg * g + pp), W * (ppg * g + pp + 1)) for pp in range(ppg)]
            qs = jnp.concatenate([q_ref[r, c] for c in cols], axis=0)
            ksrc, vsrc = (kv, vv) if local else (kcv, vcv)
            nkeys = ksrc[g, 0].shape[0]
            s = _dot_nt(qs, jnp.concatenate([ksrc[g, 0], ksrc[g, 1]], axis=0))
            ps, sink_terms = [], []
            for e in range(2):
                sink = jnp.concatenate(
                    [jnp.full((SWA_SUB, W), sink_ref[2 * (ppg * g + pp) + e] * LOG2E, F32)
                     for pp in range(ppg)], axis=0)
                c0 = nkeys * e
                blocks = [s[:, c0 + W * c:c0 + W * (c + 1)] for c in range(Lc // W)]
                if local:
                    blocks += [s[:, c0 + Lc + W * c:c0 + Lc + W * (c + 1)] + bias[:, W * c:W * (c + 1)]
                               for c in range(SWA_SPAN // W)]
                mx = blocks[0]
                for blk in blocks[1:]:
                    mx = jnp.maximum(mx, blk)
                m = jnp.maximum(sink, jnp.broadcast_to(jnp.max(mx, axis=1, keepdims=True), mx.shape))
                ps += [jnp.exp2(blk - m).astype(BF16) for blk in blocks]
                sink_terms.append(jnp.exp2(sink - m))
            out = _dot(jnp.concatenate(ps, axis=1),
                       jnp.concatenate([vsrc[g, 0], vsrc[g, 1]], axis=0))
            first_half = lax.broadcasted_iota(jnp.int32, sink_terms[0].shape, 1) < d
            pair_out = out[:, :W] / (out[:, W:] + jnp.where(first_half, sink_terms[0], sink_terms[1]))
            for pp, c in enumerate(cols):
                o_ref[r, c] = pair_out[SWA_SUB * pp:SWA_SUB * (pp + 1)].astype(o_ref.dtype)


def _swa(sink, q, kc, vc, k=None, v=None, *, B, tq):
    M = q.shape[0]
    T = M // B
    Lc = kc.shape[0] // B
    local = k is not None
    if not local:
        tq = T
    nq = T // tq
    qspec = pl.BlockSpec((tq, SWA_QW), lambda b, i: (b * nq + i, 0))
    cspec = pl.BlockSpec((Lc, 2 * SWA_KW), lambda b, i: (b, 0))
    in_specs = [pl.BlockSpec(memory_space=pltpu.SMEM), qspec]
    args = [sink, q]
    if local:
        in_specs += [pl.BlockSpec((T, 2 * SWA_KW), lambda b, i: (b, 0))] * 2
        args += [k, v]
    in_specs += [cspec, cspec]
    args += [kc, vc]
    return pl.pallas_call(
        functools.partial(_swa_kernel, local=local, tq=tq, S=T),
        grid=(B, nq),
        in_specs=in_specs,
        out_specs=qspec,
        out_shape=jax.ShapeDtypeStruct((M, SWA_QW), BF16),
        compiler_params=_cparams("parallel", "parallel"),
        name="swa_latent" if local else "swa_ctx",
    )(*args)


def _log_sigmoid(x):
    return jnp.minimum(x, 0.0) - jnp.log1p(jnp.exp(-jnp.abs(x)))


def _split3(x):
    hi = x.astype(BF16)
    r1 = x - hi.astype(F32)
    mid = r1.astype(BF16)
    return hi, mid, (r1 - mid.astype(F32)).astype(BF16)


def _mlstm_chunk(q_ref, k_ref, v_ref, g_ref, bias_ref, h_ref, c_scr, m_scr, *, rev):
    L = q_ref.shape[0]
    W = V7X_LANES
    H = MLSTM_HEADS
    dk, dv = MLSTM_QK, MLSTM_V
    pre = g_ref[...] + bias_ref[...]
    row = lax.broadcasted_iota(jnp.int32, (L, L), 0)
    col = lax.broadcasted_iota(jnp.int32, (L, L), 1)
    allowed = (col >= row) if rev else (col <= row)
    tri = jnp.where(allowed, 1.0, 0.0).astype(BF16)
    mask_bias = jnp.where(allowed, 0.0, -jnp.inf)
    acc3 = _dot(tri, jnp.concatenate(_split3(_log_sigmoid(pre)), axis=1))
    bcum = acc3[:, :W] + acc3[:, W:2 * W] + acc3[:, 2 * W:]
    lane0 = GATE_LANE0 + (2 * H if rev else 0)
    z = pre - pltpu.roll(bcum, W - H, 1)
    onehot = jnp.where(lax.broadcasted_iota(jnp.int32, (8, W), 1)
                       == lax.broadcasted_iota(jnp.int32, (8, W), 0) + lane0, 1.0, 0.0).astype(BF16)
    zr = _dot_nt(onehot, jnp.concatenate(_split3(z), axis=0))
    y_rows = zr[:, :L] + zr[:, L:2 * L] + zr[:, 2 * L:]
    rep = lambda t, n: jnp.concatenate([t] * n, axis=1)
    stack = lambda f: jnp.concatenate([f(hh) for hh in range(H)], axis=0)
    head = lambda t, hh: t[L * hh:L * (hh + 1)]
    ones_blk = jnp.ones((L, W), BF16)
    last = 0 if rev else L - 1

    b_rep = stack(lambda hh: jnp.broadcast_to(bcum[:, lane0 + H + hh:lane0 + H + hh + 1], (L, W)))
    i_rep = stack(lambda hh: jnp.broadcast_to(pre[:, lane0 + hh:lane0 + hh + 1], (L, W)))
    m_in = stack(lambda hh: jnp.broadcast_to(m_scr[hh, 0:1, :], (L, W)))
    g_tot = stack(lambda hh: jnp.broadcast_to(head(b_rep, hh)[last:last + 1, :], (L, W)))
    y = stack(lambda hh: y_rows[hh:hh + 1, :] + mask_bias)
    c = -jnp.maximum(m_in, jnp.broadcast_to(jnp.max(y, axis=1, keepdims=True), (H * L, W)))
    lanes = lax.broadcasted_iota(jnp.int32, (L, H * dk), 1)
    in_head = lambda hh: (lanes >= dk * hh) & (lanes < dk * (hh + 1))
    q_full, k_full = q_ref[...], k_ref[...]
    qm = stack(lambda hh: jnp.where(in_head(hh), q_full, jnp.zeros_like(q_full)))
    v_aug = [jnp.concatenate([v_ref[:, dv * hh:dv * (hh + 1)], ones_blk], axis=1) for hh in range(H)]
    s = (_dot_nt(qm, k_full) * jnp.exp(y + rep(c, L // W))).astype(BF16)
    c_in = c_scr[...]
    tot = (stack(lambda hh: _dot(head(s, hh), v_aug[hh]))
           + rep(jnp.exp(m_in + c), 2) * _dot(qm, c_in.astype(BF16)))
    hout = tot[:, :dv] / jnp.maximum(jnp.abs(tot[:, dv:]), jnp.exp(c - b_rep))
    for hh in range(H):
        h_ref[:, dv * hh:dv * (hh + 1)] = head(hout, hh)

    a = g_tot - b_rep + i_rep
    a_max = stack(lambda hh: jnp.broadcast_to(jnp.max(head(a, hh), axis=0, keepdims=True), (L, W)))
    w = jnp.exp(a - a_max)
    m_new = jnp.maximum(g_tot + m_in, a_max)
    decay = jnp.exp(g_tot + m_in - m_new)
    grow = jnp.exp(a_max - m_new)
    k32 = k_full.astype(F32)
    kw = stack(lambda hh: jnp.where(in_head(hh), k32 * rep(head(w, hh), H * dk // W), 0.0)).astype(BF16)
    d_c = lax.dot_general(kw, jnp.concatenate(v_aug, axis=0), (((0,), (0,)), ((), ())),
                          preferred_element_type=F32)
    per_head_rows = lambda t: rep(jnp.concatenate([head(t, hh)[:dk] for hh in range(H)], axis=0), 2)
    c_scr[...] = per_head_rows(decay) * c_in + per_head_rows(grow) * d_c
    for hh in range(H):
        m_scr[hh] = head(m_new, hh)[:m_scr.shape[1]]


def _mlstm_kernel(qf_ref, kf_ref, vf_ref, gf_ref, qb_ref, kb_ref, vb_ref, gb_ref, bias_ref, c0_ref, m0_ref,
                  hf_ref, hb_ref, cf_ref, mf_ref, c_scr, m_scr, *, nchunks):
    n = pl.program_id(1)

    @pl.when(n == 0)
    def _():
        c_scr[...] = c0_ref[...]
        m_scr[...] = m0_ref[...]

    _mlstm_chunk(qf_ref, kf_ref, vf_ref, gf_ref, bias_ref, hf_ref, c_scr.at[0], m_scr.at[0], rev=False)
    _mlstm_chunk(qb_ref, kb_ref, vb_ref, gb_ref, bias_ref, hb_ref, c_scr.at[1], m_scr.at[1], rev=True)

    @pl.when(n == nchunks - 1)
    def _():
        cf_ref[...] = c_scr[...]
        mf_ref[...] = m_scr[...]


def _mlstm(q, k, v, g, bias, c0, m0, *, B, L):
    M = q.shape[0]
    N = M // B // L
    H = MLSTM_HEADS
    fwd = lambda b, n: (b * N + n, 0)
    bwd = lambda b, n: (b * N + N - 1 - n, 0)
    data = lambda im: [pl.BlockSpec((L, MQW), im), pl.BlockSpec((L, MQW), im),
                       pl.BlockSpec((L, MVW), im), pl.BlockSpec((L, V7X_LANES), im)]
    cspec = pl.BlockSpec((None, 2, H * MLSTM_QK, 256), lambda b, n: (b, 0, 0, 0))
    mspec = pl.BlockSpec((None, 2, H, 8, V7X_LANES), lambda b, n: (b, 0, 0, 0, 0))
    return pl.pallas_call(
        functools.partial(_mlstm_kernel, nchunks=N),
        grid=(B, N),
        in_specs=data(fwd) + data(bwd) + [_resident(bias.shape), cspec, mspec],
        out_specs=[pl.BlockSpec((L, MVW), fwd), pl.BlockSpec((L, MVW), bwd), cspec, mspec],
        out_shape=[jax.ShapeDtypeStruct((M, MVW), F32), jax.ShapeDtypeStruct((M, MVW), F32),
                   jax.ShapeDtypeStruct(c0.shape, F32), jax.ShapeDtypeStruct(m0.shape, F32)],
        scratch_shapes=[pltpu.VMEM((2, H * MLSTM_QK, 256), F32), pltpu.VMEM((2, H, 8, V7X_LANES), F32)],
        compiler_params=_cparams("parallel", "arbitrary"),
        name="mlstm",
    )(q, k, v, g, q, k, v, g, bias, c0, m0)


def _outproj_kernel(ya_ref, ys_ref, hf_ref, hb_ref, mo_ref, gh_ref, w_ref, x_ref, gt_ref,
                    sh_ref, sc_ref, g2_ref, o_ref, h2_ref, xg_scr, *, tn):
    dv = MLSTM_V
    parts = []
    for hh in range(MLSTM_HEADS):
        sl = slice(dv * hh, dv * (hh + 1))
        hn = _rms(hf_ref[:, sl] + hb_ref[:, sl]) * gh_ref[:, sl]
        parts.append((hn * jax.nn.sigmoid(mo_ref[:, sl].astype(F32))).astype(BF16))
    y = jnp.concatenate([ya_ref[...], ys_ref[...]] + parts, axis=1)
    D = o_ref.shape[1]
    ssq = None
    for c in range(D // tn):
        cs = slice(tn * c, tn * (c + 1))
        xn = x_ref[:, cs] + gt_ref[:, cs] * _dot(y, w_ref[:, cs])
        o_ref[:, cs] = xn
        xg_scr[:, cs] = xn * (g2_ref[:, cs] * (1.0 + sc_ref[:, cs]))
        part = jnp.sum(xn * xn, axis=-1, keepdims=True)
        ssq = part if ssq is None else ssq + part
    inv = lax.rsqrt(ssq * (1.0 / D) + NORM_EPS)
    for c in range(D // tn):
        cs = slice(tn * c, tn * (c + 1))
        h2_ref[:, cs] = (xg_scr[:, cs] * inv + sh_ref[:, cs]).astype(BF16)


def _outproj(ya, ys, hf, hb, mo, gh, w_out, x, gt, sh, sc, g2, *, l, tm):
    M, D = x.shape
    R = M // gt.shape[0]
    row = lambda i: (i, 0)
    modv = pl.BlockSpec((None, 1, D), lambda i: (i // (R // tm), 0, 0))
    return pl.pallas_call(
        functools.partial(_outproj_kernel, tn=min(512, D)),
        grid=(M // tm,),
        in_specs=[pl.BlockSpec((tm, MLA_VW), row), pl.BlockSpec((tm, SWA_QW), row),
                  pl.BlockSpec((tm, MVW), row), pl.BlockSpec((tm, MVW), row),
                  pl.BlockSpec((tm, MVW), row), _resident(gh.shape), _layer_resident(w_out, l),
                  pl.BlockSpec((tm, D), row), modv, modv, modv, _resident((1, D))],
        out_specs=[pl.BlockSpec((tm, D), row), pl.BlockSpec((tm, D), row)],
        out_shape=[jax.ShapeDtypeStruct((M, D), F32), jax.ShapeDtypeStruct((M, D), BF16)],
        scratch_shapes=[pltpu.VMEM((tm, D), F32)],
        compiler_params=_cparams("parallel"),
        name="outproj",
    )(ya, ys, hf, hb, mo, gh, w_out, x, gt, sh, sc, g2)


def _ffn_kernel(x_ref, h_ref, gt_ref, gfin_ref, w1_ref, w2_ref, o_ref, *, nf, final):
    f = pl.program_id(1)
    u = jnp.maximum(_dot(h_ref[...], w1_ref[...]), 0.0)
    a = (u * u).astype(BF16)

    @pl.when(f == 0)
    def _():
        o_ref[...] = _dot(a, w2_ref[...])

    @pl.when(f != 0)
    def _():
        o_ref[...] += _dot(a, w2_ref[...])

    @pl.when(f == nf - 1)
    def _():
        D = o_ref.shape[1]
        tn = min(512, D)
        ssq = None
        for c in range(D // tn):
            cs = slice(tn * c, tn * (c + 1))
            y = x_ref[:, cs] + gt_ref[:, cs] * o_ref[:, cs]
            o_ref[:, cs] = y
            if final:
                part = jnp.sum(y * y, axis=-1, keepdims=True)
                ssq = part if ssq is None else ssq + part
        if final:
            inv = lax.rsqrt(ssq * (1.0 / D) + NORM_EPS)
            for c in range(D // tn):
                cs = slice(tn * c, tn * (c + 1))
                o_ref[:, cs] = o_ref[:, cs] * inv * gfin_ref[:, cs]


def _ffn(x, h2, gt, gfin, w1, w2, *, l, tm, tf, final):
    M, D = x.shape
    FF = w1.shape[2]
    R = M // gt.shape[0]
    nf = FF // tf
    rows = lambda i, f: (i, 0)
    return pl.pallas_call(
        functools.partial(_ffn_kernel, nf=nf, final=final),
        grid=(M // tm, nf),
        in_specs=[pl.BlockSpec((tm, D), rows), pl.BlockSpec((tm, D), rows),
                  pl.BlockSpec((None, 1, D), lambda i, f: (i // (R // tm), 0, 0)), _resident((1, D)),
                  pl.BlockSpec((None, D, tf), lambda i, f: (l, 0, f)),
                  pl.BlockSpec((None, tf, D), lambda i, f: (l, f, 0))],
        out_specs=pl.BlockSpec((tm, D), rows),
        out_shape=jax.ShapeDtypeStruct((M, D), F32),
        compiler_params=_cparams("parallel", "arbitrary"),
        name="ffn",
    )(x, h2, gt, gfin, w1, w2)


def _rope_tables(S):
    half = SWA_HEAD_DIM // 2
    pos = jnp.arange(S)
    inv = ROPE_BASE ** (-jnp.arange(0, half, 2, dtype=F32) / half)
    ar = (pos // GRID_W)[:, None].astype(F32) * inv
    ac = (pos % GRID_W)[:, None].astype(F32) * inv
    ang = jnp.concatenate([ar, ar, ac, ac] * 2, axis=-1)
    first = (jnp.arange(V7X_LANES) % 32) < 16
    sin = jnp.sin(ang)
    return jnp.cos(ang), jnp.where(first, -sin, 0.0), jnp.where(first, 0.0, sin)


def _w_in_segments():
    widths = (MLA_Q_RANK, MLA_KV_RANK, MLA_ROPE, SWA_QW, SWA_KW, SWA_KW, MQW, MQW, MVW, N_GATES, MVW)
    src = [0]
    for w in widths:
        src.append(src[-1] + w)
    segs, dst = [], 0
    for i in (0, 1, 3, 4, 5, 6, 7, 8, 10, 2, 9):
        segs.append((src[i], widths[i], dst))
        dst += widths[i]
    return segs, dst


def _permute_w_in_kernel(w_ref, o_ref):
    segs, end = _w_in_segments()
    for s0, w, d0 in segs:
        o_ref[:, d0:d0 + w] = w_ref[:, s0:s0 + w].astype(BF16)
    o_ref[:, end:] = jnp.zeros((o_ref.shape[0], o_ref.shape[1] - end), BF16)


def _permute_w_in(w_in):
    L, D, N = w_in.shape
    tr = _row_tile(D, 256)
    return pl.pallas_call(
        _permute_w_in_kernel,
        grid=(L, D // tr),
        in_specs=[pl.BlockSpec((None, tr, N), lambda l, i: (l, i, 0))],
        out_specs=pl.BlockSpec((None, tr, C_END), lambda l, i: (l, i, 0)),
        out_shape=jax.ShapeDtypeStruct((L, D, C_END), BF16),
        compiler_params=_cparams("parallel", "parallel"),
        name="permute_w_in",
    )(w_in)


def _permute_w_uq(w_uq):
    L, Rk, _ = w_uq.shape
    w = w_uq.reshape(L, Rk, MLA_HEADS, MLA_NOPE + MLA_ROPE)
    w = jnp.pad(w, ((0, 0), (0, 0), (0, 0), (0, 256 - MLA_NOPE - MLA_ROPE)))
    return w.reshape(L, Rk, MLA_QW).astype(BF16)


def _permute_w_ukv(w_ukv):
    L, Rk, _ = w_ukv.shape
    w = w_ukv.reshape(L, Rk, MLA_HEADS, MLA_NOPE + MLA_V)
    k = w[..., :MLA_NOPE].reshape(L, Rk, MLA_HEADS * MLA_NOPE)
    v = w[..., MLA_NOPE:].reshape(L, Rk, MLA_VW)
    return jnp.concatenate([k, v], axis=-1).astype(BF16)


def _row_tile(rows, want):
    return want if rows % want == 0 else rows


def kernel(x, c, ctx, c_ctx, w_mod, b_mod, g_norm1, g_norm2, w_in, mla_g_q, mla_w_uq, mla_g_kv,
           mla_w_ukv, swa_sink, mlstm_gate_bias, mlstm_g_h, w_out, w_ff1, w_ff2, g_final):
    B, S, D = x.shape
    Lc = ctx.shape[1]
    depth = w_in.shape[0]
    H = MLSTM_HEADS

    rows = -(-(B + 1) // 8) * 8
    c_all = jnp.concatenate([c, c_ctx[None, :], jnp.zeros((rows - B - 1, D), F32)], axis=0)
    mod = _mod_all(c_all, w_mod, b_mod)
    mod6 = mod.reshape(depth, rows, 6, D)

    w_in_p = _permute_w_in(w_in)
    w_uq_p = _permute_w_uq(mla_w_uq)
    w_ukv_p = _permute_w_ukv(mla_w_ukv)
    w_out_b = w_out.astype(BF16)
    w1_b = w_ff1.astype(BF16)
    w2_b = w_ff2.astype(BF16)
    cos, sina, sinb = _rope_tables(S)
    ones_t = jnp.ones((Lc, V7X_LANES), F32)
    zeros_t = jnp.zeros((Lc, V7X_LANES), F32)
    bias_lanes = jnp.pad(mlstm_gate_bias.reshape(depth, 1, N_GATES),
                         ((0, 0), (0, 0), (GATE_LANE0, V7X_LANES - GATE_LANE0 - N_GATES)))
    c_zero = jnp.zeros((B, 2, H * MLSTM_QK, 256), F32)
    m_zero = jnp.zeros((B, 2, H, 8, V7X_LANES), F32)

    tm = _row_tile(S, 512)
    FF = w_ff1.shape[2]
    tm_ffn, tf = _row_tile(S, 512), min(2048, FF)
    tq_mla = tk_mla = _row_tile(S, 2048)
    gfin = g_final[None, :]
    tq_swa = _row_tile(S, 1024)
    chunk = _row_tile(S, 256)

    xs = x.reshape(B * S, D)
    xc = ctx.reshape(B * Lc, D)
    for l in range(depth):
        need_ctx = l < depth - 1
        vec = lambda j: mod6[l, :B, j][:, None, :]
        vecc = lambda j: mod6[l, B:B + 1, j][:, None, :]
        g1 = g_norm1[l][None, :]
        g2 = g_norm2[l][None, :]
        gq = mla_g_q[l][None, :]
        gkv = mla_g_kv[l][None, :]
        gh = mlstm_g_h[l].reshape(1, MVW)
        inproj = functools.partial(_inproj, g1=g1, w_ext=w_in_p, gq=gq, wuq=w_uq_p, gkv=gkv,
                                   wukv=w_ukv_p, l=l)
        ffn = functools.partial(_ffn, gfin=gfin, w1=w1_b, w2=w2_b, l=l, tf=tf)
        (qmc, kmc, vmc, qsc, ksc, vsc, mqc, mkc, mvc, moc, gsc) = inproj(
            xc, vecc(0), vecc(1), cos=ones_t, sina=zeros_t, sinb=zeros_t, tm=Lc)
        hfc, hbc, c_st, m_st = _mlstm(mqc, mkc, mvc, gsc, bias_lanes[l], c_zero, m_zero, B=B, L=Lc)

        (qm, km, vm, qs, ks, vs, mq, mk, mv, mo, gs) = inproj(
            xs, vec(0), vec(1), cos=cos, sina=sina, sinb=sinb, tm=tm)
        y_mla = _mla(qm, kmc, vmc, km, vm, B=B, tq=tq_mla, tk=tk_mla)
        y_swa = _swa(swa_sink[l], qs, ksc, vsc, ks, vs, B=B, tq=tq_swa)
        hf, hb, _, _ = _mlstm(mq, mk, mv, gs, bias_lanes[l], c_st, m_st, B=B, L=chunk)
        xs, h2 = _outproj(y_mla, y_swa, hf, hb, mo, gh, w_out_b, xs, vec(2), vec(3), vec(4), g2, l=l, tm=tm)
        xs = ffn(xs, h2, vec(5), tm=tm_ffn, final=not need_ctx)
        if need_ctx:
            yc_mla = _mla(qmc, kmc, vmc, B=B, tq=Lc, tk=Lc)
            yc_swa = _swa(swa_sink[l], qsc, ksc, vsc, B=B, tq=Lc)
            xc, h2c = _outproj(yc_mla, yc_swa, hfc, hbc, moc, gh, w_out_b, xc, vecc(2), vecc(3), vecc(4), g2,
                               l=l, tm=Lc)
            xc = ffn(xc, h2c, vecc(5), tm=B * Lc, tf=min(512, FF), final=False)
    return xs.reshape(B, S, D)
```

```python
import functools
import math

import jax
import jax.numpy as jnp
from jax import lax
from jax.experimental import pallas as pl
from jax.experimental.pallas import tpu as pltpu

F32 = jnp.float32
BF16 = jnp.bfloat16

GRID_W = 64
ROPE_BASE = 10000.0
NORM_EPS = 1e-6
MLA_HEADS = 4
MLA_Q_RANK = 512
MLA_KV_RANK = 256
MLA_NOPE = 128
MLA_ROPE = 64
MLA_V = 128
SWA_HEADS = 16
SWA_KV_HEADS = 2
SWA_HEAD_DIM = 64
WINDOW = 128
MLSTM_HEADS = 4
MLSTM_QK = 64
MLSTM_V = 128
N_GATES = 4 * MLSTM_HEADS

V7X_LANES = 128
V7X_VMEM_LIMIT_BYTES = 63 * 1024 * 1024

LOG2E = math.log2(math.e)
MLA_QSCALE = (MLA_NOPE + MLA_ROPE) ** -0.5 * LOG2E
SWA_QSCALE = SWA_HEAD_DIM ** -0.5 * LOG2E
MLSTM_QSCALE = MLSTM_QK ** -0.5
NEG_BIG = -1e30

C_ZQ = 0
C_ZKV = C_ZQ + MLA_Q_RANK
C_SQ = C_ZKV + MLA_KV_RANK
C_SK = C_SQ + SWA_HEADS * SWA_HEAD_DIM
C_SV = C_SK + SWA_KV_HEADS * SWA_HEAD_DIM
C_MQ = C_SV + SWA_KV_HEADS * SWA_HEAD_DIM
C_MK = C_MQ + MLSTM_HEADS * MLSTM_QK
C_MV = C_MK + MLSTM_HEADS * MLSTM_QK
C_MO = C_MV + MLSTM_HEADS * MLSTM_V
C_SMALL = C_MO + MLSTM_HEADS * MLSTM_V
C_END = C_SMALL + V7X_LANES
GATE_LANE0 = MLA_ROPE

MLA_QW = MLA_HEADS * 256
MLA_VW = MLA_HEADS * MLA_V
SWA_QW = SWA_HEADS * SWA_HEAD_DIM
SWA_KW = SWA_KV_HEADS * SWA_HEAD_DIM
MQW = MLSTM_HEADS * MLSTM_QK
MVW = MLSTM_HEADS * MLSTM_V


def _cparams(*sem):
    return pltpu.CompilerParams(dimension_semantics=sem, vmem_limit_bytes=V7X_VMEM_LIMIT_BYTES)


def _resident(shape):
    nd = len(shape)
    return pl.BlockSpec(shape, lambda *_: (0,) * nd, pipeline_mode=pl.Buffered(1))


def _layer_resident(stacked, l):
    nd = stacked.ndim - 1
    return pl.BlockSpec((None,) + stacked.shape[1:], lambda *_: (l,) + (0,) * nd,
                        pipeline_mode=pl.Buffered(1))


def _dot(a, b):
    return jnp.dot(a, b, preferred_element_type=F32)


def _dot_nt(a, b):
    return lax.dot_general(a, b, (((1,), (1,)), ((), ())), preferred_element_type=F32)


def _rms(x):
    return x * lax.rsqrt(jnp.mean(x * x, axis=-1, keepdims=True) + NORM_EPS)


def _mod_kernel(c_ref, w_ref, b_ref, o_ref):
    c = c_ref[...]
    a = c * jax.nn.sigmoid(c)
    w = w_ref[...]
    a_hi, w_hi = a.astype(BF16), w.astype(BF16)
    a_lo = (a - a_hi.astype(F32)).astype(BF16)
    w_lo = (w - w_hi.astype(F32)).astype(BF16)
    o_ref[...] = _dot(a_hi, w_hi) + _dot(a_lo, w_hi) + _dot(a_hi, w_lo) + b_ref[...]


def _mod_all(c_all, w_mod, b_mod):
    L, D, N = w_mod.shape
    R = c_all.shape[0]
    tn = next(t for t in (2048, 1024, N) if N % t == 0)
    return pl.pallas_call(
        _mod_kernel,
        grid=(L, N // tn),
        in_specs=[pl.BlockSpec((R, D), lambda l, j: (0, 0)),
                  pl.BlockSpec((None, D, tn), lambda l, j: (l, 0, j)),
                  pl.BlockSpec((None, 1, tn), lambda l, j: (l, 0, j))],
        out_specs=pl.BlockSpec((None, R, tn), lambda l, j: (l, 0, j)),
        out_shape=jax.ShapeDtypeStruct((L, R, N), F32),
        compiler_params=_cparams("parallel", "parallel"),
        name="mod",
    )(c_all, w_mod, b_mod.reshape(L, 1, N))


def _rope(x, cos, sina, sinb):
    return x * cos + pltpu.roll(x, V7X_LANES - 16, 1) * sina + pltpu.roll(x, 16, 1) * sinb


def _inproj_kernel(x_ref, sh_ref, sc_ref, g1_ref, w_ref, gq_ref, wuq_ref, gkv_ref, wukv_ref,
                   cos_ref, sina_ref, sinb_ref,
                   qm_ref, km_ref, vm_ref, qs_ref, ks_ref, vs_ref, mq_ref, mk_ref, mv_ref, mo_ref,
                   gs_ref):
    tm = x_ref.shape[0]
    h = (_rms(x_ref[...]) * g1_ref[...]) * (1.0 + sc_ref[...]) + sh_ref[...]
    hb = h.astype(BF16)
    cos, sina, sinb = cos_ref[...], sina_ref[...], sinb_ref[...]
    rope = lambda t: _rope(t, cos, sina, sinb)
    proj = lambda c0, c1: _dot(hb, w_ref[:, c0:c1])
    low = lax.broadcasted_iota(jnp.int32, (tm, V7X_LANES), 1) < MLA_ROPE

    small = proj(C_SMALL, C_END)
    gs_ref[...] = small
    k_rope = jnp.where(low, rope(small), 0.0).astype(BF16)

    zqn = (_rms(proj(C_ZQ, C_ZKV)) * gq_ref[...]).astype(BF16)
    qall = _dot(zqn, wuq_ref[...]) * MLA_QSCALE
    for hh in range(MLA_HEADS):
        qa = qall[:, 256 * hh:256 * (hh + 1)]
        qm_ref[:, 256 * hh:256 * hh + 128] = qa[:, :128].astype(BF16)
        qm_ref[:, 256 * hh + 128:256 * (hh + 1)] = jnp.where(low, rope(qa[:, 128:]), 0.0).astype(BF16)

    zkvn = (_rms(proj(C_ZKV, C_SQ)) * gkv_ref[...]).astype(BF16)
    kv = _dot(zkvn, wukv_ref[...])
    for hh in range(MLA_HEADS):
        km_ref[:, 256 * hh:256 * hh + 128] = kv[:, 128 * hh:128 * (hh + 1)].astype(BF16)
        km_ref[:, 256 * hh + 128:256 * (hh + 1)] = k_rope
    vm_ref[...] = kv[:, MLA_HEADS * MLA_NOPE:].astype(BF16)

    for c in range(SWA_QW // 512):
        sq = proj(C_SQ + 512 * c, C_SQ + 512 * (c + 1))
        for p in range(4):
            blk = rope(sq[:, 128 * p:128 * (p + 1)]) * SWA_QSCALE
            qs_ref[:, 512 * c + 128 * p:512 * c + 128 * (p + 1)] = blk.astype(BF16)
    for ref, val in ((ks_ref, rope(proj(C_SK, C_SV))), (vs_ref, proj(C_SV, C_MQ))):
        ref[:, :SWA_KW] = val.astype(BF16)
        ref[:, SWA_KW:] = pltpu.roll(val, SWA_HEAD_DIM, 1).astype(BF16)

    mq_ref[...] = (proj(C_MQ, C_MK) * MLSTM_QSCALE).astype(BF16)
    mk_ref[...] = proj(C_MK, C_MV).astype(BF16)
    mv_ref[...] = proj(C_MV, C_MO).astype(BF16)
    mo_ref[...] = proj(C_MO, C_SMALL).astype(BF16)


def _inproj(x, sh, sc, g1, w_ext, gq, wuq, gkv, wukv, cos, sina, sinb, *, l, tm):
    M, D = x.shape
    S = cos.shape[0]
    R = M // sh.shape[0]
    nt = S // tm
    row = lambda i: (i, 0)
    modv = pl.BlockSpec((None, 1, D), lambda i: (i // (R // tm), 0, 0))
    tab = pl.BlockSpec((tm, V7X_LANES), lambda i: (i % nt, 0))
    widths = (MLA_QW, MLA_QW, MLA_VW, SWA_QW, 2 * SWA_KW, 2 * SWA_KW, MQW, MQW, MVW, MVW)
    out_shape = [jax.ShapeDtypeStruct((M, w), BF16) for w in widths]
    out_shape.append(jax.ShapeDtypeStruct((M, V7X_LANES), F32))
    out_specs = [pl.BlockSpec((tm, w), row) for w in widths] + [pl.BlockSpec((tm, V7X_LANES), row)]
    return pl.pallas_call(
        _inproj_kernel,
        grid=(M // tm,),
        in_specs=[pl.BlockSpec((tm, D), row), modv, modv, _resident((1, D)),
                  _layer_resident(w_ext, l), _resident(gq.shape), _layer_resident(wuq, l),
                  _resident(gkv.shape), _layer_resident(wukv, l), tab, tab, tab],
        out_specs=out_specs,
        out_shape=out_shape,
        compiler_params=_cparams("parallel"),
        name="inproj",
    )(x, sh, sc, g1, w_ext, gq, wuq, gkv, wukv, cos, sina, sinb)


def _mla_kernel(*refs, tk, nk):
    if nk:
        q_ref, k_ref, v_ref, kc_ref, vc_ref, o_ref, m_ref, l_ref, acc_ref = refs
    else:
        q_ref, kc_ref, vc_ref, o_ref, m_ref, l_ref, acc_ref = refs
    q = q_ref[...]
    tq = q.shape[0]
    W = V7X_LANES

    def update(kblk, vblk, first):
        ncol = kblk.shape[0] // W
        s = _dot_nt(q, kblk)
        smax = s[:, 0:W]
        for c in range(1, ncol):
            smax = jnp.maximum(smax, s[:, W * c:W * (c + 1)])
        m_new = jnp.broadcast_to(jnp.max(smax, axis=1, keepdims=True), (tq, W))
        if not first:
            m_prev = m_ref[...]
            m_new = jnp.maximum(m_prev, m_new)
            alpha = jnp.exp2(m_prev - m_new)
        lsum = None
        ps = []
        for c in range(ncol):
            pc = jnp.exp2(s[:, W * c:W * (c + 1)] - m_new)
            lsum = pc if lsum is None else lsum + pc
            ps.append(pc.astype(BF16))
        pv = _dot(jnp.concatenate(ps, axis=1), vblk)
        if first:
            l_ref[...] = lsum
            acc_ref[...] = pv
        else:
            l_ref[...] = alpha * l_ref[...] + lsum
            acc_ref[...] = alpha * acc_ref[...] + pv
        m_ref[...] = m_new

    update(kc_ref[...], vc_ref[...], True)
    if nk:
        def body(j, carry):
            off = pl.multiple_of(j * tk, tk)
            update(k_ref[pl.ds(off, tk), :], v_ref[pl.ds(off, tk), :], False)
            return carry
        lax.fori_loop(0, nk, body, 0)
    l = jnp.sum(l_ref[...], axis=1, keepdims=True)
    o_ref[...] = (acc_ref[...] / l).astype(o_ref.dtype)


def _mla(q, kc, vc, k=None, v=None, *, B, tq, tk):
    M = q.shape[0]
    T = M // B
    Lc = kc.shape[0] // B
    nq = T // tq
    H = MLA_HEADS
    qspec = pl.BlockSpec((tq, 256), lambda b, h, i: (b * nq + i, h))
    cspecs = [pl.BlockSpec((Lc, 256), lambda b, h, i: (b, h)),
              pl.BlockSpec((Lc, MLA_V), lambda b, h, i: (b, h))]
    if k is None:
        nk, in_specs, args = 0, [qspec] + cspecs, (q, kc, vc)
    else:
        nk = T // tk
        in_specs = [qspec, pl.BlockSpec((T, 256), lambda b, h, i: (b, h)),
                    pl.BlockSpec((T, MLA_V), lambda b, h, i: (b, h))] + cspecs
        args = (q, k, v, kc, vc)
    return pl.pallas_call(
        functools.partial(_mla_kernel, tk=tk, nk=nk),
        grid=(B, H, nq),
        in_specs=in_specs,
        out_specs=pl.BlockSpec((tq, MLA_V), lambda b, h, i: (b * nq + i, h)),
        out_shape=jax.ShapeDtypeStruct((M, MLA_VW), BF16),
        scratch_shapes=[pltpu.VMEM((tq, V7X_LANES), F32), pltpu.VMEM((tq, V7X_LANES), F32),
                        pltpu.VMEM((tq, MLA_V), F32)],
        compiler_params=_cparams("parallel", "parallel", "arbitrary"),
        name="mla_latent" if nk else "mla_ctx",
    )(*args)


SWA_SUB = 128
SWA_SPAN = SWA_SUB + 2 * WINDOW


def _swa_kernel(*refs, local, tq, S):
    if local:
        sink_ref, q_ref, k_ref, v_ref, kc_ref, vc_ref, o_ref = refs
    else:
        sink_ref, q_ref, kc_ref, vc_ref, o_ref = refs
    W = V7X_LANES
    d = SWA_HEAD_DIM
    ppg = SWA_HEADS // SWA_KV_HEADS // 2
    Lc = kc_ref.shape[0]

    def variants(x2):
        low = lax.broadcasted_iota(jnp.int32, (x2.shape[0], W), 1) < d
        x, xs = x2[:, :W], x2[:, W:]
        zero = jnp.zeros_like(x)
        return {(0, 0): jnp.where(low, x, zero), (1, 1): jnp.where(low, zero, x),
                (0, 1): jnp.where(low, zero, xs), (1, 0): jnp.where(low, xs, zero)}

    def with_ones(vv):
        def ones_half(t, e):
            mine = (lax.broadcasted_iota(jnp.int32, t.shape, 1) < d) == (e == 0)
            return jnp.where(mine, 1.0, 0.0).astype(t.dtype)
        return {(g, e): jnp.concatenate([t, ones_half(t, e)], axis=1) for (g, e), t in vv.items()}

    kcv = variants(kc_ref[...])
    vcv = with_ones(variants(vc_ref[...]))
    for j in range(tq // SWA_SUB):
        r = slice(SWA_SUB * j, SWA_SUB * (j + 1))
        if local:
            qpos = pl.program_id(1) * tq + SWA_SUB * j
            r0 = pl.multiple_of(jnp.clip(qpos - WINDOW, 0, S - SWA_SPAN), W)
            kwin = variants(k_ref[pl.ds(r0, SWA_SPAN), :])
            vwin = with_ones(variants(v_ref[pl.ds(r0, SWA_SPAN), :]))
            kv = {ge: jnp.concatenate([kcv[ge], kwin[ge]], axis=0) for ge in kwin}
            vv = {ge: jnp.concatenate([vcv[ge], vwin[ge]], axis=0) for ge in vwin}
            rel = (lax.broadcasted_iota(jnp.int32, (SWA_SUB, SWA_SPAN), 1)
                   - lax.broadcasted_iota(jnp.int32, (SWA_SUB, SWA_SPAN), 0)) + (r0 - qpos)
            bias = jnp.where(jnp.abs(rel) <= WINDOW, 0.0, NEG_BIG)
            bias = jnp.concatenate([bias] * ppg, axis=0)
        for g in range(SWA_KV_HEADS):
            cols = [slice(W * (ppg * g + pp), W * (ppg * g + pp + 1)) for pp in range(ppg)]
            qs = jnp.concatenate([q_ref[r, c] for c in cols], axis=0)
            ksrc, vsrc = (kv, vv) if local else (kcv, vcv)
            nkeys = ksrc[g, 0].shape[0]
            s = _dot_nt(qs, jnp.concatenate([ksrc[g, 0], ksrc[g, 1]], axis=0))
            ps, sink_terms = [], []
            for e in range(2):
                sink = jnp.concatenate(
                    [jnp.full((SWA_SUB, W), sink_ref[2 * (ppg * g + pp) + e] * LOG2E, F32)
                     for pp in range(ppg)], axis=0)
                c0 = nkeys * e
                blocks = [s[:, c0 + W * c:c0 + W * (c + 1)] for c in range(Lc // W)]
                if local:
                    blocks += [s[:, c0 + Lc + W * c:c0 + Lc + W * (c + 1)] + bias[:, W * c:W * (c + 1)]
                               for c in range(SWA_SPAN // W)]
                mx = blocks[0]
                for blk in blocks[1:]:
                    mx = jnp.maximum(mx, blk)
                m = jnp.maximum(sink, jnp.broadcast_to(jnp.max(mx, axis=1, keepdims=True), mx.shape))
                ps += [jnp.exp2(blk - m).astype(BF16) for blk in blocks]
                sink_terms.append(jnp.exp2(sink - m))
            out = _dot(jnp.concatenate(ps, axis=1),
                       jnp.concatenate([vsrc[g, 0], vsrc[g, 1]], axis=0))
            first_half = lax.broadcasted_iota(jnp.int32, sink_terms[0].shape, 1) < d
            pair_out = out[:, :W] / (out[:, W:] + jnp.where(first_half, sink_terms[0], sink_terms[1]))
            for pp, c in enumerate(cols):
                o_ref[r, c] = pair_out[SWA_SUB * pp:SWA_SUB * (pp + 1)].astype(o_ref.dtype)


def _swa(sink, q, kc, vc, k=None, v=None, *, B, tq):
    M = q.shape[0]
    T = M // B
    Lc = kc.shape[0] // B
    local = k is not None
    if not local:
        tq = T
    nq = T // tq
    qspec = pl.BlockSpec((tq, SWA_QW), lambda b, i: (b * nq + i, 0))
    cspec = pl.BlockSpec((Lc, 2 * SWA_KW), lambda b, i: (b, 0))
    in_specs = [pl.BlockSpec(memory_space=pltpu.SMEM), qspec]
    args = [sink, q]
    if local:
        in_specs += [pl.BlockSpec((T, 2 * SWA_KW), lambda b, i: (b, 0))] * 2
        args += [k, v]
    in_specs += [cspec, cspec]
    args += [kc, vc]
    return pl.pallas_call(
        functools.partial(_swa_kernel, local=local, tq=tq, S=T),
        grid=(B, nq),
        in_specs=in_specs,
        out_specs=qspec,
        out_shape=jax.ShapeDtypeStruct((M, SWA_QW), BF16),
        compiler_params=_cparams("parallel", "parallel"),
        name="swa_latent" if local else "swa_ctx",
    )(*args)


def _log_sigmoid(x):
    return jnp.minimum(x, 0.0) - jnp.log1p(jnp.exp(-jnp.abs(x)))


def _split3(x):
    hi = x.astype(BF16)
    r1 = x - hi.astype(F32)
    mid = r1.astype(BF16)
    return hi, mid, (r1 - mid.astype(F32)).astype(BF16)


def _mlstm_chunk(q_ref, k_ref, v_ref, g_ref, bias_ref, h_ref, c_scr, m_scr, *, rev):
    L = q_ref.shape[0]
    W = V7X_LANES
    H = MLSTM_HEADS
    dk, dv = MLSTM_QK, MLSTM_V
    pre = g_ref[...] + bias_ref[...]
    row = lax.broadcasted_iota(jnp.int32, (L, L), 0)
    col = lax.broadcasted_iota(jnp.int32, (L, L), 1)
    allowed = (col >= row) if rev else (col <= row)
    tri = jnp.where(allowed, 1.0, 0.0).astype(BF16)
    mask_bias = jnp.where(allowed, 0.0, -jnp.inf)
    acc3 = _dot(tri, jnp.concatenate(_split3(_log_sigmoid(pre)), axis=1))
    bcum = acc3[:, :W] + acc3[:, W:2 * W] + acc3[:, 2 * W:]
    lane0 = GATE_LANE0 + (2 * H if rev else 0)
    z = pre - pltpu.roll(bcum, W - H, 1)
    onehot = jnp.where(lax.broadcasted_iota(jnp.int32, (8, W), 1)
                       == lax.broadcasted_iota(jnp.int32, (8, W), 0) + lane0, 1.0, 0.0).astype(BF16)
    zr = _dot_nt(onehot, jnp.concatenate(_split3(z), axis=0))
    y_rows = zr[:, :L] + zr[:, L:2 * L] + zr[:, 2 * L:]
    rep = lambda t, n: jnp.concatenate([t] * n, axis=1)
    stack = lambda f: jnp.concatenate([f(hh) for hh in range(H)], axis=0)
    head = lambda t, hh: t[L * hh:L * (hh + 1)]
    ones_blk = jnp.ones((L, W), BF16)
    last = 0 if rev else L - 1

    b_rep = stack(lambda hh: jnp.broadcast_to(bcum[:, lane0 + H + hh:lane0 + H + hh + 1], (L, W)))
    i_rep = stack(lambda hh: jnp.broadcast_to(pre[:, lane0 + hh:lane0 + hh + 1], (L, W)))
    m_in = stack(lambda hh: jnp.broadcast_to(m_scr[hh, 0:1, :], (L, W)))
    g_tot = stack(lambda hh: jnp.broadcast_to(head(b_rep, hh)[last:last + 1, :], (L, W)))
    y = stack(lambda hh: y_rows[hh:hh + 1, :] + mask_bias)
    c = -jnp.maximum(m_in, jnp.broadcast_to(jnp.max(y, axis=1, keepdims=True), (H * L, W)))
    lanes = lax.broadcasted_iota(jnp.int32, (L, H * dk), 1)
    in_head = lambda hh: (lanes >= dk * hh) & (lanes < dk * (hh + 1))
    q_full, k_full = q_ref[...], k_ref[...]
    qm = stack(lambda hh: jnp.where(in_head(hh), q_full, jnp.zeros_like(q_full)))
    v_aug = [jnp.concatenate([v_ref[:, dv * hh:dv * (hh + 1)], ones_blk], axis=1) for hh in range(H)]
    s = (_dot_nt(qm, k_full) * jnp.exp(y + rep(c, L // W))).astype(BF16)
    c_in = c_scr[...]
    tot = (stack(lambda hh: _dot(head(s, hh), v_aug[hh]))
           + rep(jnp.exp(m_in + c), 2) * _dot(qm, c_in.astype(BF16)))
    hout = tot[:, :dv] / jnp.maximum(jnp.abs(tot[:, dv:]), jnp.exp(c - b_rep))
    for hh in range(H):
        h_ref[:, dv * hh:dv * (hh + 1)] = head(hout, hh)

    a = g_tot - b_rep + i_rep
    a_max = stack(lambda hh: jnp.broadcast_to(jnp.max(head(a, hh), axis=0, keepdims=True), (L, W)))
    w = jnp.exp(a - a_max)
    m_new = jnp.maximum(g_tot + m_in, a_max)
    decay = jnp.exp(g_tot + m_in - m_new)
    grow = jnp.exp(a_max - m_new)
    k32 = k_full.astype(F32)
    kw = stack(lambda hh: jnp.where(in_head(hh), k32 * rep(head(w, hh), H * dk // W), 0.0)).astype(BF16)
    d_c = lax.dot_general(kw, jnp.concatenate(v_aug, axis=0), (((0,), (0,)), ((), ())),
                          preferred_element_type=F32)
    per_head_rows = lambda t: rep(jnp.concatenate([head(t, hh)[:dk] for hh in range(H)], axis=0), 2)
    c_scr[...] = per_head_rows(decay) * c_in + per_head_rows(grow) * d_c
    for hh in range(H):
        m_scr[hh] = head(m_new, hh)[:m_scr.shape[1]]


def _mlstm_kernel(qf_ref, kf_ref, vf_ref, gf_ref, qb_ref, kb_ref, vb_ref, gb_ref, bias_ref, c0_ref, m0_ref,
                  hf_ref, hb_ref, cf_ref, mf_ref, c_scr, m_scr, *, nchunks):
    n = pl.program_id(1)

    @pl.when(n == 0)
    def _():
        c_scr[...] = c0_ref[...]
        m_scr[...] = m0_ref[...]

    _mlstm_chunk(qf_ref, kf_ref, vf_ref, gf_ref, bias_ref, hf_ref, c_scr.at[0], m_scr.at[0], rev=False)
    _mlstm_chunk(qb_ref, kb_ref, vb_ref, gb_ref, bias_ref, hb_ref, c_scr.at[1], m_scr.at[1], rev=True)

    @pl.when(n == nchunks - 1)
    def _():
        cf_ref[...] = c_scr[...]
        mf_ref[...] = m_scr[...]


def _mlstm(q, k, v, g, bias, c0, m0, *, B, L):
    M = q.shape[0]
    N = M // B // L
    H = MLSTM_HEADS
    fwd = lambda b, n: (b * N + n, 0)
    bwd = lambda b, n: (b * N + N - 1 - n, 0)
    data = lambda im: [pl.BlockSpec((L, MQW), im), pl.BlockSpec((L, MQW), im),
                       pl.BlockSpec((L, MVW), im), pl.BlockSpec((L, V7X_LANES), im)]
    cspec = pl.BlockSpec((None, 2, H * MLSTM_QK, 256), lambda b, n: (b, 0, 0, 0))
    mspec = pl.BlockSpec((None, 2, H, 8, V7X_LANES), lambda b, n: (b, 0, 0, 0, 0))
    return pl.pallas_call(
        functools.partial(_mlstm_kernel, nchunks=N),
        grid=(B, N),
        in_specs=data(fwd) + data(bwd) + [_resident(bias.shape), cspec, mspec],
        out_specs=[pl.BlockSpec((L, MVW), fwd), pl.BlockSpec((L, MVW), bwd), cspec, mspec],
        out_shape=[jax.ShapeDtypeStruct((M, MVW), F32), jax.ShapeDtypeStruct((M, MVW), F32),
                   jax.ShapeDtypeStruct(c0.shape, F32), jax.ShapeDtypeStruct(m0.shape, F32)],
        scratch_shapes=[pltpu.VMEM((2, H * MLSTM_QK, 256), F32), pltpu.VMEM((2, H, 8, V7X_LANES), F32)],
        compiler_params=_cparams("parallel", "arbitrary"),
        name="mlstm",
    )(q, k, v, g, q, k, v, g, bias, c0, m0)


def _outproj_kernel(ya_ref, ys_ref, hf_ref, hb_ref, mo_ref, gh_ref, w_ref, x_ref, gt_ref,
                    sh_ref, sc_ref, g2_ref, o_ref, h2_ref, xg_scr, *, tn):
    dv = MLSTM_V
    parts = []
    for hh in range(MLSTM_HEADS):
        sl = slice(dv * hh, dv * (hh + 1))
        hn = _rms(hf_ref[:, sl] + hb_ref[:, sl]) * gh_ref[:, sl]
        parts.append((hn * jax.nn.sigmoid(mo_ref[:, sl].astype(F32))).astype(BF16))
    y = jnp.concatenate([ya_ref[...], ys_ref[...]] + parts, axis=1)
    D = o_ref.shape[1]
    ssq = None
    for c in range(D // tn):
        cs = slice(tn * c, tn * (c + 1))
        xn = x_ref[:, cs] + gt_ref[:, cs] * _dot(y, w_ref[:, cs])
        o_ref[:, cs] = xn
        xg_scr[:, cs] = xn * (g2_ref[:, cs] * (1.0 + sc_ref[:, cs]))
        part = jnp.sum(xn * xn, axis=-1, keepdims=True)
        ssq = part if ssq is None else ssq + part
    inv = lax.rsqrt(ssq * (1.0 / D) + NORM_EPS)
    for c in range(D // tn):
        cs = slice(tn * c, tn * (c + 1))
        h2_ref[:, cs] = (xg_scr[:, cs] * inv + sh_ref[:, cs]).astype(BF16)


def _outproj(ya, ys, hf, hb, mo, gh, w_out, x, gt, sh, sc, g2, *, l, tm):
    M, D = x.shape
    R = M // gt.shape[0]
    row = lambda i: (i, 0)
    modv = pl.BlockSpec((None, 1, D), lambda i: (i // (R // tm), 0, 0))
    return pl.pallas_call(
        functools.partial(_outproj_kernel, tn=min(512, D)),
        grid=(M // tm,),
        in_specs=[pl.BlockSpec((tm, MLA_VW), row), pl.BlockSpec((tm, SWA_QW), row),
                  pl.BlockSpec((tm, MVW), row), pl.BlockSpec((tm, MVW), row),
                  pl.BlockSpec((tm, MVW), row), _resident(gh.shape), _layer_resident(w_out, l),
                  pl.BlockSpec((tm, D), row), modv, modv, modv, _resident((1, D))],
        out_specs=[pl.BlockSpec((tm, D), row), pl.BlockSpec((tm, D), row)],
        out_shape=[jax.ShapeDtypeStruct((M, D), F32), jax.ShapeDtypeStruct((M, D), BF16)],
        scratch_shapes=[pltpu.VMEM((tm, D), F32)],
        compiler_params=_cparams("parallel"),
        name="outproj",
    )(ya, ys, hf, hb, mo, gh, w_out, x, gt, sh, sc, g2)


def _ffn_kernel(x_ref, h_ref, gt_ref, gfin_ref, w1_ref, w2_ref, o_ref, *, nf, final):
    f = pl.program_id(1)
    u = jnp.maximum(_dot(h_ref[...], w1_ref[...]), 0.0)
    a = (u * u).astype(BF16)

    @pl.when(f == 0)
    def _():
        o_ref[...] = _dot(a, w2_ref[...])

    @pl.when(f != 0)
    def _():
        o_ref[...] += _dot(a, w2_ref[...])

    @pl.when(f == nf - 1)
    def _():
        D = o_ref.shape[1]
        tn = min(512, D)
        ssq = None
        for c in range(D // tn):
            cs = slice(tn * c, tn * (c + 1))
            y = x_ref[:, cs] + gt_ref[:, cs] * o_ref[:, cs]
            o_ref[:, cs] = y
            if final:
                part = jnp.sum(y * y, axis=-1, keepdims=True)
                ssq = part if ssq is None else ssq + part
        if final:
            inv = lax.rsqrt(ssq * (1.0 / D) + NORM_EPS)
            for c in range(D // tn):
                cs = slice(tn * c, tn * (c + 1))
                o_ref[:, cs] = o_ref[:, cs] * inv * gfin_ref[:, cs]


def _ffn(x, h2, gt, gfin, w1, w2, *, l, tm, tf, final):
    M, D = x.shape
    FF = w1.shape[2]
    R = M // gt.shape[0]
    nf = FF // tf
    rows = lambda i, f: (i, 0)
    return pl.pallas_call(
        functools.partial(_ffn_kernel, nf=nf, final=final),
        grid=(M // tm, nf),
        in_specs=[pl.BlockSpec((tm, D), rows), pl.BlockSpec((tm, D), rows),
                  pl.BlockSpec((None, 1, D), lambda i, f: (i // (R // tm), 0, 0)), _resident((1, D)),
                  pl.BlockSpec((None, D, tf), lambda i, f: (l, 0, f)),
                  pl.BlockSpec((None, tf, D), lambda i, f: (l, f, 0))],
        out_specs=pl.BlockSpec((tm, D), rows),
        out_shape=jax.ShapeDtypeStruct((M, D), F32),
        compiler_params=_cparams("parallel", "arbitrary"),
        name="ffn",
    )(x, h2, gt, gfin, w1, w2)


def _rope_tables(S):
    half = SWA_HEAD_DIM // 2
    pos = jnp.arange(S)
    inv = ROPE_BASE ** (-jnp.arange(0, half, 2, dtype=F32) / half)
    ar = (pos // GRID_W)[:, None].astype(F32) * inv
    ac = (pos % GRID_W)[:, None].astype(F32) * inv
    ang = jnp.concatenate([ar, ar, ac, ac] * 2, axis=-1)
    first = (jnp.arange(V7X_LANES) % 32) < 16
    sin = jnp.sin(ang)
    return jnp.cos(ang), jnp.where(first, -sin, 0.0), jnp.where(first, 0.0, sin)


def _w_in_segments():
    widths = (MLA_Q_RANK, MLA_KV_RANK, MLA_ROPE, SWA_QW, SWA_KW, SWA_KW, MQW, MQW, MVW, N_GATES, MVW)
    src = [0]
    for w in widths:
        src.append(src[-1] + w)
    segs, dst = [], 0
    for i in (0, 1, 3, 4, 5, 6, 7, 8, 10, 2, 9):
        segs.append((src[i], widths[i], dst))
        dst += widths[i]
    return segs, dst


def _permute_w_in(w_in):
    segs, end = _w_in_segments()
    w = lax.optimization_barrier(w_in.astype(BF16))
    cols = [w[..., s0:s0 + n] for s0, n, _ in segs]
    cols.append(jnp.zeros(w.shape[:-1] + (C_END - end,), BF16))
    return jnp.concatenate(cols, axis=-1)


def _permute_w_uq(w_uq):
    L, Rk, _ = w_uq.shape
    w = w_uq.reshape(L, Rk, MLA_HEADS, MLA_NOPE + MLA_ROPE)
    w = jnp.pad(w, ((0, 0), (0, 0), (0, 0), (0, 256 - MLA_NOPE - MLA_ROPE)))
    return w.reshape(L, Rk, MLA_QW).astype(BF16)


def _permute_w_ukv(w_ukv):
    L, Rk, _ = w_ukv.shape
    w = w_ukv.reshape(L, Rk, MLA_HEADS, MLA_NOPE + MLA_V)
    k = w[..., :MLA_NOPE].reshape(L, Rk, MLA_HEADS * MLA_NOPE)
    v = w[..., MLA_NOPE:].reshape(L, Rk, MLA_VW)
    return jnp.concatenate([k, v], axis=-1).astype(BF16)


def _row_tile(rows, want):
    return want if rows % want == 0 else rows


def kernel(x, c, ctx, c_ctx, w_mod, b_mod, g_norm1, g_norm2, w_in, mla_g_q, mla_w_uq, mla_g_kv,
           mla_w_ukv, swa_sink, mlstm_gate_bias, mlstm_g_h, w_out, w_ff1, w_ff2, g_final):
    B, S, D = x.shape
    Lc = ctx.shape[1]
    depth = w_in.shape[0]
    H = MLSTM_HEADS

    rows = -(-(B + 1) // 8) * 8
    c_all = jnp.concatenate([c, c_ctx[None, :], jnp.zeros((rows - B - 1, D), F32)], axis=0)
    mod = _mod_all(c_all, w_mod, b_mod)
    mod6 = mod.reshape(depth, rows, 6, D)

    w_in_p = _permute_w_in(w_in)
    w_uq_p = _permute_w_uq(mla_w_uq)
    w_ukv_p = _permute_w_ukv(mla_w_ukv)
    w_out_b = w_out.astype(BF16)
    w1_b = w_ff1.astype(BF16)
    w2_b = w_ff2.astype(BF16)
    cos, sina, sinb = _rope_tables(S)
    ones_t = jnp.ones((Lc, V7X_LANES), F32)
    zeros_t = jnp.zeros((Lc, V7X_LANES), F32)
    bias_lanes = jnp.pad(mlstm_gate_bias.reshape(depth, 1, N_GATES),
                         ((0, 0), (0, 0), (GATE_LANE0, V7X_LANES - GATE_LANE0 - N_GATES)))
    c_zero = jnp.zeros((B, 2, H * MLSTM_QK, 256), F32)
    m_zero = jnp.zeros((B, 2, H, 8, V7X_LANES), F32)

    tm = _row_tile(S, 512)
    FF = w_ff1.shape[2]
    tm_ffn, tf = _row_tile(S, 512), min(2048, FF)
    tq_mla = tk_mla = _row_tile(S, 2048)
    gfin = g_final[None, :]
    tq_swa = _row_tile(S, 1024)
    chunk = _row_tile(S, 256)

    xs = x.reshape(B * S, D)
    xc = ctx.reshape(B * Lc, D)
    for l in range(depth):
        need_ctx = l < depth - 1
        vec = lambda j: mod6[l, :B, j][:, None, :]
        vecc = lambda j: mod6[l, B:B + 1, j][:, None, :]
        g1 = g_norm1[l][None, :]
        g2 = g_norm2[l][None, :]
        gq = mla_g_q[l][None, :]
        gkv = mla_g_kv[l][None, :]
        gh = mlstm_g_h[l].reshape(1, MVW)
        inproj = functools.partial(_inproj, g1=g1, w_ext=w_in_p, gq=gq, wuq=w_uq_p, gkv=gkv,
                                   wukv=w_ukv_p, l=l)
        ffn = functools.partial(_ffn, gfin=gfin, w1=w1_b, w2=w2_b, l=l, tf=tf)
        (qmc, kmc, vmc, qsc, ksc, vsc, mqc, mkc, mvc, moc, gsc) = inproj(
            xc, vecc(0), vecc(1), cos=ones_t, sina=zeros_t, sinb=zeros_t, tm=Lc)
        hfc, hbc, c_st, m_st = _mlstm(mqc, mkc, mvc, gsc, bias_lanes[l], c_zero, m_zero, B=B, L=Lc)

        (qm, km, vm, qs, ks, vs, mq, mk, mv, mo, gs) = inproj(
            xs, vec(0), vec(1), cos=cos, sina=sina, sinb=sinb, tm=tm)
        y_mla = _mla(qm, kmc, vmc, km, vm, B=B, tq=tq_mla, tk=tk_mla)
        y_swa = _swa(swa_sink[l], qs, ksc, vsc, ks, vs, B=B, tq=tq_swa)
        hf, hb, _, _ = _mlstm(mq, mk, mv, gs, bias_lanes[l], c_st, m_st, B=B, L=chunk)
        xs, h2 = _outproj(y_mla, y_swa, hf, hb, mo, gh, w_out_b, xs, vec(2), vec(3), vec(4), g2, l=l, tm=tm)
        xs = ffn(xs, h2, vec(5), tm=tm_ffn, final=not need_ctx)
        if need_ctx:
            yc_mla = _mla(qmc, kmc, vmc, B=B, tq=Lc, tk=Lc)
            yc_swa = _swa(swa_sink[l], qsc, ksc, vsc, B=B, tq=Lc)
            xc, h2c = _outproj(yc_mla, yc_swa, hfc, hbc, moc, gh, w_out_b, xc, vecc(2), vecc(3), vecc(4), g2,
                               l=l, tm=Lc)
            xc = ffn(xc, h2c, vecc(5), tm=B * Lc, tf=min(512, FF), final=False)
    return xs.reshape(B, S, D)
```
